```python
import math
import jax, jax.numpy as jnp
from jax import lax
import numpy as np

D_MODEL = 1024
BATCH = 4
SEQ = 8192
DEPTH = 2
DEC_BATCH = 8
DEC_SEQ = 2048
PAST_LEN = 128

HEAD_DIM = 64
A_PAIRS = ((128, 1), (512, 4), (2048, 16))
A_GROUPS = 3
A_HEADS = 4
A_WIDTH = A_HEADS * HEAD_DIM
B_Q_HEADS = 8
B_KV_HEADS = 2
B_WINDOW = 128
B_WIDTH = B_Q_HEADS * HEAD_DIM
GRID_W = 64
C_HEADS = 4
C_WIN_R = 8
C_WIN_C = 16
C_COL_BLOCK = 16
C_KEY_COLS = 32
C_WIDTH = C_HEADS * HEAD_DIM
D_WIDTH = 256
SSM_GROUP = 16
SSM_GROUPS = D_WIDTH // SSM_GROUP
SSM_STATE = 64
N_BRANCH = 4
BRANCH_WIDTHS = (A_WIDTH, B_WIDTH, C_WIDTH, D_WIDTH)
BRANCH_OFFSETS = (0, A_WIDTH, A_WIDTH + B_WIDTH, A_WIDTH + B_WIDTH + C_WIDTH, A_WIDTH + B_WIDTH + C_WIDTH + D_WIDTH)
MOE_GROUPS = 4
MOE_EXPERTS_PER_GROUP = 8
N_EXPERTS = MOE_GROUPS * MOE_EXPERTS_PER_GROUP
MOE_TOP_K = 2
MOE_FF = 512
ALPHA = (2 * DEPTH) ** 0.25
BETA = (8 * DEPTH) ** -0.25
LN_EPS = 1e-5
NEG = -1e30
SPLIT_SIZES = (A_GROUPS * A_WIDTH,) * 3 + (B_WIDTH, B_KV_HEADS * HEAD_DIM, B_KV_HEADS * HEAD_DIM) + (C_WIDTH,) * 3 + (D_WIDTH, N_BRANCH * D_MODEL)
SPLIT_POINTS = tuple(sum(SPLIT_SIZES[:i + 1]) for i in range(len(SPLIT_SIZES) - 1))
IN_COLS = sum(SPLIT_SIZES)

kernel_name = 'hybrid_dilated_window_natten_s5_hmoe_encoder'


def layer_norm(x, gain=None, bias=None):
    xf = x.astype(jnp.float32)
    mu = jnp.mean(xf, axis=-1, keepdims=True)
    var = jnp.mean(jnp.square(xf - mu), axis=-1, keepdims=True)
    y = (xf - mu) * lax.rsqrt(var + LN_EPS)
    if gain is not None:
        y = y * gain.astype(jnp.float32) + bias.astype(jnp.float32)
    return y.astype(x.dtype)


def alibi_slopes(n):
    return jnp.asarray([2.0 ** (-8.0 * (i + 1) / n) for i in range(n)], jnp.float32)


def banded_attention(q, k, v, half, block, slopes, dist_scale, sink=None):
    bsz, n_h, seq, hd = q.shape
    n_kv = k.shape[1]
    rep = n_h // n_kv
    nb = -(-seq // block)
    pad = nb * block - seq
    qb = jnp.pad(q, ((0, 0), (0, 0), (0, pad), (0, 0))).reshape(bsz, n_kv, rep, nb, block, hd)

    def windows(t):
        tb = jnp.pad(t, ((0, 0), (0, 0), (block, block + pad), (0, 0))).reshape(bsz, n_kv, nb + 2, block, hd)
        return jnp.concatenate([tb[:, :, :-2], tb[:, :, 1:-1], tb[:, :, 2:]], axis=3)

    kw, vw = windows(k), windows(v)
    s = jnp.einsum('bgrnqd,bgnkd->bgrnqk', qb, kw, preferred_element_type=jnp.float32) * (hd ** -0.5)
    rel = jnp.arange(3 * block)[None, :] - block - jnp.arange(block)[:, None]
    kpos = (jnp.arange(nb)[:, None, None] - 1) * block + jnp.arange(3 * block)[None, None, :]
    valid = (jnp.abs(rel) <= half)[None] & (kpos >= 0) & (kpos < seq)
    dist = jnp.abs(rel).astype(jnp.float32) * dist_scale
    bias = -slopes.astype(jnp.float32).reshape(n_kv, rep)[:, :, None, None, None] * dist
    s = jnp.where(valid, s + bias, NEG)
    m = jnp.max(s, axis=-1)
    if sink is not None:
        sk = sink.astype(jnp.float32).reshape(n_kv, rep)[:, :, None, None]
        m = jnp.maximum(m, sk)
    p = jnp.exp(s - m[..., None])
    den = jnp.sum(p, axis=-1)
    if sink is not None:
        den = den + jnp.exp(sk - m)
    o = jnp.einsum('bgrnqk,bgnkd->bgrnqd', p.astype(v.dtype), vw, preferred_element_type=jnp.float32) / den[..., None]
    o = o.reshape(bsz, n_h, nb * block, hd)[:, :, :seq].astype(q.dtype)
    lse = (m + jnp.log(den)).reshape(bsz, n_h, nb * block)[:, :, :seq]
    return o, lse


def dilated_attention(q, k, v, window, dilation, slopes):
    bsz, seq, n_h, hd = q.shape
    sub = seq // dilation
    half = window // (2 * dilation)

    def fold(t):
        return t.reshape(bsz, sub, dilation, n_h, hd).transpose(0, 3, 2, 1, 4).reshape(bsz, n_h * dilation, sub, hd)

    o, lse = banded_attention(fold(q), fold(k), fold(v), half, half, jnp.repeat(slopes, dilation), dilation)
    o = o.reshape(bsz, n_h, dilation, sub, hd).transpose(0, 3, 2, 1, 4).reshape(bsz, seq, n_h, hd)
    lse = lse.reshape(bsz, n_h, dilation, sub).transpose(0, 3, 2, 1).reshape(bsz, seq, n_h)
    return o, lse


def neighborhood_attention(q, k, v, rpb):
    bsz, seq, n_h, hd = q.shape
    rows = seq // GRID_W
    wr = min(C_WIN_R, rows)

    def to_grid(t):
        return t.reshape(bsz, rows, GRID_W, n_h, hd).transpose(0, 3, 1, 2, 4)

    qg, kg, vg = to_grid(q), to_grid(k), to_grid(v)
    qr = jnp.arange(rows)
    row_idx = jnp.clip(qr - wr // 2, 0, rows - wr)[:, None] + jnp.arange(wr)[None, :]
    row_off = row_idx - qr[:, None] + (C_WIN_R - 1)
    k_rows = kg[:, :, row_idx]
    v_rows = vg[:, :, row_idx]
    col_start = jnp.clip(jnp.arange(GRID_W) - C_WIN_C // 2, 0, GRID_W - C_WIN_C)
    outs = []
    for c0 in range(0, GRID_W, C_COL_BLOCK):
        kc0 = min(max(c0 - (C_KEY_COLS - C_COL_BLOCK) // 2, 0), GRID_W - C_KEY_COLS)
        qcols = c0 + jnp.arange(C_COL_BLOCK)
        kcols = kc0 + jnp.arange(C_KEY_COLS)
        cs = col_start[qcols][:, None]
        valid = (kcols[None, :] >= cs) & (kcols[None, :] < cs + C_WIN_C)
        col_off = jnp.clip(kcols[None, :] - qcols[:, None] + (C_WIN_C - 1), 0, 2 * C_WIN_C - 2)
        bias = rpb[:, row_off[:, None, :, None], col_off[None, :, None, :]].astype(jnp.float32)
        s = jnp.einsum('bhrqd,bhrikd->bhrqik', qg[:, :, :, c0:c0 + C_COL_BLOCK],
                       k_rows[:, :, :, :, kc0:kc0 + C_KEY_COLS], preferred_element_type=jnp.float32) * (hd ** -0.5)
        s = jnp.where(valid[:, None, :], s + bias, NEG)
        p = jax.nn.softmax(s.reshape(s.shape[:-2] + (-1,)), axis=-1).reshape(s.shape)
        outs.append(jnp.einsum('bhrqik,bhrikd->bhrqd', p.astype(v.dtype), v_rows[:, :, :, :, kc0:kc0 + C_KEY_COLS]))
    o = jnp.concatenate(outs, axis=3)
    return o.transpose(0, 2, 3, 1, 4).reshape(bsz, seq, n_h * hd)


def _ssm_combine(e1, e2):
    a1, b1 = e1
    a2, b2 = e2
    return a1 * a2, a2 * b1 + b2


def s5_bidirectional(u, lam_re, lam_im, log_dt, b_re, b_im, c_re, c_im, d_skip):
    bsz, seq, width = u.shape
    uf = u.astype(jnp.float32).reshape(bsz, seq, SSM_GROUPS, SSM_GROUP)
    uc = uf.astype(jnp.complex64)
    y = d_skip.astype(jnp.float32).reshape(SSM_GROUPS, SSM_GROUP) * uf
    for direction in range(2):
        lam = lax.complex(lam_re[direction].astype(jnp.float32), lam_im[direction].astype(jnp.float32))
        dt = jnp.exp(log_dt[direction].astype(jnp.float32))[:, None]
        lam_bar = jnp.exp(lam * dt)
        b_mat = lax.complex(b_re[direction].astype(jnp.float32), b_im[direction].astype(jnp.float32))
        b_bar = ((lam_bar - 1.0) / lam)[..., None] * b_mat
        bu = jnp.einsum('bsgh,gph->bsgp', uc, b_bar)
        a = jnp.broadcast_to(lam_bar, bu.shape)
        _, states = lax.associative_scan(_ssm_combine, (a, bu), reverse=(direction == 1), axis=1)
        c_mat = lax.complex(c_re[direction].astype(jnp.float32), c_im[direction].astype(jnp.float32))
        y = y + jnp.real(jnp.einsum('bsgp,ghp->bsgh', states, c_mat))
    return y.reshape(bsz, seq, width).astype(u.dtype)


def mixer_block(h, w_in, b_gate, b_sink, c_rpb, lam_re, lam_im, log_dt, ssm_b_re, ssm_b_im,
                ssm_c_re, ssm_c_im, ssm_d, w_glu, b_glu, w_branch, w_out):
    bsz, seq, _ = h.shape
    a_q, a_k, a_v, b_q, b_k, b_v, c_q, c_k, c_v, d_u, gates = jnp.split(h @ w_in, SPLIT_POINTS, axis=-1)

    def grp(t):
        return t.reshape(bsz, seq, A_GROUPS, A_HEADS, HEAD_DIM)
    a_q, a_k, a_v = grp(a_q), grp(a_k), grp(a_v)
    slopes_a = alibi_slopes(A_GROUPS * A_HEADS).reshape(A_GROUPS, A_HEADS)
    outs, lses = [], []
    for g, (window, dilation) in enumerate(A_PAIRS):
        o, lse = dilated_attention(a_q[:, :, g], a_k[:, :, g], a_v[:, :, g], window, dilation, slopes_a[g])
        outs.append(o.astype(jnp.float32))
        lses.append(lse)
    mix_w = jax.nn.softmax(jnp.stack(lses), axis=0)[..., None]
    y_a = jnp.sum(mix_w * jnp.stack(outs), axis=0).reshape(bsz, seq, A_WIDTH).astype(h.dtype)

    def bhsd(t, n):
        return t.reshape(bsz, seq, n, HEAD_DIM).transpose(0, 2, 1, 3)
    o_b, _ = banded_attention(bhsd(b_q, B_Q_HEADS), bhsd(b_k, B_KV_HEADS), bhsd(b_v, B_KV_HEADS),
                              B_WINDOW, B_WINDOW, alibi_slopes(B_Q_HEADS), 1, sink=b_sink)
    y_b = o_b.transpose(0, 2, 1, 3).reshape(bsz, seq, B_WIDTH)

    def bshd(t):
        return t.reshape(bsz, seq, C_HEADS, HEAD_DIM)
    y_c = neighborhood_attention(bshd(c_q), bshd(c_k), bshd(c_v), c_rpb)

    y_d = jax.nn.gelu(s5_bidirectional(d_u, lam_re, lam_im, log_dt, ssm_b_re, ssm_b_im, ssm_c_re, ssm_c_im, ssm_d))
    z = y_d @ w_glu + b_glu
    y_d = z[..., :D_WIDTH] * jax.nn.sigmoid(z[..., D_WIDTH:])

    gate = jax.nn.sigmoid((gates + b_gate).astype(jnp.float32)).astype(h.dtype).reshape(bsz, seq, N_BRANCH, D_MODEL)
    merged = sum(gate[:, :, i] * (y @ w_branch[BRANCH_OFFSETS[i]:BRANCH_OFFSETS[i + 1]])
                 for i, y in enumerate((y_a, y_b, y_c, y_d)))
    return merged @ w_out


def moe_block(h, w_route_g, b_route_g, w_route_e, b_route_e, w_up, w_down):
    bsz, seq, d = h.shape
    t = h.reshape(-1, d)
    g_logits = (t @ w_route_g + b_route_g).astype(jnp.float32)
    g_prob = jax.nn.softmax(g_logits, axis=-1)
    g_sel = jnp.argmax(g_logits, axis=-1)
    g_w = jnp.take_along_axis(g_prob, g_sel[:, None], axis=-1)
    e_logits = (t @ w_route_e + b_route_e).astype(jnp.float32).reshape(-1, MOE_GROUPS, MOE_EXPERTS_PER_GROUP)
    e_logits = jnp.take_along_axis(e_logits, g_sel[:, None, None], axis=1)[:, 0]
    top_v, top_i = lax.top_k(e_logits, MOE_TOP_K)
    top_w = jax.nn.softmax(top_v, axis=-1) * g_w
    eid = g_sel[:, None] * MOE_EXPERTS_PER_GROUP + top_i
    combine = jnp.sum(jax.nn.one_hot(eid, N_EXPERTS, dtype=jnp.float32) * top_w[..., None], axis=1).astype(h.dtype)
    y = jnp.zeros_like(t)
    for e in range(N_EXPERTS):
        a = t @ w_up[e]
        hid = jax.nn.silu(a[:, :MOE_FF]) * a[:, MOE_FF:]
        y = y + combine[:, e:e + 1] * (hid @ w_down[e])
    return y.reshape(bsz, seq, d)


def setup_inputs(seed: int = 0) -> dict:
    key = jax.random.key(seed)
    ks = iter(jax.random.split(key, 48))
    f32 = jnp.float32

    def nrm(shape, scale):
        return scale * jax.random.normal(next(ks), shape, f32)

    n_idx = jnp.arange(SSM_STATE, dtype=f32)
    ssm_shape = (DEPTH, 2, SSM_GROUPS, SSM_STATE)
    return {
        'x_prompt': nrm((BATCH, SEQ, D_MODEL), 1.0),
        'x_sample': nrm((DEC_BATCH, DEC_SEQ, D_MODEL), 1.0),
        'c_prompt': nrm((BATCH, D_MODEL), 1.0),
        'c_sample': nrm((DEC_BATCH, D_MODEL), 1.0),
        'w_ada': nrm((DEPTH, D_MODEL, 6 * D_MODEL), 0.1 * D_MODEL ** -0.5),
        'b_ada': nrm((DEPTH, 6 * D_MODEL), 0.01),
        'w_in': nrm((DEPTH, D_MODEL, IN_COLS), D_MODEL ** -0.5),
        'b_gate': nrm((DEPTH, N_BRANCH * D_MODEL), 0.1),
        'b_sink': nrm((DEPTH, B_Q_HEADS), 1.0),
        'c_rpb': nrm((DEPTH, C_HEADS, 2 * C_WIN_R - 1, 2 * C_WIN_C - 1), 0.1),
        'lam_re': -0.5 + nrm(ssm_shape, 0.01),
        'lam_im': math.pi * n_idx + nrm(ssm_shape, 0.01),
        'log_dt': jax.random.uniform(next(ks), (DEPTH, 2, SSM_GROUPS), f32, math.log(1e-3), math.log(1e-1)),
        'ssm_b_re': nrm((DEPTH, 2, SSM_GROUPS, SSM_STATE, SSM_GROUP), (2 * SSM_GROUP) ** -0.5),
        'ssm_b_im': nrm((DEPTH, 2, SSM_GROUPS, SSM_STATE, SSM_GROUP), (2 * SSM_GROUP) ** -0.5),
        'ssm_c_re': nrm((DEPTH, 2, SSM_GROUPS, SSM_GROUP, SSM_STATE), (2 * SSM_STATE) ** -0.5),
        'ssm_c_im': nrm((DEPTH, 2, SSM_GROUPS, SSM_GROUP, SSM_STATE), (2 * SSM_STATE) ** -0.5),
        'ssm_d': nrm((DEPTH, D_WIDTH), 1.0),
        'w_glu': nrm((DEPTH, D_WIDTH, 2 * D_WIDTH), D_WIDTH ** -0.5),
        'b_glu': nrm((DEPTH, 2 * D_WIDTH), 0.01),
        'w_branch': jnp.concatenate([nrm((DEPTH, wd, D_MODEL), wd ** -0.5) for wd in BRANCH_WIDTHS], axis=1),
        'w_out': nrm((DEPTH, D_MODEL, D_MODEL), BETA * D_MODEL ** -0.5),
        'ln1_g': 1.0 + nrm((DEPTH, D_MODEL), 0.01),
        'ln1_b': nrm((DEPTH, D_MODEL), 0.01),
        'ln2_g': 1.0 + nrm((DEPTH, D_MODEL), 0.01),
        'ln2_b': nrm((DEPTH, D_MODEL), 0.01),
        'w_route_g': nrm((DEPTH, D_MODEL, MOE_GROUPS), D_MODEL ** -0.5),
        'b_route_g': nrm((DEPTH, MOE_GROUPS), 0.01),
        'w_route_e': nrm((DEPTH, D_MODEL, N_EXPERTS), D_MODEL ** -0.5),
        'b_route_e': nrm((DEPTH, N_EXPERTS), 0.01),
        'w_up': nrm((DEPTH, N_EXPERTS, D_MODEL, 2 * MOE_FF), D_MODEL ** -0.5),
        'w_down': nrm((DEPTH, N_EXPERTS, MOE_FF, D_MODEL), BETA * MOE_FF ** -0.5),
    }


def reference(x_prompt, x_sample, c_prompt, c_sample, w_ada, b_ada, w_in, b_gate, b_sink, c_rpb,
              lam_re, lam_im, log_dt, ssm_b_re, ssm_b_im, ssm_c_re, ssm_c_im, ssm_d, w_glu, b_glu,
              w_branch, w_out, ln1_g, ln1_b, ln2_g, ln2_b, w_route_g, b_route_g, w_route_e, b_route_e,
              w_up, w_down):
    def trunk(x, c):
        for l in range(DEPTH):
            ada = (jax.nn.silu(c) @ w_ada[l] + b_ada[l])[:, None, :]
            shift1, scale1, gate1, shift2, scale2, gate2 = jnp.split(ada, 6, axis=-1)
            h = layer_norm(x) * (1.0 + scale1) + shift1
            mix = mixer_block(h, w_in[l], b_gate[l], b_sink[l], c_rpb[l], lam_re[l], lam_im[l], log_dt[l],
                              ssm_b_re[l], ssm_b_im[l], ssm_c_re[l], ssm_c_im[l], ssm_d[l], w_glu[l], b_glu[l],
                              w_branch[l], w_out[l])
            x = layer_norm(ALPHA * x + (1.0 + gate1) * mix, ln1_g[l], ln1_b[l])
            h = layer_norm(x) * (1.0 + scale2) + shift2
            ffn = moe_block(h, w_route_g[l], b_route_g[l], w_route_e[l], b_route_e[l], w_up[l], w_down[l])
            x = layer_norm(ALPHA * x + (1.0 + gate2) * ffn, ln2_g[l], ln2_b[l])
        return x

    y_prompt = trunk(x_prompt, c_prompt)
    y_sample = trunk(x_sample, c_sample)
    return (y_prompt, y_sample)
```

```python
import functools
import math

import jax
import jax.numpy as jnp
from jax import lax
from jax.experimental import pallas as pl
from jax.experimental.pallas import tpu as pltpu

F32 = jnp.float32
BF16 = jnp.bfloat16
HIGHEST = lax.Precision.HIGHEST

D_MODEL = 1024
DEPTH = 2
HEAD_DIM = 64
A_PAIRS = ((128, 1), (512, 4), (2048, 16))
A_GROUPS = 3
A_HEADS = 4
A_WIDTH = A_HEADS * HEAD_DIM
B_Q_HEADS = 8
B_KV_HEADS = 2
B_WINDOW = 128
B_WIDTH = B_Q_HEADS * HEAD_DIM
GRID_W = 64
C_HEADS = 4
C_WIN_R = 8
C_WIN_C = 16
C_WIDTH = C_HEADS * HEAD_DIM
D_WIDTH = 256
SSM_GROUP = 16
SSM_GROUPS = D_WIDTH // SSM_GROUP
SSM_STATE = 64
N_BRANCH = 4
MOE_GROUPS = 4
MOE_EPG = 8
N_EXPERTS = MOE_GROUPS * MOE_EPG
MOE_FF = 512
ALPHA = (2 * DEPTH) ** 0.25
LN_EPS = 1e-5
NEG = -1e30

QKV_COLS = 4096
COL_BQ = 0
COL_A = 2
COL_BKV = 11
COL_CQ = 12
COL_DU = 15
A_HALF = 64
SSM_CHUNK = 64
SSM_CW = SSM_CHUNK * SSM_GROUP
MOE_TM = 256
VMEM_LIMIT = 56 * 1024 * 1024


def _cparams(sem):
    return pltpu.CompilerParams(dimension_semantics=sem, vmem_limit_bytes=VMEM_LIMIT)


def _sigmoid(x):
    return 1.0 / (1.0 + jnp.exp(-x))


def _ln(x):
    mu = jnp.mean(x, axis=-1, keepdims=True)
    xc = x - mu
    var = jnp.mean(xc * xc, axis=-1, keepdims=True)
    return xc * lax.rsqrt(var + LN_EPS)


def _dot_nt(a, b, **kw):
    return lax.dot_general(a, b, (((1,), (1,)), ((), ())), preferred_element_type=F32, **kw)


def _alibi(n):
    return [2.0 ** (-8.0 * (i + 1) / n) for i in range(n)]


def _ada_kernel(c_ref, w_ref, b_ref, o_ref):
    c = c_ref[...]
    sc = c * _sigmoid(c)
    o_ref[0] = jnp.dot(sc, w_ref[0], precision=HIGHEST, preferred_element_type=F32) + b_ref[0]


def _ada_call(c_all, w_ada, b_ada):
    nb = c_all.shape[0]
    return pl.pallas_call(
        _ada_kernel,
        grid=(DEPTH, 6),
        in_specs=[
            pl.BlockSpec((nb, D_MODEL), lambda l, j: (0, 0)),
            pl.BlockSpec((1, D_MODEL, D_MODEL), lambda l, j: (l, 0, j)),
            pl.BlockSpec((1, 1, D_MODEL), lambda l, j: (l, 0, j)),
        ],
        out_specs=pl.BlockSpec((1, nb, D_MODEL), lambda l, j: (l, 0, j)),
        out_shape=jax.ShapeDtypeStruct((DEPTH, nb, 6 * D_MODEL), F32),
        compiler_params=_cparams(("arbitrary", "arbitrary")),
        name="ada",
    )(c_all, w_ada, b_ada.reshape(DEPTH, 1, 6 * D_MODEL))


def _proj_kernel(x_ref, ada_ref, w_ref, h_ref, qkv_ref, *, chunk):
    x = x_ref[0]
    ada = ada_ref[0]
    h = _ln(x) * (1.0 + ada[1:2]) + ada[0:1]
    hb = h.astype(BF16)
    h_ref[0] = hb
    for j in range(QKV_COLS // chunk):
        sl = slice(j * chunk, (j + 1) * chunk)
        qkv_ref[0, :, sl] = jnp.dot(hb, w_ref[:, sl], preferred_element_type=F32).astype(BF16)


def _proj_call(x, ada8, w_qkv, tm):
    bsz, seq, _ = x.shape
    return pl.pallas_call(
        functools.partial(_proj_kernel, chunk=512),
        grid=(bsz, seq // tm),
        in_specs=[
            pl.BlockSpec((1, tm, D_MODEL), lambda b, i: (b, i, 0)),
            pl.BlockSpec((1, 8, D_MODEL), lambda b, i: (b, 0, 0)),
            pl.BlockSpec((D_MODEL, QKV_COLS), lambda b, i: (0, 0)),
        ],
        out_specs=[
            pl.BlockSpec((1, tm, D_MODEL), lambda b, i: (b, i, 0)),
            pl.BlockSpec((1, tm, QKV_COLS), lambda b, i: (b, i, 0)),
        ],
        out_shape=[
            jax.ShapeDtypeStruct((bsz, seq, D_MODEL), BF16),
            jax.ShapeDtypeStruct((bsz, seq, QKV_COLS), BF16),
        ],
        compiler_params=_cparams(("parallel", "parallel")),
        name="proj",
    )(x, ada8, w_qkv)


def _attn_a_kernel(q_ref, kc_ref, kp_ref, kn_ref, vc_ref, vp_ref, vn_ref, o_ref, lse_ref,
                   *, tq, sub, slopes, dil):
    i = pl.program_id(2)
    q = q_ref[0]
    k = jnp.concatenate([kp_ref[0], kc_ref[0], kn_ref[0]], axis=0)
    v = jnp.concatenate([vp_ref[0], vc_ref[0], vn_ref[0]], axis=0)
    sq = 128
    kw = sq + 2 * A_HALF
    row = lax.broadcasted_iota(jnp.int32, (sq, kw), 0)
    col = lax.broadcasted_iota(jnp.int32, (sq, kw), 1)
    rel = col - A_HALF - row
    dist = jnp.abs(rel).astype(F32) * float(dil)
    for j in range(tq // sq):
        kpos = i * tq + j * sq - A_HALF + col
        valid = (jnp.abs(rel) <= A_HALF) & (kpos >= 0) & (kpos < sub)
        qj = q[j * sq:(j + 1) * sq]
        kj = k[j * sq:j * sq + kw]
        vj = v[j * sq:j * sq + kw]
        for h in range(A_HEADS):
            hs = slice(h * HEAD_DIM, (h + 1) * HEAD_DIM)
            s = _dot_nt(qj[:, hs], kj[:, hs])
            s = jnp.where(valid, s - slopes[h] * dist, NEG)
            m = jnp.max(s, axis=-1, keepdims=True)
            p = jnp.exp(s - m)
            den = jnp.sum(p, axis=-1, keepdims=True)
            o = jnp.dot(p.astype(BF16), vj[:, hs], preferred_element_type=F32) / den
            o_ref[0, j * sq:(j + 1) * sq, hs] = o.astype(BF16)
            lse_ref[0, j * sq:(j + 1) * sq, hs] = jnp.broadcast_to(m + jnp.log(den), (sq, HEAD_DIM))


def _attn_a_call(qkv, g):
    bsz, seq, _ = qkv.shape
    _, dil = A_PAIRS[g]
    sub = seq // dil
    tq = min(512, sub)
    nq = sub // tq
    hb = tq // A_HALF
    nhb = sub // A_HALF
    nblk = QKV_COLS // 256
    folded = qkv.reshape(bsz, sub, dil * QKV_COLS)
    cq = COL_A + 3 * g
    slopes = _alibi(A_GROUPS * A_HEADS)[g * A_HEADS:(g + 1) * A_HEADS]

    def cur(c):
        return pl.BlockSpec((1, tq, 256), lambda b, r, i: (b, i, r * nblk + c))

    def prev(c):
        return pl.BlockSpec((1, A_HALF, 256), lambda b, r, i: (b, jnp.maximum(i * hb - 1, 0), r * nblk + c))

    def nxt(c):
        return pl.BlockSpec((1, A_HALF, 256), lambda b, r, i: (b, jnp.minimum((i + 1) * hb, nhb - 1), r * nblk + c))

    o, lse = pl.pallas_call(
        functools.partial(_attn_a_kernel, tq=tq, sub=sub, slopes=slopes, dil=dil),
        grid=(bsz, dil, nq),
        in_specs=[cur(cq), cur(cq + 1), prev(cq + 1), nxt(cq + 1), cur(cq + 2), prev(cq + 2), nxt(cq + 2)],
        out_specs=[
            pl.BlockSpec((1, tq, A_WIDTH), lambda b, r, i: (b, i, r)),
            pl.BlockSpec((1, tq, A_WIDTH), lambda b, r, i: (b, i, r)),
        ],
        out_shape=[
            jax.ShapeDtypeStruct((bsz, sub, dil * A_WIDTH), BF16),
            jax.ShapeDtypeStruct((bsz, sub, dil * A_WIDTH), F32),
        ],
        compiler_params=_cparams(("parallel", "parallel", "parallel")),
        name=f"attn_a{g}",
    )(folded, folded, folded, folded, folded, folded, folded)
    return o.reshape(bsz, seq, A_WIDTH), lse.reshape(bsz, seq, A_WIDTH)


def _attn_b_kernel(sink_ref, q_ref, kvc_ref, kvp_ref, kvn_ref, o_ref, *, tq, seq, slopes):
    i = pl.program_id(1)
    q = q_ref[0]
    kv = jnp.concatenate([kvp_ref[0], kvc_ref[0], kvn_ref[0]], axis=0)
    sq = 128
    kw = sq + 2 * B_WINDOW
    row = lax.broadcasted_iota(jnp.int32, (sq, kw), 0)
    col = lax.broadcasted_iota(jnp.int32, (sq, kw), 1)
    rel = col - B_WINDOW - row
    dist = jnp.abs(rel).astype(F32)
    rep = B_Q_HEADS // B_KV_HEADS
    for j in range(tq // sq):
        kpos = i * tq + j * sq - B_WINDOW + col
        valid = (jnp.abs(rel) <= B_WINDOW) & (kpos >= 0) & (kpos < seq)
        qj = q[j * sq:(j + 1) * sq]
        kvj = kv[j * sq:j * sq + kw]
        for h in range(B_Q_HEADS):
            g = h // rep
            kh = kvj[:, g * HEAD_DIM:(g + 1) * HEAD_DIM]
            vh = kvj[:, (B_KV_HEADS + g) * HEAD_DIM:(B_KV_HEADS + g + 1) * HEAD_DIM]
            s = _dot_nt(qj[:, h * HEAD_DIM:(h + 1) * HEAD_DIM], kh)
            s = jnp.where(valid, s - slopes[h] * dist, NEG)
            sk = sink_ref[h]
            m = jnp.maximum(jnp.max(s, axis=-1, keepdims=True), sk)
            p = jnp.exp(s - m)
            den = jnp.sum(p, axis=-1, keepdims=True) + jnp.exp(sk - m)
            o = jnp.dot(p.astype(BF16), vh, preferred_element_type=F32) / den
            o_ref[0, j * sq:(j + 1) * sq, h * HEAD_DIM:(h + 1) * HEAD_DIM] = o.astype(BF16)


def _attn_b_call(qkv, sink):
    bsz, seq, _ = qkv.shape
    tq = 512
    hb = tq // B_WINDOW
    nhb = seq // B_WINDOW
    return pl.pallas_call(
        functools.partial(_attn_b_kernel, tq=tq, seq=seq, slopes=_alibi(B_Q_HEADS)),
        grid=(bsz, seq // tq),
        in_specs=[
            pl.BlockSpec(memory_space=pltpu.SMEM),
            pl.BlockSpec((1, tq, B_WIDTH), lambda b, i: (b, i, COL_BQ)),
            pl.BlockSpec((1, tq, 256), lambda b, i: (b, i, COL_BKV)),
            pl.BlockSpec((1, B_WINDOW, 256), lambda b, i: (b, jnp.maximum(i * hb - 1, 0), COL_BKV)),
            pl.BlockSpec((1, B_WINDOW, 256), lambda b, i: (b, jnp.minimum((i + 1) * hb, nhb - 1), COL_BKV)),
        ],
        out_specs=pl.BlockSpec((1, tq, B_WIDTH), lambda b, i: (b, i, 0)),
        out_shape=jax.ShapeDtypeStruct((bsz, seq, B_WIDTH), BF16),
        compiler_params=_cparams(("parallel", "parallel")),
        name="attn_b",
    )(sink, qkv, qkv, qkv, qkv)


C_ROWS_PER_STEP = 8


def _attn_c_kernel(q_ref, k_ref, v_ref, bias_ref, o_ref, *, rows):
    i = pl.program_id(1)
    kwin = C_WIN_R * GRID_W
    for jr in range(C_ROWS_PER_STEP):
        r = i * C_ROWS_PER_STEP + jr
        rs = jnp.clip(r - C_WIN_R // 2, 0, rows - C_WIN_R)
        delta = r - rs
        start = pl.multiple_of(rs * GRID_W, GRID_W)
        kw = k_ref[0, pl.ds(start, kwin), :]
        vw = v_ref[0, pl.ds(start, kwin), :]
        qr = q_ref[0, jr * GRID_W:(jr + 1) * GRID_W, :]
        for h in range(C_HEADS):
            hs = slice(h * HEAD_DIM, (h + 1) * HEAD_DIM)
            s = _dot_nt(qr[:, hs], kw[:, hs]) + bias_ref[h, delta]
            m = jnp.max(s, axis=-1, keepdims=True)
            p = jnp.exp(s - m)
            den = jnp.sum(p, axis=-1, keepdims=True)
            o = jnp.dot(p.astype(BF16), vw[:, hs], preferred_element_type=F32) / den
            o_ref[0, jr * GRID_W:(jr + 1) * GRID_W, hs] = o.astype(BF16)


def _c_bias_table(rpb):
    delta = jnp.arange(C_WIN_R)[:, None, None, None]
    qc = jnp.arange(GRID_W)[None, :, None, None]
    ki = jnp.arange(C_WIN_R)[None, None, :, None]
    kc = jnp.arange(GRID_W)[None, None, None, :]
    cs = jnp.clip(qc - C_WIN_C // 2, 0, GRID_W - C_WIN_C)
    valid = (kc >= cs) & (kc < cs + C_WIN_C)
    row_off = jnp.broadcast_to(ki - delta + (C_WIN_R - 1), (C_WIN_R, GRID_W, C_WIN_R, GRID_W))
    col_off = jnp.broadcast_to(jnp.clip(kc - qc + (C_WIN_C - 1), 0, 2 * C_WIN_C - 2), row_off.shape)
    tab = rpb.astype(F32)[:, row_off, col_off]
    tab = jnp.where(valid[None], tab, NEG)
    return tab.reshape(C_HEADS, C_WIN_R, GRID_W, C_WIN_R * GRID_W)


def _attn_c_call(qkv, bias_tab):
    bsz, seq, _ = qkv.shape
    rows = seq // GRID_W
    tq = C_ROWS_PER_STEP * GRID_W
    return pl.pallas_call(
        functools.partial(_attn_c_kernel, rows=rows),
        grid=(bsz, seq // tq),
        in_specs=[
            pl.BlockSpec((1, tq, C_WIDTH), lambda b, i: (b, i, COL_CQ)),
            pl.BlockSpec((1, seq, C_WIDTH), lambda b, i: (b, 0, COL_CQ + 1)),
            pl.BlockSpec((1, seq, C_WIDTH), lambda b, i: (b, 0, COL_CQ + 2)),
            pl.BlockSpec(bias_tab.shape, lambda b, i: (0, 0, 0, 0)),
        ],
        out_specs=pl.BlockSpec((1, tq, C_WIDTH), lambda b, i: (b, i, 0)),
        out_shape=jax.ShapeDtypeStruct((bsz, seq, C_WIDTH), BF16),
        compiler_params=_cparams(("parallel", "arbitrary")),
        name="attn_c",
    )(qkv, qkv, qkv, bias_tab)


def _cmul(ar, ai, br, bi):
    return ar * br - ai * bi, ar * bi + ai * br


def _ssm_param_kernel(lre_ref, lim_ref, ldt_ref, cre_ref, cim_ref, bre_ref, bim_ref, d_ref,
                      m_ref, bend_ref, cin_ref, apow_ref):
    tc, hh, pp = SSM_CHUNK, SSM_GROUP, SSM_STATE
    it = lax.broadcasted_iota(jnp.int32, (tc, pp), 0).astype(F32)
    lane = lax.broadcasted_iota(jnp.int32, (hh, 128), 1)
    subl = lax.broadcasted_iota(jnp.int32, (hh, 128), 0)
    tabs = []
    for d in range(2):
        lr, li = lre_ref[d, 0, 0], lim_ref[d, 0, 0]
        dt = jnp.exp(ldt_ref[d, 0, 0])

        def powers(kv):
            mag = jnp.exp(kv * (lr * dt))
            ang = kv * (li * dt)
            return mag * jnp.cos(ang), mag * jnp.sin(ang)

        one = jnp.ones((1, pp), F32)
        lbr, lbi = powers(one)
        den = lr * lr + li * li
        nr, ni = lbr - 1.0, lbi
        fr = (nr * lr + ni * li) / den
        fi = (ni * lr - nr * li) / den
        apr, api = powers(one * float(tc))
        apow_ref[d, 0] = jnp.concatenate([apr, api], axis=1)

        cr, ci = cre_ref[d, 0], cim_ref[d, 0]
        br, bi = bre_ref[d, 0], bim_ref[d, 0]

        wr, wi = powers(it + 1.0 if d == 0 else float(tc) - it)
        cwr, cwi = _cmul(cr[None], ci[None], wr[:, None, :], wi[:, None, :])
        cin = jnp.concatenate([cwr, -cwi], axis=2).reshape(tc * hh, 2 * pp)
        cin_ref[d, 0] = cin.astype(BF16)

        wr, wi = powers(float(tc - 1) - it if d == 0 else it)
        wfr, wfi = _cmul(wr, wi, fr, fi)
        er, ei = _cmul(br[None], bi[None], wfr[:, None, :], wfi[:, None, :])
        bend_ref[d, 0] = jnp.concatenate([er, ei], axis=2).reshape(tc * hh, 2 * pp).astype(BF16)

        wr, wi = powers(it if d == 0 else float(tc - 1) - it)
        wfr, wfi = _cmul(wr, wi, fr, fi)
        gr, gi = _cmul(cr[None], ci[None], wfr[:, None, :], wfi[:, None, :])
        gr = gr.reshape(tc * hh, pp)
        gi = gi.reshape(tc * hh, pp)
        tabs.append(_dot_nt(br, gr, precision=HIGHEST) - _dot_nt(bi, gi, precision=HIGHEST))

    ktf, ktb = tabs
    last = pltpu.roll(ktb[:, SSM_CW - 128:], hh, axis=1)
    diag = jnp.where(lane == subl, d_ref[0], 0.0)
    first = ktf[:, :128] + jnp.where(lane < hh, last + diag, 0.0)
    ktf = jnp.concatenate([first, ktf[:, 128:]], axis=1)
    lane_w = lax.broadcasted_iota(jnp.int32, (hh, SSM_CW), 1)
    ktb = jnp.where(lane_w < SSM_CW - hh, ktb, 0.0)
    zeros = jnp.zeros((hh, SSM_CW), F32)
    pf = jnp.concatenate([ktf, zeros], axis=1)
    pb = jnp.concatenate([ktb, zeros], axis=1)
    for j in range(tc):
        blk = pf if j == 0 else pltpu.roll(pf, j * hh, axis=1)
        sh = (tc - 1 - j) * hh
        blb = pb if sh == 0 else pltpu.roll(pb, 2 * SSM_CW - sh, axis=1)
        m_ref[0, j * hh:(j + 1) * hh, :] = (blk[:, :SSM_CW] + blb[:, :SSM_CW]).astype(BF16)


def _ssm_param_call(lam_re, lam_im, log_dt, b_re, b_im, c_re, c_im, d_skip):
    gg, pp, hh = SSM_GROUPS, SSM_STATE, SSM_GROUP
    nl = lam_re.shape[0]
    lre = lam_re.reshape(nl * 2, gg, 1, 1, pp)
    lim = lam_im.reshape(nl * 2, gg, 1, 1, pp)
    ldt = jnp.broadcast_to(log_dt.reshape(nl * 2, gg, 1, 1, 1), (nl * 2, gg, 1, 1, pp))
    cre = c_re.reshape(nl * 2, gg, hh, pp)
    cim = c_im.reshape(nl * 2, gg, hh, pp)
    bre = jnp.swapaxes(b_re, -1, -2).reshape(nl * 2, gg, hh, pp)
    bim = jnp.swapaxes(b_im, -1, -2).reshape(nl * 2, gg, hh, pp)
    dpad = jnp.pad(d_skip.reshape(nl * gg, 1, hh), ((0, 0), (0, 0), (0, 128 - hh)))

    def vec(l, g):
        return (l, g, 0, 0, 0)

    def mat(l, g):
        return (l, g, 0, 0)

    return pl.pallas_call(
        _ssm_param_kernel,
        grid=(nl, gg),
        in_specs=[
            pl.BlockSpec((2, 1, 1, 1, pp), vec), pl.BlockSpec((2, 1, 1, 1, pp), vec),
            pl.BlockSpec((2, 1, 1, 1, pp), vec),
            pl.BlockSpec((2, 1, hh, pp), mat), pl.BlockSpec((2, 1, hh, pp), mat),
            pl.BlockSpec((2, 1, hh, pp), mat), pl.BlockSpec((2, 1, hh, pp), mat),
            pl.BlockSpec((1, 1, 128), lambda l, g: (l * gg + g, 0, 0)),
        ],
        out_specs=[
            pl.BlockSpec((1, SSM_CW, SSM_CW), lambda l, g: (l * gg + g, 0, 0)),
            pl.BlockSpec((2, 1, SSM_CW, 2 * pp), mat),
            pl.BlockSpec((2, 1, SSM_CW, 2 * pp), mat),
            pl.BlockSpec((2, 1, 1, 2 * pp), mat),
        ],
        out_shape=[
            jax.ShapeDtypeStruct((nl * gg, SSM_CW, SSM_CW), BF16),
            jax.ShapeDtypeStruct((nl * 2, gg, SSM_CW, 2 * pp), BF16),
            jax.ShapeDtypeStruct((nl * 2, gg, SSM_CW, 2 * pp), BF16),
            jax.ShapeDtypeStruct((nl * 2, gg, 1, 2 * pp), F32),
        ],
        compiler_params=_cparams(("arbitrary", "arbitrary")),
        name="ssm_param",
    )(lre, lim, ldt, cre, cim, bre, bim, dpad)


def _ssm_state_kernel(u_ref, bend_ref, s_ref):
    for g in range(SSM_GROUPS):
        u = u_ref[0, g]
        for d in range(2):
            s_ref[0, d, :, g * 128:(g + 1) * 128] = jnp.dot(u, bend_ref[d, g], preferred_element_type=F32)


def _ssm_scan_kernel(s_ref, apow_ref, x_ref, *, nc):
    width = SSM_GROUPS * 128
    lane = lax.broadcasted_iota(jnp.int32, (1, width), 1)
    first_half = (lane % 128) < SSM_STATE
    for d in range(2):
        a = apow_ref[d]
        a_re = jnp.where(first_half, a, pltpu.roll(a, SSM_STATE, axis=1))
        a_im = jnp.where(first_half, -pltpu.roll(a, width - SSM_STATE, axis=1), a)

        def body(c, x):
            cc = c if d == 0 else nc - 1 - c
            x_ref[0, d, pl.ds(cc, 1), :] = x
            x_sw_lo = pltpu.roll(x, SSM_STATE, axis=1)
            x_sw_hi = pltpu.roll(x, width - SSM_STATE, axis=1)
            x_sw = jnp.where(first_half, x_sw_hi, x_sw_lo)
            return a_re * x + a_im * x_sw + s_ref[0, d, pl.ds(cc, 1), :]

        lax.fori_loop(0, nc, body, jnp.zeros((1, width), F32))


def _ssm_out_kernel(u_ref, m_ref, cin_ref, x_ref, y_ref):
    u = u_ref[0, 0]
    y = jnp.dot(u, m_ref[0], preferred_element_type=F32)
    for d in range(2):
        y = y + _dot_nt(x_ref[0, d].astype(BF16), cin_ref[d, 0])
    y_ref[0, 0] = y


def _ssm_call(qkv, prm):
    m_mat, bend, cin, apow = prm
    bsz, seq, _ = qkv.shape
    gg, tc, hh = SSM_GROUPS, SSM_CHUNK, SSM_GROUP
    nc = seq // tc
    u = qkv[:, :, COL_DU * 256:(COL_DU + 1) * 256]
    u = u.reshape(bsz, nc, tc, gg, hh).transpose(0, 3, 1, 2, 4).reshape(bsz, gg, nc, SSM_CW)
    width = gg * 128
    s = pl.pallas_call(
        _ssm_state_kernel,
        grid=(bsz,),
        in_specs=[
            pl.BlockSpec((1, gg, nc, SSM_CW), lambda b: (b, 0, 0, 0)),
            pl.BlockSpec((2, gg, SSM_CW, 128), lambda b: (0, 0, 0, 0)),
        ],
        out_specs=pl.BlockSpec((1, 2, nc, width), lambda b: (b, 0, 0, 0)),
        out_shape=jax.ShapeDtypeStruct((bsz, 2, nc, width), F32),
        compiler_params=_cparams(("parallel",)),
        name="ssm_state",
    )(u, bend)
    xprev = pl.pallas_call(
        functools.partial(_ssm_scan_kernel, nc=nc),
        grid=(bsz,),
        in_specs=[
            pl.BlockSpec((1, 2, nc, width), lambda b: (b, 0, 0, 0)),
            pl.BlockSpec((2, 1, width), lambda b: (0, 0, 0)),
        ],
        out_specs=pl.BlockSpec((1, 2, nc, width), lambda b: (b, 0, 0, 0)),
        out_shape=jax.ShapeDtypeStruct((bsz, 2, nc, width), F32),
        compiler_params=_cparams(("parallel",)),
        name="ssm_scan",
    )(s, apow.reshape(2, 1, width))
    y = pl.pallas_call(
        _ssm_out_kernel,
        grid=(gg, bsz),
        in_specs=[
            pl.BlockSpec((1, 1, nc, SSM_CW), lambda g, b: (b, g, 0, 0)),
            pl.BlockSpec((1, SSM_CW, SSM_CW), lambda g, b: (g, 0, 0)),
            pl.BlockSpec((2, 1, SSM_CW, 128), lambda g, b: (0, g, 0, 0)),
            pl.BlockSpec((1, 2, nc, 128), lambda g, b: (b, 0, 0, g)),
        ],
        out_specs=pl.BlockSpec((1, 1, nc, SSM_CW), lambda g, b: (b, g, 0, 0)),
        out_shape=jax.ShapeDtypeStruct((bsz, gg, nc, SSM_CW), F32),
        compiler_params=_cparams(("arbitrary", "arbitrary")),
        name="ssm_out",
    )(u, m_mat, cin, xprev)
    return y.reshape(bsz, gg, nc, tc, hh).transpose(0, 2, 3, 1, 4).reshape(bsz, seq, D_WIDTH)


def _merge_kernel(x_ref, h_ref, ada_ref,
                  oa0_ref, oa1_ref, oa2_ref, la0_ref, la1_ref, la2_ref, yb_ref, yc_ref, s5_ref,
                  wg_ref, bg_ref, wbr_ref, wout_ref, wglu_ref, bglu_ref, lng_ref, lnb_ref,
                  wr_ref, br_ref,
                  x1_ref, h2_ref, route_ref):
    ada = ada_ref[0]
    hb = h_ref[0]
    l0, l1, l2 = la0_ref[0], la1_ref[0], la2_ref[0]
    lm = jnp.maximum(jnp.maximum(l0, l1), l2)
    e0, e1, e2 = jnp.exp(l0 - lm), jnp.exp(l1 - lm), jnp.exp(l2 - lm)
    ya = (e0 * oa0_ref[0].astype(F32) + e1 * oa1_ref[0].astype(F32) + e2 * oa2_ref[0].astype(F32)) / (e0 + e1 + e2)
    s5 = s5_ref[0]
    gel = 0.5 * s5 * (1.0 + jnp.tanh(math.sqrt(2.0 / math.pi) * (s5 + 0.044715 * (s5 * s5 * s5))))
    z = jnp.dot(gel.astype(BF16), wglu_ref[...], preferred_element_type=F32) + bglu_ref[...]
    yd = z[:, :D_WIDTH] * _sigmoid(z[:, D_WIDTH:])
    branches = (ya.astype(BF16), yb_ref[0], yc_ref[0], yd.astype(BF16))
    offs = (0, A_WIDTH, A_WIDTH + B_WIDTH, A_WIDTH + B_WIDTH + C_WIDTH, A_WIDTH + B_WIDTH + C_WIDTH + D_WIDTH)
    merged = None
    for n, yb in enumerate(branches):
        cs = slice(n * D_MODEL, (n + 1) * D_MODEL)
        gate = _sigmoid(jnp.dot(hb, wg_ref[:, cs], preferred_element_type=F32) + bg_ref[:, cs])
        term = gate * jnp.dot(yb, wbr_ref[offs[n]:offs[n + 1], :], preferred_element_type=F32)
        merged = term if merged is None else merged + term
    mix = jnp.dot(merged.astype(BF16), wout_ref[...], preferred_element_type=F32)
    x1 = _ln(ALPHA * x_ref[0] + (1.0 + ada[2:3]) * mix) * lng_ref[...] + lnb_ref[...]
    x1_ref[0] = x1
    h2 = _ln(x1) * (1.0 + ada[4:5]) + ada[3:4]
    h2_ref[0] = h2
    logit = jnp.dot(h2, wr_ref[...], precision=HIGHEST, preferred_element_type=F32) + br_ref[...]
    lane = lax.broadcasted_iota(jnp.int32, logit.shape, 1)
    ninf = -jnp.inf
    gmask = lane < MOE_GROUPS
    gl = jnp.where(gmask, logit, ninf)
    gmax = jnp.max(gl, axis=-1, keepdims=True)
    gsel = jnp.min(jnp.where(gl == gmax, lane, 1 << 20), axis=-1, keepdims=True)
    gw = 1.0 / jnp.sum(jnp.exp(gl - gmax), axis=-1, keepdims=True)
    lo = MOE_GROUPS + gsel * MOE_EPG
    emask = (lane >= lo) & (lane < lo + MOE_EPG)
    el = jnp.where(emask, logit, ninf)
    v1 = jnp.max(el, axis=-1, keepdims=True)
    i1 = jnp.min(jnp.where(el == v1, lane, 1 << 20), axis=-1, keepdims=True)
    el2 = jnp.where(lane == i1, ninf, el)
    v2 = jnp.max(el2, axis=-1, keepdims=True)
    i2 = jnp.min(jnp.where(el2 == v2, lane, 1 << 20), axis=-1, keepdims=True)
    t = jnp.exp(v2 - v1)
    w1 = gw / (1.0 + t)
    w2 = gw * t / (1.0 + t)
    route = jnp.where(lane == 0, (i1 - MOE_GROUPS).astype(F32),
                      jnp.where(lane == 1, (i2 - MOE_GROUPS).astype(F32),
                                jnp.where(lane == 2, w1, jnp.where(lane == 3, w2, 0.0))))
    route_ref[0] = route


def _merge_call(x, h, ada8, oa, la, yb, yc, s5, lw, tm):
    bsz, seq, _ = x.shape

    def tok(width):
        return pl.BlockSpec((1, tm, width), lambda b, i: (b, i, 0))

    def full(arr):
        return pl.BlockSpec(arr.shape, lambda b, i: (0,) * arr.ndim)

    weights = (lw["w_gate"], lw["b_gate"], lw["w_branch"], lw["w_out"], lw["w_glu"], lw["b_glu"],
               lw["ln1_g"], lw["ln1_b"], lw["w_route"], lw["b_route"])
    return pl.pallas_call(
        _merge_kernel,
        grid=(bsz, seq // tm),
        in_specs=[tok(D_MODEL), tok(D_MODEL), pl.BlockSpec((1, 8, D_MODEL), lambda b, i: (b, 0, 0)),
                  tok(A_WIDTH), tok(A_WIDTH), tok(A_WIDTH), tok(A_WIDTH), tok(A_WIDTH), tok(A_WIDTH),
                  tok(B_WIDTH), tok(C_WIDTH), tok(D_WIDTH)] + [full(w) for w in weights],
        out_specs=[tok(D_MODEL), tok(D_MODEL), tok(128)],
        out_shape=[
            jax.ShapeDtypeStruct((bsz, seq, D_MODEL), F32),
            jax.ShapeDtypeStruct((bsz, seq, D_MODEL), F32),
            jax.ShapeDtypeStruct((bsz, seq, 128), F32),
        ],
        compiler_params=_cparams(("parallel", "parallel")),
        name="merge",
    )(x, h, ada8, oa[0], oa[1], oa[2], la[0], la[1], la[2], yb, yc, s5, *weights)


def _ffn_kernel(te_ref, nu_ref, src_ref, dst_ref, h_ref, wup_ref, wdn_ref, out_ref,
                src_s, dst_s, xbuf, ybuf, sem_i, sem_g, sem_s):
    del te_ref
    i = pl.program_id(0)
    tm = xbuf.shape[0]

    @pl.when(i < nu_ref[0])
    def _():
        c_src = pltpu.make_async_copy(src_ref.at[i], src_s, sem_i.at[0])
        c_dst = pltpu.make_async_copy(dst_ref.at[i], dst_s, sem_i.at[1])
        c_src.start()
        c_dst.start()
        c_src.wait()
        c_dst.wait()

        def gather(r, carry):
            pltpu.make_async_copy(h_ref.at[pl.ds(src_s[0, r], 1)], xbuf.at[pl.ds(r, 1)], sem_g).start()
            return carry

        lax.fori_loop(0, tm, gather, 0)
        pltpu.make_async_copy(h_ref.at[pl.ds(0, tm)], xbuf, sem_g).wait()
        a = jnp.dot(xbuf[...].astype(BF16), wup_ref[0], preferred_element_type=F32)
        g = a[:, :MOE_FF]
        hid = g * _sigmoid(g) * a[:, MOE_FF:]
        ybuf[...] = jnp.dot(hid.astype(BF16), wdn_ref[0], preferred_element_type=F32)

        def scatter(r, nreal):
            dst = dst_s[0, r]

            @pl.when(dst >= 0)
            def _():
                pltpu.make_async_copy(ybuf.at[pl.ds(r, 1)], out_ref.at[pl.ds(dst, 1)], sem_s).start()

            return nreal + (dst >= 0).astype(jnp.int32)

        nreal = lax.fori_loop(0, tm, scatter, 0)

        def drain(r, carry):
            pltpu.make_async_copy(ybuf.at[pl.ds(0, 1)], out_ref.at[pl.ds(0, 1)], sem_s).wait()
            return carry

        lax.fori_loop(0, nreal, drain, 0)


def _moe_plan(eid, tm):
    ntok = eid.shape[0]
    npair = 2 * ntok
    e_flat = jnp.concatenate([eid[:, 0], eid[:, 1]])
    order = jnp.argsort(e_flat, stable=True).astype(jnp.int32)
    counts = jnp.sum(jax.nn.one_hot(e_flat, N_EXPERTS, dtype=jnp.int32), axis=0)
    pcounts = (counts + tm - 1) // tm * tm
    pend = jnp.cumsum(pcounts)
    pstart = pend - pcounts
    cstart = jnp.cumsum(counts) - counts
    e_sorted = e_flat[order]
    pos = pstart[e_sorted] + jnp.arange(npair, dtype=jnp.int32) - cstart[e_sorted]
    nrow = npair + N_EXPERTS * tm
    ntile = nrow // tm
    src = jnp.zeros((nrow,), jnp.int32).at[pos].set(order % ntok)
    dst = jnp.full((nrow,), -1, jnp.int32).at[pos].set(order)
    tile_expert = jnp.searchsorted(pend, jnp.arange(ntile, dtype=jnp.int32) * tm, side="right")
    tile_expert = jnp.minimum(tile_expert, N_EXPERTS - 1).astype(jnp.int32)
    n_used = (pend[-1] // tm).astype(jnp.int32).reshape(1)
    return tile_expert, n_used, src.reshape(ntile, 1, tm), dst.reshape(ntile, 1, tm)


def _ffn_call(h2, route, w_up, w_down):
    ntok = h2.shape[0]
    tm = MOE_TM
    eid = route[:, :2].astype(jnp.int32)
    tile_expert, n_used, src, dst = _moe_plan(eid, tm)
    ntile = src.shape[0]
    grid_spec = pltpu.PrefetchScalarGridSpec(
        num_scalar_prefetch=2,
        grid=(ntile,),
        in_specs=[
            pl.BlockSpec(memory_space=pl.ANY),
            pl.BlockSpec(memory_space=pl.ANY),
            pl.BlockSpec(memory_space=pl.ANY),
            pl.BlockSpec((1, D_MODEL, 2 * MOE_FF), lambda i, te, nu: (te[i], 0, 0)),
            pl.BlockSpec((1, MOE_FF, D_MODEL), lambda i, te, nu: (te[i], 0, 0)),
        ],
        out_specs=pl.BlockSpec(memory_space=pl.ANY),
        scratch_shapes=[
            pltpu.SMEM((1, tm), jnp.int32),
            pltpu.SMEM((1, tm), jnp.int32),
            pltpu.VMEM((tm, D_MODEL), F32),
            pltpu.VMEM((tm, D_MODEL), F32),
            pltpu.SemaphoreType.DMA((2,)),
            pltpu.SemaphoreType.DMA(()),
            pltpu.SemaphoreType.DMA(()),
        ],
    )
    return pl.pallas_call(
        _ffn_kernel,
        grid_spec=grid_spec,
        out_shape=jax.ShapeDtypeStruct((2 * ntok, D_MODEL), F32),
        compiler_params=_cparams(("arbitrary",)),
        name="moe_ffn",
    )(tile_expert, n_used, src, dst, h2, w_up, w_down)


def _final_kernel(x1_ref, y0_ref, y1_ref, route_ref, ada_ref, lng_ref, lnb_ref, o_ref):
    ada = ada_ref[0]
    route = route_ref[0]
    ffn = route[:, 2:3] * y0_ref[...] + route[:, 3:4] * y1_ref[...]
    o_ref[0] = _ln(ALPHA * x1_ref[0] + (1.0 + ada[5:6]) * ffn) * lng_ref[...] + lnb_ref[...]


def _final_call(x1, y2, route, ada8, ln_g, ln_b, tm):
    bsz, seq, _ = x1.shape
    nt = seq // tm
    half = bsz * nt
    return pl.pallas_call(
        _final_kernel,
        grid=(bsz, nt),
        in_specs=[
            pl.BlockSpec((1, tm, D_MODEL), lambda b, i: (b, i, 0)),
            pl.BlockSpec((tm, D_MODEL), lambda b, i: (b * nt + i, 0)),
            pl.BlockSpec((tm, D_MODEL), lambda b, i: (half + b * nt + i, 0)),
            pl.BlockSpec((1, tm, 128), lambda b, i: (b, i, 0)),
            pl.BlockSpec((1, 8, D_MODEL), lambda b, i: (b, 0, 0)),
            pl.BlockSpec((1, D_MODEL), lambda b, i: (0, 0)),
            pl.BlockSpec((1, D_MODEL), lambda b, i: (0, 0)),
        ],
        out_specs=pl.BlockSpec((1, tm, D_MODEL), lambda b, i: (b, i, 0)),
        out_shape=jax.ShapeDtypeStruct((bsz, seq, D_MODEL), F32),
        compiler_params=_cparams(("parallel", "parallel")),
        name="final",
    )(x1, y2, y2, route, ada8, ln_g, ln_b)


def _split_w_in(w_in):
    sizes = (768, 768, 768, 512, 128, 128, 256, 256, 256, 256, N_BRANCH * D_MODEL)
    pts = [0]
    for s in sizes:
        pts.append(pts[-1] + s)
    return [w_in[:, pts[n]:pts[n + 1]] for n in range(len(sizes))]


def _layer_weights(p, l):
    a_q, a_k, a_v, b_q, b_k, b_v, c_q, c_k, c_v, d_u, w_gate = _split_w_in(p["w_in"][l])
    scale = HEAD_DIM ** -0.5
    cols = [b_q * scale]
    for g in range(A_GROUPS):
        gs = slice(g * A_WIDTH, (g + 1) * A_WIDTH)
        cols += [a_q[:, gs] * scale, a_k[:, gs], a_v[:, gs]]
    cols += [b_k, b_v, c_q * scale, c_k, c_v, d_u]
    w_route = jnp.concatenate([p["w_route_g"][l], p["w_route_e"][l]], axis=1)
    b_route = jnp.concatenate([p["b_route_g"][l], p["b_route_e"][l]])
    npad = 128 - w_route.shape[1]
    return {
        "w_qkv": jnp.concatenate(cols, axis=1).astype(BF16),
        "w_gate": w_gate.astype(BF16),
        "b_gate": p["b_gate"][l].reshape(1, -1),
        "w_branch": p["w_branch"][l].astype(BF16),
        "w_out": p["w_out"][l].astype(BF16),
        "w_glu": p["w_glu"][l].astype(BF16),
        "b_glu": p["b_glu"][l].reshape(1, -1),
        "ln1_g": p["ln1_g"][l].reshape(1, -1),
        "ln1_b": p["ln1_b"][l].reshape(1, -1),
        "ln2_g": p["ln2_g"][l].reshape(1, -1),
        "ln2_b": p["ln2_b"][l].reshape(1, -1),
        "w_route": jnp.pad(w_route, ((0, 0), (0, npad))),
        "b_route": jnp.pad(b_route, (0, npad)).reshape(1, -1),
        "w_up": p["w_up"][l].astype(BF16),
        "w_down": p["w_down"][l].astype(BF16),
        "b_sink": p["b_sink"][l],
        "c_bias": _c_bias_table(p["c_rpb"][l]),
    }


def _layer(x, ada8, lw, ssm_prm):
    bsz, seq, _ = x.shape
    tm = 256
    h, qkv = _proj_call(x, ada8, lw["w_qkv"], tm)
    oa, la = zip(*[_attn_a_call(qkv, g) for g in range(A_GROUPS)])
    yb = _attn_b_call(qkv, lw["b_sink"])
    yc = _attn_c_call(qkv, lw["c_bias"])
    s5 = _ssm_call(qkv, ssm_prm)
    x1, h2, route = _merge_call(x, h, ada8, oa, la, yb, yc, s5, lw, tm)
    ntok = bsz * seq
    y2 = _ffn_call(h2.reshape(ntok, D_MODEL), route.reshape(ntok, 128), lw["w_up"], lw["w_down"])
    return _final_call(x1, y2, route, ada8, lw["ln2_g"], lw["ln2_b"], tm)


def kernel(x_prompt, x_sample, c_prompt, c_sample, w_ada, b_ada, w_in, b_gate, b_sink, c_rpb, lam_re, lam_im,
           log_dt, ssm_b_re, ssm_b_im, ssm_c_re, ssm_c_im, ssm_d, w_glu, b_glu, w_branch, w_out, ln1_g, ln1_b,
           ln2_g, ln2_b, w_route_g, b_route_g, w_route_e, b_route_e, w_up, w_down):
    p = dict(w_in=w_in, b_gate=b_gate, b_sink=b_sink, c_rpb=c_rpb, w_glu=w_glu, b_glu=b_glu, w_branch=w_branch,
             w_out=w_out, ln1_g=ln1_g, ln1_b=ln1_b, ln2_g=ln2_g, ln2_b=ln2_b, w_route_g=w_route_g,
             b_route_g=b_route_g, w_route_e=w_route_e, b_route_e=b_route_e, w_up=w_up, w_down=w_down)
    nbp, nbs = c_prompt.shape[0], c_sample.shape[0]
    c_all = jnp.concatenate([c_prompt, c_sample], axis=0)
    c_all = jnp.pad(c_all, ((0, -c_all.shape[0] % 8), (0, 0)))
    ada = _ada_call(c_all, w_ada, b_ada).reshape(DEPTH, c_all.shape[0], 6, D_MODEL)
    ada = jnp.pad(ada, ((0, 0), (0, 0), (0, 2), (0, 0)))
    m_mat, bend, cin, apow = _ssm_param_call(lam_re, lam_im, log_dt, ssm_b_re, ssm_b_im, ssm_c_re, ssm_c_im, ssm_d)
    xs = [x_prompt, x_sample]
    for l in range(DEPTH):
        lw = _layer_weights(p, l)
        prm = (m_mat[l * SSM_GROUPS:(l + 1) * SSM_GROUPS], bend[2 * l:2 * l + 2], cin[2 * l:2 * l + 2],
               apow[2 * l:2 * l + 2])
        xs = [_layer(xs[0], ada[l, :nbp], lw, prm), _layer(xs[1], ada[l, nbp:nbp + nbs], lw, prm)]
    return (xs[0], xs[1])
```

```python
import functools
import math

import jax
import jax.numpy as jnp
from jax import lax
from jax.experimental import pallas as pl
from jax.experimental.pallas import tpu as pltpu

F32 = jnp.float32
BF16 = jnp.bfloat16
HIGHEST = lax.Precision.HIGHEST

D_MODEL = 1024
DEPTH = 2
HEAD_DIM = 64
A_PAIRS = ((128, 1), (512, 4), (2048, 16))
A_GROUPS = 3
A_HEADS = 4
A_WIDTH = A_HEADS * HEAD_DIM
B_Q_HEADS = 8
B_KV_HEADS = 2
B_WINDOW = 128
B_WIDTH = B_Q_HEADS * HEAD_DIM
GRID_W = 64
C_HEADS = 4
C_WIN_R = 8
C_WIN_C = 16
C_WIDTH = C_HEADS * HEAD_DIM
D_WIDTH = 256
SSM_GROUP = 16
SSM_GROUPS = D_WIDTH // SSM_GROUP
SSM_STATE = 64
N_BRANCH = 4
MOE_GROUPS = 4
MOE_EPG = 8
N_EXPERTS = MOE_GROUPS * MOE_EPG
MOE_FF = 512
ALPHA = (2 * DEPTH) ** 0.25
LN_EPS = 1e-5
NEG = -1e30

QKV_COLS = 4096
MAIN_COLS = 2560
COL_BQ = 0
COL_A0 = 2
COL_BKV = 5
COL_CQ = 6
COL_DU = 9
A_QKV = 3 * A_WIDTH
ROW_TILE = 8
A_HALF = 64
SSM_CHUNK = 64
SSM_CW = SSM_CHUNK * SSM_GROUP
MOE_TM = 256
VMEM_LIMIT = 56 * 1024 * 1024


def _cparams(sem):
    return pltpu.CompilerParams(dimension_semantics=sem, vmem_limit_bytes=VMEM_LIMIT)


def _sigmoid(x):
    return 1.0 / (1.0 + jnp.exp(-x))


def _ln(x):
    mu = jnp.mean(x, axis=-1, keepdims=True)
    xc = x - mu
    var = jnp.mean(xc * xc, axis=-1, keepdims=True)
    return xc * lax.rsqrt(var + LN_EPS)


def _dot_nt(a, b, **kw):
    return lax.dot_general(a, b, (((1,), (1,)), ((), ())), preferred_element_type=F32, **kw)


def _alibi(n):
    return [2.0 ** (-8.0 * (i + 1) / n) for i in range(n)]


def _ada_kernel(c_ref, w_ref, b_ref, o_ref):
    c = c_ref[...]
    sc = c * _sigmoid(c)
    o_ref[0] = jnp.dot(sc, w_ref[0], precision=HIGHEST, preferred_element_type=F32) + b_ref[0]


def _ada_call(c_all, w_ada, b_ada):
    nb = c_all.shape[0]
    return pl.pallas_call(
        _ada_kernel,
        grid=(DEPTH, 6),
        in_specs=[
            pl.BlockSpec((nb, D_MODEL), lambda l, j: (0, 0)),
            pl.BlockSpec((1, D_MODEL, D_MODEL), lambda l, j: (l, 0, j)),
            pl.BlockSpec((1, 1, D_MODEL), lambda l, j: (l, 0, j)),
        ],
        out_specs=pl.BlockSpec((1, nb, D_MODEL), lambda l, j: (l, 0, j)),
        out_shape=jax.ShapeDtypeStruct((DEPTH, nb, 6 * D_MODEL), F32),
        compiler_params=_cparams(("arbitrary", "arbitrary")),
        name="ada",
    )(c_all, w_ada, b_ada.reshape(DEPTH, 1, 6 * D_MODEL))


def _proj_kernel(x_ref, ada_ref, w_ref, h_ref, main_ref, f1_ref, f2_ref, scr_ref, *, chunk):
    x = x_ref[0]
    ada = ada_ref[0]
    tm = x.shape[0]
    h = _ln(x) * (1.0 + ada[1:2]) + ada[0:1]
    hb = h.astype(BF16)
    h_ref[0] = hb
    for j in range(MAIN_COLS // chunk):
        sl = slice(j * chunk, (j + 1) * chunk)
        main_ref[0, :, sl] = jnp.dot(hb, w_ref[:, sl], preferred_element_type=F32).astype(BF16)
    for g, f_ref in ((1, f1_ref), (2, f2_ref)):
        dil = A_PAIRS[g][1]
        c0 = MAIN_COLS + (g - 1) * A_QKV
        res = jnp.dot(hb, w_ref[:, c0:c0 + A_QKV], preferred_element_type=F32)
        for c in range(A_QKV // 128):
            scr_ref[c] = res[:, c * 128:(c + 1) * 128]
        for r in range(dil):
            for c in range(A_QKV // 128):
                lo = r * A_QKV + c * 128
                f_ref[0, :, lo:lo + 128] = scr_ref[c, pl.ds(r, tm // dil, stride=dil), :].astype(BF16)


def _proj_call(x, ada8, w_qkv, tm):
    bsz, seq, _ = x.shape
    d1, d2 = A_PAIRS[1][1], A_PAIRS[2][1]
    return pl.pallas_call(
        functools.partial(_proj_kernel, chunk=512),
        grid=(bsz, seq // tm),
        in_specs=[
            pl.BlockSpec((1, tm, D_MODEL), lambda b, i: (b, i, 0)),
            pl.BlockSpec((1, 8, D_MODEL), lambda b, i: (b, 0, 0)),
            pl.BlockSpec((D_MODEL, QKV_COLS), lambda b, i: (0, 0), pipeline_mode=pl.Buffered(1)),
        ],
        out_specs=[
            pl.BlockSpec((1, tm, D_MODEL), lambda b, i: (b, i, 0)),
            pl.BlockSpec((1, tm, MAIN_COLS), lambda b, i: (b, i, 0)),
            pl.BlockSpec((1, tm // d1, d1 * A_QKV), lambda b, i: (b, i, 0)),
            pl.BlockSpec((1, tm // d2, d2 * A_QKV), lambda b, i: (b, i, 0)),
        ],
        out_shape=[
            jax.ShapeDtypeStruct((bsz, seq, D_MODEL), BF16),
            jax.ShapeDtypeStruct((bsz, seq, MAIN_COLS), BF16),
            jax.ShapeDtypeStruct((bsz, seq // d1, d1 * A_QKV), BF16),
            jax.ShapeDtypeStruct((bsz, seq // d2, d2 * A_QKV), BF16),
        ],
        scratch_shapes=[pltpu.VMEM((A_QKV // 128, tm, 128), F32)],
        compiler_params=_cparams(("parallel", "parallel")),
        name="proj",
    )(x, ada8, w_qkv)


def _attn_a_kernel(q_ref, kc_ref, kp_ref, kn_ref, vc_ref, vp_ref, vn_ref, o_ref, lse_ref,
                   *, tq, sub, slopes, dil):
    i = pl.program_id(2)
    q = q_ref[0]
    k = jnp.concatenate([kp_ref[0], kc_ref[0], kn_ref[0]], axis=0)
    v = jnp.concatenate([vp_ref[0], vc_ref[0], vn_ref[0]], axis=0)
    sq = 128
    kw = sq + 2 * A_HALF
    row = lax.broadcasted_iota(jnp.int32, (sq, kw), 0)
    col = lax.broadcasted_iota(jnp.int32, (sq, kw), 1)
    rel = col - A_HALF - row
    dist = jnp.abs(rel).astype(F32) * float(dil)
    for j in range(tq // sq):
        kpos = i * tq + j * sq - A_HALF + col
        valid = (jnp.abs(rel) <= A_HALF) & (kpos >= 0) & (kpos < sub)
        qj = q[j * sq:(j + 1) * sq]
        kj = k[j * sq:j * sq + kw]
        vj = v[j * sq:j * sq + kw]
        for h in range(A_HEADS):
            hs = slice(h * HEAD_DIM, (h + 1) * HEAD_DIM)
            s = _dot_nt(qj[:, hs], kj[:, hs])
            s = jnp.where(valid, s - slopes[h] * dist, NEG)
            m = jnp.max(s, axis=-1, keepdims=True)
            p = jnp.exp(s - m)
            den = jnp.sum(p, axis=-1, keepdims=True)
            o = jnp.dot(p.astype(BF16), vj[:, hs], preferred_element_type=F32) / den
            o_ref[0, j * sq:(j + 1) * sq, hs] = o.astype(BF16)
            lse_ref[0, j * sq:(j + 1) * sq, hs] = jnp.broadcast_to(m + jnp.log(den), (sq, HEAD_DIM))


def _attn_a_call(folded, g, seq):
    bsz = folded.shape[0]
    _, dil = A_PAIRS[g]
    sub = seq // dil
    tq = min(512, sub)
    nq = sub // tq
    hb = tq // A_HALF
    nhb = sub // A_HALF
    nblk = folded.shape[2] // dil // 256
    cq = COL_A0 if g == 0 else 0
    slopes = _alibi(A_GROUPS * A_HEADS)[g * A_HEADS:(g + 1) * A_HEADS]

    def cur(c):
        return pl.BlockSpec((1, tq, 256), lambda b, r, i: (b, i, r * nblk + c))

    def prev(c):
        return pl.BlockSpec((1, A_HALF, 256), lambda b, r, i: (b, jnp.maximum(i * hb - 1, 0), r * nblk + c))

    def nxt(c):
        return pl.BlockSpec((1, A_HALF, 256), lambda b, r, i: (b, jnp.minimum((i + 1) * hb, nhb - 1), r * nblk + c))

    return pl.pallas_call(
        functools.partial(_attn_a_kernel, tq=tq, sub=sub, slopes=slopes, dil=dil),
        grid=(bsz, dil, nq),
        in_specs=[cur(cq), cur(cq + 1), prev(cq + 1), nxt(cq + 1), cur(cq + 2), prev(cq + 2), nxt(cq + 2)],
        out_specs=[
            pl.BlockSpec((1, tq, A_WIDTH), lambda b, r, i: (b, i, r)),
            pl.BlockSpec((1, tq, A_WIDTH), lambda b, r, i: (b, i, r)),
        ],
        out_shape=[
            jax.ShapeDtypeStruct((bsz, sub, dil * A_WIDTH), BF16),
            jax.ShapeDtypeStruct((bsz, sub, dil * A_WIDTH), F32),
        ],
        compiler_params=_cparams(("parallel", "parallel", "parallel")),
        name=f"attn_a{g}",
    )(folded, folded, folded, folded, folded, folded, folded)


def _attn_b_kernel(sink_ref, q_ref, kvc_ref, kvp_ref, kvn_ref, o_ref, *, tq, seq, slopes):
    i = pl.program_id(1)
    q = q_ref[0]
    kv = jnp.concatenate([kvp_ref[0], kvc_ref[0], kvn_ref[0]], axis=0)
    sq = 128
    kw = sq + 2 * B_WINDOW
    row = lax.broadcasted_iota(jnp.int32, (sq, kw), 0)
    col = lax.broadcasted_iota(jnp.int32, (sq, kw), 1)
    rel = col - B_WINDOW - row
    dist = jnp.abs(rel).astype(F32)
    rep = B_Q_HEADS // B_KV_HEADS
    for j in range(tq // sq):
        kpos = i * tq + j * sq - B_WINDOW + col
        valid = (jnp.abs(rel) <= B_WINDOW) & (kpos >= 0) & (kpos < seq)
        qj = q[j * sq:(j + 1) * sq]
        kvj = kv[j * sq:j * sq + kw]
        for h in range(B_Q_HEADS):
            g = h // rep
            kh = kvj[:, g * HEAD_DIM:(g + 1) * HEAD_DIM]
            vh = kvj[:, (B_KV_HEADS + g) * HEAD_DIM:(B_KV_HEADS + g + 1) * HEAD_DIM]
            s = _dot_nt(qj[:, h * HEAD_DIM:(h + 1) * HEAD_DIM], kh)
            s = jnp.where(valid, s - slopes[h] * dist, NEG)
            sk = sink_ref[h]
            m = jnp.maximum(jnp.max(s, axis=-1, keepdims=True), sk)
            p = jnp.exp(s - m)
            den = jnp.sum(p, axis=-1, keepdims=True) + jnp.exp(sk - m)
            o = jnp.dot(p.astype(BF16), vh, preferred_element_type=F32) / den
            o_ref[0, j * sq:(j + 1) * sq, h * HEAD_DIM:(h + 1) * HEAD_DIM] = o.astype(BF16)


def _attn_b_call(qkv, sink):
    bsz, seq, _ = qkv.shape
    tq = 512
    hb = tq // B_WINDOW
    nhb = seq // B_WINDOW
    return pl.pallas_call(
        functools.partial(_attn_b_kernel, tq=tq, seq=seq, slopes=_alibi(B_Q_HEADS)),
        grid=(bsz, seq // tq),
        in_specs=[
            pl.BlockSpec(memory_space=pltpu.SMEM),
            pl.BlockSpec((1, tq, B_WIDTH), lambda b, i: (b, i, COL_BQ)),
            pl.BlockSpec((1, tq, 256), lambda b, i: (b, i, COL_BKV)),
            pl.BlockSpec((1, B_WINDOW, 256), lambda b, i: (b, jnp.maximum(i * hb - 1, 0), COL_BKV)),
            pl.BlockSpec((1, B_WINDOW, 256), lambda b, i: (b, jnp.minimum((i + 1) * hb, nhb - 1), COL_BKV)),
        ],
        out_specs=pl.BlockSpec((1, tq, B_WIDTH), lambda b, i: (b, i, 0)),
        out_shape=jax.ShapeDtypeStruct((bsz, seq, B_WIDTH), BF16),
        compiler_params=_cparams(("parallel", "parallel")),
        name="attn_b",
    )(sink, qkv, qkv, qkv, qkv)


C_ROWS_PER_STEP = 8


def _attn_c_kernel(q_ref, k_ref, v_ref, bias_ref, o_ref, *, rows):
    i = pl.program_id(1)
    kwin = C_WIN_R * GRID_W
    for jr in range(C_ROWS_PER_STEP):
        r = i * C_ROWS_PER_STEP + jr
        rs = jnp.clip(r - C_WIN_R // 2, 0, rows - C_WIN_R)
        delta = r - rs
        start = pl.multiple_of(rs * GRID_W, GRID_W)
        kw = k_ref[0, pl.ds(start, kwin), :]
        vw = v_ref[0, pl.ds(start, kwin), :]
        qr = q_ref[0, jr * GRID_W:(jr + 1) * GRID_W, :]
        for h in range(C_HEADS):
            hs = slice(h * HEAD_DIM, (h + 1) * HEAD_DIM)
            s = _dot_nt(qr[:, hs], kw[:, hs]) + bias_ref[h, delta]
            m = jnp.max(s, axis=-1, keepdims=True)
            p = jnp.exp(s - m)
            den = jnp.sum(p, axis=-1, keepdims=True)
            o = jnp.dot(p.astype(BF16), vw[:, hs], preferred_element_type=F32) / den
            o_ref[0, jr * GRID_W:(jr + 1) * GRID_W, hs] = o.astype(BF16)


def _c_bias_table(rpb):
    qc = jnp.arange(GRID_W)[:, None]
    kc = jnp.arange(GRID_W)[None, :]
    col_off = jnp.clip(kc - qc + (C_WIN_C - 1), 0, 2 * C_WIN_C - 2)
    sel = (col_off[None] == jnp.arange(2 * C_WIN_C - 1)[:, None, None]).astype(F32)
    band = jnp.einsum("hrc,cqk->hrqk", rpb.astype(F32), sel, precision=HIGHEST)
    cs = jnp.clip(qc - C_WIN_C // 2, 0, GRID_W - C_WIN_C)
    valid = (kc >= cs) & (kc < cs + C_WIN_C)
    band = jnp.where(valid[None, None], band, NEG)
    tabs = [band[:, C_WIN_R - 1 - d:2 * C_WIN_R - 1 - d].transpose(0, 2, 1, 3) for d in range(C_WIN_R)]
    return jnp.stack(tabs, axis=1).reshape(C_HEADS, C_WIN_R, GRID_W, C_WIN_R * GRID_W)


def _attn_c_call(qkv, bias_tab):
    bsz, seq, _ = qkv.shape
    rows = seq // GRID_W
    tq = C_ROWS_PER_STEP * GRID_W
    return pl.pallas_call(
        functools.partial(_attn_c_kernel, rows=rows),
        grid=(bsz, seq // tq),
        in_specs=[
            pl.BlockSpec((1, tq, C_WIDTH), lambda b, i: (b, i, COL_CQ)),
            pl.BlockSpec((1, seq, C_WIDTH), lambda b, i: (b, 0, COL_CQ + 1)),
            pl.BlockSpec((1, seq, C_WIDTH), lambda b, i: (b, 0, COL_CQ + 2)),
            pl.BlockSpec(bias_tab.shape, lambda b, i: (0, 0, 0, 0)),
        ],
        out_specs=pl.BlockSpec((1, tq, C_WIDTH), lambda b, i: (b, i, 0)),
        out_shape=jax.ShapeDtypeStruct((bsz, seq, C_WIDTH), BF16),
        compiler_params=_cparams(("parallel", "arbitrary")),
        name="attn_c",
    )(qkv, qkv, qkv, bias_tab)


def _cmul(ar, ai, br, bi):
    return ar * br - ai * bi, ar * bi + ai * br


def _ssm_param_kernel(lre_ref, lim_ref, ldt_ref, cre_ref, cim_ref, bre_ref, bim_ref, d_ref,
                      m_ref, bend_ref, cin_ref, apow_ref):
    tc, hh, pp = SSM_CHUNK, SSM_GROUP, SSM_STATE
    it = lax.broadcasted_iota(jnp.int32, (tc, pp), 0).astype(F32)
    lane = lax.broadcasted_iota(jnp.int32, (hh, 128), 1)
    subl = lax.broadcasted_iota(jnp.int32, (hh, 128), 0)
    tabs = []
    for d in range(2):
        lr, li = lre_ref[d, 0, 0], lim_ref[d, 0, 0]
        dt = jnp.exp(ldt_ref[d, 0, 0])

        def powers(kv):
            mag = jnp.exp(kv * (lr * dt))
            ang = kv * (li * dt)
            return mag * jnp.cos(ang), mag * jnp.sin(ang)

        one = jnp.ones((1, pp), F32)
        lbr, lbi = powers(one)
        den = lr * lr + li * li
        nr, ni = lbr - 1.0, lbi
        fr = (nr * lr + ni * li) / den
        fi = (ni * lr - nr * li) / den
        apr, api = powers(one * float(tc))
        apow_ref[d, 0] = jnp.concatenate([apr, api], axis=1)

        cr, ci = cre_ref[d, 0], cim_ref[d, 0]
        br, bi = bre_ref[d, 0], bim_ref[d, 0]

        wr, wi = powers(it + 1.0 if d == 0 else float(tc) - it)
        cwr, cwi = _cmul(cr[None], ci[None], wr[:, None, :], wi[:, None, :])
        cin = jnp.concatenate([cwr, -cwi], axis=2).reshape(tc * hh, 2 * pp)
        cin_ref[d, 0] = cin.astype(BF16)

        wr, wi = powers(float(tc - 1) - it if d == 0 else it)
        wfr, wfi = _cmul(wr, wi, fr, fi)
        er, ei = _cmul(br[None], bi[None], wfr[:, None, :], wfi[:, None, :])
        bend_ref[d, 0] = jnp.concatenate([er, ei], axis=2).reshape(tc * hh, 2 * pp).astype(BF16)

        wr, wi = powers(it if d == 0 else float(tc - 1) - it)
        wfr, wfi = _cmul(wr, wi, fr, fi)
        gr, gi = _cmul(cr[None], ci[None], wfr[:, None, :], wfi[:, None, :])
        gr = gr.reshape(tc * hh, pp)
        gi = gi.reshape(tc * hh, pp)
        tabs.append(_dot_nt(br, gr, precision=HIGHEST) - _dot_nt(bi, gi, precision=HIGHEST))

    ktf, ktb = tabs
    last = pltpu.roll(ktb[:, SSM_CW - 128:], hh, axis=1)
    diag = jnp.where(lane == subl, d_ref[0], 0.0)
    first = ktf[:, :128] + jnp.where(lane < hh, last + diag, 0.0)
    ktf = jnp.concatenate([first, ktf[:, 128:]], axis=1)
    lane_w = lax.broadcasted_iota(jnp.int32, (hh, SSM_CW), 1)
    ktb = jnp.where(lane_w < SSM_CW - hh, ktb, 0.0)
    zeros = jnp.zeros((hh, SSM_CW), F32)
    pf = jnp.concatenate([ktf, zeros], axis=1)
    pb = jnp.concatenate([ktb, zeros], axis=1)
    for j in range(tc):
        blk = pf if j == 0 else pltpu.roll(pf, j * hh, axis=1)
        sh = (tc - 1 - j) * hh
        blb = pb if sh == 0 else pltpu.roll(pb, 2 * SSM_CW - sh, axis=1)
        m_ref[0, j * hh:(j + 1) * hh, :] = (blk[:, :SSM_CW] + blb[:, :SSM_CW]).astype(BF16)


def _ssm_param_call(lam_re, lam_im, log_dt, b_re, b_im, c_re, c_im, d_skip):
    gg, pp, hh = SSM_GROUPS, SSM_STATE, SSM_GROUP
    nl = lam_re.shape[0]
    lre = lam_re.reshape(nl * 2, gg, 1, 1, pp)
    lim = lam_im.reshape(nl * 2, gg, 1, 1, pp)
    ldt = jnp.broadcast_to(log_dt.reshape(nl * 2, gg, 1, 1, 1), (nl * 2, gg, 1, 1, pp))
    cre = c_re.reshape(nl * 2, gg, hh, pp)
    cim = c_im.reshape(nl * 2, gg, hh, pp)
    bre = jnp.swapaxes(b_re, -1, -2).reshape(nl * 2, gg, hh, pp)
    bim = jnp.swapaxes(b_im, -1, -2).reshape(nl * 2, gg, hh, pp)
    dpad = jnp.pad(d_skip.reshape(nl * gg, 1, hh), ((0, 0), (0, 0), (0, 128 - hh)))

    def vec(l, g):
        return (l, g, 0, 0, 0)

    def mat(l, g):
        return (l, g, 0, 0)

    return pl.pallas_call(
        _ssm_param_kernel,
        grid=(nl, gg),
        in_specs=[
            pl.BlockSpec((2, 1, 1, 1, pp), vec), pl.BlockSpec((2, 1, 1, 1, pp), vec),
            pl.BlockSpec((2, 1, 1, 1, pp), vec),
            pl.BlockSpec((2, 1, hh, pp), mat), pl.BlockSpec((2, 1, hh, pp), mat),
            pl.BlockSpec((2, 1, hh, pp), mat), pl.BlockSpec((2, 1, hh, pp), mat),
            pl.BlockSpec((1, 1, 128), lambda l, g: (l * gg + g, 0, 0)),
        ],
        out_specs=[
            pl.BlockSpec((1, SSM_CW, SSM_CW), lambda l, g: (l * gg + g, 0, 0)),
            pl.BlockSpec((2, 1, SSM_CW, 2 * pp), mat),
            pl.BlockSpec((2, 1, SSM_CW, 2 * pp), mat),
            pl.BlockSpec((2, 1, 1, 2 * pp), mat),
        ],
        out_shape=[
            jax.ShapeDtypeStruct((nl * gg, SSM_CW, SSM_CW), BF16),
            jax.ShapeDtypeStruct((nl * 2, gg, SSM_CW, 2 * pp), BF16),
            jax.ShapeDtypeStruct((nl * 2, gg, SSM_CW, 2 * pp), BF16),
            jax.ShapeDtypeStruct((nl * 2, gg, 1, 2 * pp), F32),
        ],
        compiler_params=_cparams(("arbitrary", "arbitrary")),
        name="ssm_param",
    )(lre, lim, ldt, cre, cim, bre, bim, dpad)


def _ssm_state_kernel(u_ref, bend_ref, s_ref):
    for g in range(SSM_GROUPS):
        u = u_ref[0, g]
        for d in range(2):
            s_ref[0, d, :, g * 128:(g + 1) * 128] = jnp.dot(u, bend_ref[d, g], preferred_element_type=F32)


def _ssm_scan_kernel(s_ref, apow_ref, x_ref, *, nc):
    width = SSM_GROUPS * 128
    lane = lax.broadcasted_iota(jnp.int32, (1, width), 1)
    first_half = (lane % 128) < SSM_STATE
    for d in range(2):
        a = apow_ref[d]
        a_re = jnp.where(first_half, a, pltpu.roll(a, SSM_STATE, axis=1))
        a_im = jnp.where(first_half, -pltpu.roll(a, width - SSM_STATE, axis=1), a)

        def body(c, x):
            cc = c if d == 0 else nc - 1 - c
            x_ref[0, d, pl.ds(cc, 1), :] = x
            x_sw_lo = pltpu.roll(x, SSM_STATE, axis=1)
            x_sw_hi = pltpu.roll(x, width - SSM_STATE, axis=1)
            x_sw = jnp.where(first_half, x_sw_hi, x_sw_lo)
            return a_re * x + a_im * x_sw + s_ref[0, d, pl.ds(cc, 1), :]

        lax.fori_loop(0, nc, body, jnp.zeros((1, width), F32))


def _ssm_out_kernel(u_ref, m_ref, cin_ref, x_ref, y_ref):
    u = u_ref[0, 0]
    y = jnp.dot(u, m_ref[0], preferred_element_type=F32)
    for d in range(2):
        y = y + _dot_nt(x_ref[0, d].astype(BF16), cin_ref[d, 0])
    y_ref[0, 0] = y


def _ssm_call(qkv, prm):
    m_mat, bend, cin, apow = prm
    bsz, seq, _ = qkv.shape
    gg, tc, hh = SSM_GROUPS, SSM_CHUNK, SSM_GROUP
    nc = seq // tc
    u = qkv[:, :, COL_DU * 256:(COL_DU + 1) * 256]
    u = u.reshape(bsz, nc, tc, gg, hh).transpose(0, 3, 1, 2, 4).reshape(bsz, gg, nc, SSM_CW)
    width = gg * 128
    s = pl.pallas_call(
        _ssm_state_kernel,
        grid=(bsz,),
        in_specs=[
            pl.BlockSpec((1, gg, nc, SSM_CW), lambda b: (b, 0, 0, 0)),
            pl.BlockSpec((2, gg, SSM_CW, 128), lambda b: (0, 0, 0, 0)),
        ],
        out_specs=pl.BlockSpec((1, 2, nc, width), lambda b: (b, 0, 0, 0)),
        out_shape=jax.ShapeDtypeStruct((bsz, 2, nc, width), F32),
        compiler_params=_cparams(("parallel",)),
        name="ssm_state",
    )(u, bend)
    xprev = pl.pallas_call(
        functools.partial(_ssm_scan_kernel, nc=nc),
        grid=(bsz,),
        in_specs=[
            pl.BlockSpec((1, 2, nc, width), lambda b: (b, 0, 0, 0)),
            pl.BlockSpec((2, 1, width), lambda b: (0, 0, 0)),
        ],
        out_specs=pl.BlockSpec((1, 2, nc, width), lambda b: (b, 0, 0, 0)),
        out_shape=jax.ShapeDtypeStruct((bsz, 2, nc, width), F32),
        compiler_params=_cparams(("parallel",)),
        name="ssm_scan",
    )(s, apow.reshape(2, 1, width))
    y = pl.pallas_call(
        _ssm_out_kernel,
        grid=(gg, bsz),
        in_specs=[
            pl.BlockSpec((1, 1, nc, SSM_CW), lambda g, b: (b, g, 0, 0)),
            pl.BlockSpec((1, SSM_CW, SSM_CW), lambda g, b: (g, 0, 0)),
            pl.BlockSpec((2, 1, SSM_CW, 128), lambda g, b: (0, g, 0, 0)),
            pl.BlockSpec((1, 2, nc, 128), lambda g, b: (b, 0, 0, g)),
        ],
        out_specs=pl.BlockSpec((1, 1, nc, SSM_CW), lambda g, b: (b, g, 0, 0)),
        out_shape=jax.ShapeDtypeStruct((bsz, gg, nc, SSM_CW), F32),
        compiler_params=_cparams(("arbitrary", "arbitrary")),
        name="ssm_out",
    )(u, m_mat, cin, xprev)
    return y.reshape(bsz, gg, nc, tc, hh).transpose(0, 2, 3, 1, 4).reshape(bsz, seq, D_WIDTH)


def _merge_kernel(x_ref, h_ref, ada_ref,
                  oa0_ref, oa1_ref, oa2_ref, la0_ref, la1_ref, la2_ref, yb_ref, yc_ref, s5_ref,
                  wg_ref, bg_ref, wbr_ref, wout_ref, wglu_ref, bglu_ref, lng_ref, lnb_ref,
                  wr_ref, br_ref,
                  x1_ref, h2_ref, route_ref, o1_scr, l1_scr, o2_scr, l2_scr):
    ada = ada_ref[0]
    hb = h_ref[0]
    tm = hb.shape[0]
    for g, (o_ref, l_ref, o_scr, l_scr) in ((1, (oa1_ref, la1_ref, o1_scr, l1_scr)),
                                            (2, (oa2_ref, la2_ref, o2_scr, l2_scr))):
        dil = A_PAIRS[g][1]
        for r in range(dil):
            for c in range(A_WIDTH // 128):
                cs = slice(r * A_WIDTH + c * 128, r * A_WIDTH + (c + 1) * 128)
                o_scr[c, pl.ds(r, tm // dil, stride=dil), :] = o_ref[0, :, cs].astype(F32)
                l_scr[c, pl.ds(r, tm // dil, stride=dil), :] = l_ref[0, :, cs]

    def unfolded(scr):
        return jnp.concatenate([scr[c] for c in range(A_WIDTH // 128)], axis=1)

    l0, l1, l2 = la0_ref[0], unfolded(l1_scr), unfolded(l2_scr)
    lm = jnp.maximum(jnp.maximum(l0, l1), l2)
    e0, e1, e2 = jnp.exp(l0 - lm), jnp.exp(l1 - lm), jnp.exp(l2 - lm)
    ya = (e0 * oa0_ref[0].astype(F32) + e1 * unfolded(o1_scr) + e2 * unfolded(o2_scr)) / (e0 + e1 + e2)
    s5 = s5_ref[0]
    gel = 0.5 * s5 * (1.0 + jnp.tanh(math.sqrt(2.0 / math.pi) * (s5 + 0.044715 * (s5 * s5 * s5))))
    z = jnp.dot(gel.astype(BF16), wglu_ref[...], preferred_element_type=F32) + bglu_ref[...]
    yd = z[:, :D_WIDTH] * _sigmoid(z[:, D_WIDTH:])
    branches = (ya.astype(BF16), yb_ref[0], yc_ref[0], yd.astype(BF16))
    offs = (0, A_WIDTH, A_WIDTH + B_WIDTH, A_WIDTH + B_WIDTH + C_WIDTH, A_WIDTH + B_WIDTH + C_WIDTH + D_WIDTH)
    merged = None
    for n, yb in enumerate(branches):
        cs = slice(n * D_MODEL, (n + 1) * D_MODEL)
        gate = _sigmoid(jnp.dot(hb, wg_ref[:, cs], preferred_element_type=F32) + bg_ref[:, cs])
        term = gate * jnp.dot(yb, wbr_ref[offs[n]:offs[n + 1], :], preferred_element_type=F32)
        merged = term if merged is None else merged + term
    mix = jnp.dot(merged.astype(BF16), wout_ref[...], preferred_element_type=F32)
    x1 = _ln(ALPHA * x_ref[0] + (1.0 + ada[2:3]) * mix) * lng_ref[...] + lnb_ref[...]
    x1_ref[0] = x1
    h2 = _ln(x1) * (1.0 + ada[4:5]) + ada[3:4]
    for s in range(ROW_TILE):
        h2_ref[pl.ds(s, tm, stride=ROW_TILE), :] = h2[:, s * 128:(s + 1) * 128]
    logit = jnp.dot(h2, wr_ref[...], precision=HIGHEST, preferred_element_type=F32) + br_ref[...]
    lane = lax.broadcasted_iota(jnp.int32, logit.shape, 1)
    ninf = -jnp.inf
    gmask = lane < MOE_GROUPS
    gl = jnp.where(gmask, logit, ninf)
    gmax = jnp.max(gl, axis=-1, keepdims=True)
    gsel = jnp.min(jnp.where(gl == gmax, lane, 1 << 20), axis=-1, keepdims=True)
    gw = 1.0 / jnp.sum(jnp.exp(gl - gmax), axis=-1, keepdims=True)
    lo = MOE_GROUPS + gsel * MOE_EPG
    emask = (lane >= lo) & (lane < lo + MOE_EPG)
    el = jnp.where(emask, logit, ninf)
    v1 = jnp.max(el, axis=-1, keepdims=True)
    i1 = jnp.min(jnp.where(el == v1, lane, 1 << 20), axis=-1, keepdims=True)
    el2 = jnp.where(lane == i1, ninf, el)
    v2 = jnp.max(el2, axis=-1, keepdims=True)
    i2 = jnp.min(jnp.where(el2 == v2, lane, 1 << 20), axis=-1, keepdims=True)
    t = jnp.exp(v2 - v1)
    w1 = gw / (1.0 + t)
    w2 = gw * t / (1.0 + t)
    route = jnp.where(lane == 0, (i1 - MOE_GROUPS).astype(F32),
                      jnp.where(lane == 1, (i2 - MOE_GROUPS).astype(F32),
                                jnp.where(lane == 2, w1, jnp.where(lane == 3, w2, 0.0))))
    route_ref[0] = route


def _merge_call(x, h, ada8, oa, la, yb, yc, s5, lw, tm):
    bsz, seq, _ = x.shape
    nt = seq // tm

    def tok(width):
        return pl.BlockSpec((1, tm, width), lambda b, i: (b, i, 0))

    def fold(g):
        dil = A_PAIRS[g][1]
        return pl.BlockSpec((1, tm // dil, dil * A_WIDTH), lambda b, i: (b, i, 0))

    def full(arr):
        return pl.BlockSpec(arr.shape, lambda b, i: (0,) * arr.ndim, pipeline_mode=pl.Buffered(1))

    weights = (lw["w_gate"], lw["b_gate"], lw["w_branch"], lw["w_out"], lw["w_glu"], lw["b_glu"],
               lw["ln1_g"], lw["ln1_b"], lw["w_route"], lw["b_route"])
    return pl.pallas_call(
        _merge_kernel,
        grid=(bsz, nt),
        in_specs=[tok(D_MODEL), tok(D_MODEL), pl.BlockSpec((1, 8, D_MODEL), lambda b, i: (b, 0, 0)),
                  tok(A_WIDTH), fold(1), fold(2), tok(A_WIDTH), fold(1), fold(2),
                  tok(B_WIDTH), tok(C_WIDTH), tok(D_WIDTH)] + [full(w) for w in weights],
        out_specs=[tok(D_MODEL), pl.BlockSpec((tm * ROW_TILE, 128), lambda b, i: (b * nt + i, 0)), tok(128)],
        out_shape=[
            jax.ShapeDtypeStruct((bsz, seq, D_MODEL), F32),
            jax.ShapeDtypeStruct((bsz * seq * ROW_TILE, 128), F32),
            jax.ShapeDtypeStruct((bsz, seq, 128), F32),
        ],
        scratch_shapes=[pltpu.VMEM((A_WIDTH // 128, tm, 128), F32)] * 4,
        compiler_params=_cparams(("parallel", "parallel")),
        name="merge",
    )(x, h, ada8, oa[0], oa[1], oa[2], la[0], la[1], la[2], yb, yc, s5, *weights)


def _moe_plan(eid, tm):
    ntok = eid.shape[0]
    npair = 2 * ntok
    blk = 256
    e_flat = jnp.concatenate([eid[:, 0], eid[:, 1]])
    experts = jnp.arange(N_EXPERTS, dtype=jnp.int32)
    oh = (e_flat[:, None] == experts[None, :]).astype(F32).reshape(npair // blk, blk, N_EXPERTS)
    tri = (jnp.arange(blk)[:, None] >= jnp.arange(blk)[None, :]).astype(F32)
    within = jnp.einsum("ij,bjk->bik", tri, oh, precision=HIGHEST)
    btot = within[:, -1, :]
    before = jnp.cumsum(btot, axis=0) - btot
    rank = (within - 1.0 + before[:, None, :]).reshape(npair, N_EXPERTS)
    oh = oh.reshape(npair, N_EXPERTS)
    counts = jnp.sum(btot, axis=0).astype(jnp.int32)
    pcounts = (counts + tm - 1) // tm * tm
    pend = jnp.cumsum(pcounts)
    pstart = pend - pcounts
    pos = jnp.sum(oh * (rank + pstart.astype(F32)[None, :]), axis=1).astype(jnp.int32)
    nrow = npair + N_EXPERTS * tm
    ntile = nrow // tm
    tile_expert = jnp.sum((pend[None, :] <= (jnp.arange(ntile, dtype=jnp.int32) * tm)[:, None]).astype(jnp.int32), axis=1)
    tile_expert = jnp.minimum(tile_expert, N_EXPERTS - 1)
    n_used = (pend[-1] // tm).astype(jnp.int32).reshape(1)
    padcnt = pcounts - counts
    padend = jnp.cumsum(padcnt)
    j = jnp.arange(N_EXPERTS * tm, dtype=jnp.int32)
    e_j = jnp.sum((padend[None, :] <= j[:, None]).astype(jnp.int32), axis=1)
    shift = pstart + counts - (padend - padcnt)
    row_pad = jnp.sum((e_j[:, None] == experts[None, :]).astype(jnp.int32) * shift[None, :], axis=1) + j
    zero_rows = jnp.where(e_j < N_EXPERTS, row_pad, pend[-1] + j - padend[-1])
    return pos, zero_rows, tile_expert, n_used


def _dispatch_kernel(pos_ref, zr_ref, h_ref, xs_ref, zbuf, sem, *, ntok, tm, nz):
    i = pl.program_id(0)

    def issue(r, carry):
        src = h_ref.at[pl.ds(pl.multiple_of(r * ROW_TILE, ROW_TILE), ROW_TILE)]
        for k in range(2):
            row = pos_ref[k * ntok + i * tm + r]
            dst = xs_ref.at[pl.ds(pl.multiple_of(row * ROW_TILE, ROW_TILE), ROW_TILE)]
            pltpu.make_async_copy(src, dst, sem).start()
        return carry

    lax.fori_loop(0, tm, issue, 0)
    zbuf[...] = jnp.zeros_like(zbuf)

    def zero(r, carry):
        row = zr_ref[i * nz + r]
        dst = xs_ref.at[pl.ds(pl.multiple_of(row * ROW_TILE, ROW_TILE), ROW_TILE)]
        pltpu.make_async_copy(zbuf, dst, sem).start()
        return carry

    lax.fori_loop(0, nz, zero, 0)
    for _ in range(2):
        pltpu.make_async_copy(h_ref, xs_ref.at[pl.ds(0, tm * ROW_TILE)], sem).wait()
    pltpu.make_async_copy(h_ref.at[pl.ds(0, nz * ROW_TILE)], xs_ref.at[pl.ds(0, nz * ROW_TILE)], sem).wait()


def _dispatch_call(h2t, pos, zero_rows, nrow, tm):
    ntok = pos.shape[0] // 2
    nstep = ntok // tm
    nz = zero_rows.shape[0] // nstep
    grid_spec = pltpu.PrefetchScalarGridSpec(
        num_scalar_prefetch=2,
        grid=(nstep,),
        in_specs=[pl.BlockSpec((tm * ROW_TILE, 128), lambda i, p, z: (i, 0))],
        out_specs=pl.BlockSpec(memory_space=pl.ANY),
        scratch_shapes=[pltpu.VMEM((ROW_TILE, 128), F32), pltpu.SemaphoreType.DMA(())],
    )
    return pl.pallas_call(
        functools.partial(_dispatch_kernel, ntok=ntok, tm=tm, nz=nz),
        grid_spec=grid_spec,
        out_shape=jax.ShapeDtypeStruct((nrow * ROW_TILE, 128), F32),
        compiler_params=_cparams(("arbitrary",)),
        name="moe_dispatch",
    )(pos, zero_rows, h2t)


def _ffn_kernel(te_ref, nu_ref, xs_ref, wup_ref, wdn_ref, ys_ref, *, tm):
    del te_ref
    i = pl.program_id(0)

    @pl.when(i < nu_ref[0])
    def _():
        x = jnp.concatenate([xs_ref[pl.ds(s, tm, stride=ROW_TILE), :] for s in range(ROW_TILE)], axis=1)
        a = jnp.dot(x.astype(BF16), wup_ref[0], preferred_element_type=F32)
        g = a[:, :MOE_FF]
        hid = g * _sigmoid(g) * a[:, MOE_FF:]
        y = jnp.dot(hid.astype(BF16), wdn_ref[0], preferred_element_type=F32)
        for s in range(ROW_TILE):
            ys_ref[pl.ds(s, tm, stride=ROW_TILE), :] = y[:, s * 128:(s + 1) * 128]

    @pl.when(i >= nu_ref[0])
    def _():
        ys_ref[...] = jnp.zeros_like(ys_ref)


def _ffn_call(xs, tile_expert, n_used, w_up, w_down):
    tm = MOE_TM
    ntile = tile_expert.shape[0]
    grid_spec = pltpu.PrefetchScalarGridSpec(
        num_scalar_prefetch=2,
        grid=(ntile,),
        in_specs=[
            pl.BlockSpec((tm * ROW_TILE, 128), lambda i, te, nu: (i, 0)),
            pl.BlockSpec((1, D_MODEL, 2 * MOE_FF), lambda i, te, nu: (te[i], 0, 0)),
            pl.BlockSpec((1, MOE_FF, D_MODEL), lambda i, te, nu: (te[i], 0, 0)),
        ],
        out_specs=pl.BlockSpec((tm * ROW_TILE, 128), lambda i, te, nu: (i, 0)),
    )
    return pl.pallas_call(
        functools.partial(_ffn_kernel, tm=tm),
        grid_spec=grid_spec,
        out_shape=jax.ShapeDtypeStruct(xs.shape, F32),
        compiler_params=_cparams(("arbitrary",)),
        name="moe_ffn",
    )(tile_expert, n_used, xs, w_up, w_down)


def _final_kernel(pos_ref, x1_ref, route_ref, ada_ref, lng_ref, lnb_ref, ys_ref, o_ref, buf, sem,
                  *, ntok, tm, nstep):
    i = pl.program_id(0)
    slot = i % 2

    def fetch(step, to_slot):
        def issue(r, carry):
            for k in range(2):
                row = pos_ref[k * ntok + step * tm + r]
                src = ys_ref.at[pl.ds(pl.multiple_of(row * ROW_TILE, ROW_TILE), ROW_TILE)]
                dst = buf.at[to_slot, k, pl.ds(pl.multiple_of(r * ROW_TILE, ROW_TILE), ROW_TILE)]
                pltpu.make_async_copy(src, dst, sem.at[to_slot]).start()
            return carry

        lax.fori_loop(0, tm, issue, 0)

    @pl.when(i == 0)
    def _():
        fetch(0, 0)

    @pl.when(i + 1 < nstep)
    def _():
        fetch(i + 1, 1 - slot)

    for k in range(2):
        pltpu.make_async_copy(ys_ref.at[pl.ds(0, tm * ROW_TILE)], buf.at[slot, k], sem.at[slot]).wait()
    ada = ada_ref[0]
    route = route_ref[0]
    w0, w1 = route[:, 2:3], route[:, 3:4]
    ffn = jnp.concatenate(
        [w0 * buf[slot, 0, pl.ds(s, tm, stride=ROW_TILE), :] + w1 * buf[slot, 1, pl.ds(s, tm, stride=ROW_TILE), :]
         for s in range(ROW_TILE)], axis=1)
    o_ref[0] = _ln(ALPHA * x1_ref[0] + (1.0 + ada[5:6]) * ffn) * lng_ref[...] + lnb_ref[...]


def _final_call(x1, ys, pos, route, ada8, ln_g, ln_b, tm):
    bsz, seq, _ = x1.shape
    nt = seq // tm
    nstep = bsz * nt
    grid_spec = pltpu.PrefetchScalarGridSpec(
        num_scalar_prefetch=1,
        grid=(nstep,),
        in_specs=[
            pl.BlockSpec((1, tm, D_MODEL), lambda i, p: (i // nt, i % nt, 0)),
            pl.BlockSpec((1, tm, 128), lambda i, p: (i // nt, i % nt, 0)),
            pl.BlockSpec((1, 8, D_MODEL), lambda i, p: (i // nt, 0, 0)),
            pl.BlockSpec((1, D_MODEL), lambda i, p: (0, 0)),
            pl.BlockSpec((1, D_MODEL), lambda i, p: (0, 0)),
            pl.BlockSpec(memory_space=pl.ANY),
        ],
        out_specs=pl.BlockSpec((1, tm, D_MODEL), lambda i, p: (i // nt, i % nt, 0)),
        scratch_shapes=[pltpu.VMEM((2, 2, tm * ROW_TILE, 128), F32), pltpu.SemaphoreType.DMA((2,))],
    )
    return pl.pallas_call(
        functools.partial(_final_kernel, ntok=bsz * seq, tm=tm, nstep=nstep),
        grid_spec=grid_spec,
        out_shape=jax.ShapeDtypeStruct((bsz, seq, D_MODEL), F32),
        compiler_params=_cparams(("arbitrary",)),
        name="final",
    )(pos, x1, route, ada8, ln_g, ln_b, ys)


def _split_w_in(w_in):
    sizes = (768, 768, 768, 512, 128, 128, 256, 256, 256, 256, N_BRANCH * D_MODEL)
    pts = [0]
    for s in sizes:
        pts.append(pts[-1] + s)
    return [w_in[:, pts[n]:pts[n + 1]] for n in range(len(sizes))]


def _layer_weights(p, l):
    a_q, a_k, a_v, b_q, b_k, b_v, c_q, c_k, c_v, d_u, w_gate = _split_w_in(p["w_in"][l])
    scale = HEAD_DIM ** -0.5
    def a_cols(g):
        gs = slice(g * A_WIDTH, (g + 1) * A_WIDTH)
        return [a_q[:, gs] * scale, a_k[:, gs], a_v[:, gs]]

    cols = [b_q * scale] + a_cols(0) + [b_k, b_v, c_q * scale, c_k, c_v, d_u] + a_cols(1) + a_cols(2)
    w_route = jnp.concatenate([p["w_route_g"][l], p["w_route_e"][l]], axis=1)
    b_route = jnp.concatenate([p["b_route_g"][l], p["b_route_e"][l]])
    npad = 128 - w_route.shape[1]
    return {
        "w_qkv": jnp.concatenate(cols, axis=1).astype(BF16),
        "w_gate": w_gate.astype(BF16),
        "b_gate": p["b_gate"][l].reshape(1, -1),
        "w_branch": p["w_branch"][l].astype(BF16),
        "w_out": p["w_out"][l].astype(BF16),
        "w_glu": p["w_glu"][l].astype(BF16),
        "b_glu": p["b_glu"][l].reshape(1, -1),
        "ln1_g": p["ln1_g"][l].reshape(1, -1),
        "ln1_b": p["ln1_b"][l].reshape(1, -1),
        "ln2_g": p["ln2_g"][l].reshape(1, -1),
        "ln2_b": p["ln2_b"][l].reshape(1, -1),
        "w_route": jnp.pad(w_route, ((0, 0), (0, npad))),
        "b_route": jnp.pad(b_route, (0, npad)).reshape(1, -1),
        "w_up": p["w_up"][l].astype(BF16),
        "w_down": p["w_down"][l].astype(BF16),
        "b_sink": p["b_sink"][l],
        "c_bias": _c_bias_table(p["c_rpb"][l]),
    }


def _layer(x, ada8, lw, ssm_prm):
    bsz, seq, _ = x.shape
    tm, tm_mm = 256, 512
    h, main, fold1, fold2 = _proj_call(x, ada8, lw["w_qkv"], tm_mm)
    oa, la = zip(*[_attn_a_call(f, g, seq) for g, f in enumerate((main, fold1, fold2))])
    yb = _attn_b_call(main, lw["b_sink"])
    yc = _attn_c_call(main, lw["c_bias"])
    s5 = _ssm_call(main, ssm_prm)
    x1, h2t, route = _merge_call(x, h, ada8, oa, la, yb, yc, s5, lw, tm_mm)
    ntok = bsz * seq
    eid = route.reshape(ntok, 128)[:, :2].astype(jnp.int32)
    pos, zero_rows, tile_expert, n_used = _moe_plan(eid, MOE_TM)
    xs = _dispatch_call(h2t, pos, zero_rows, 2 * ntok + N_EXPERTS * MOE_TM, tm)
    ys = _ffn_call(xs, tile_expert, n_used, lw["w_up"], lw["w_down"])
    return _final_call(x1, ys, pos, route, ada8, lw["ln2_g"], lw["ln2_b"], tm)


def kernel(x_prompt, x_sample, c_prompt, c_sample, w_ada, b_ada, w_in, b_gate, b_sink, c_rpb, lam_re, lam_im,
           log_dt, ssm_b_re, ssm_b_im, ssm_c_re, ssm_c_im, ssm_d, w_glu, b_glu, w_branch, w_out, ln1_g, ln1_b,
           ln2_g, ln2_b, w_route_g, b_route_g, w_route_e, b_route_e, w_up, w_down):
    p = dict(w_in=w_in, b_gate=b_gate, b_sink=b_sink, c_rpb=c_rpb, w_glu=w_glu, b_glu=b_glu, w_branch=w_branch,
             w_out=w_out, ln1_g=ln1_g, ln1_b=ln1_b, ln2_g=ln2_g, ln2_b=ln2_b, w_route_g=w_route_g,
             b_route_g=b_route_g, w_route_e=w_route_e, b_route_e=b_route_e, w_up=w_up, w_down=w_down)
    nbp, nbs = c_prompt.shape[0], c_sample.shape[0]
    c_all = jnp.concatenate([c_prompt, c_sample], axis=0)
    c_all = jnp.pad(c_all, ((0, -c_all.shape[0] % 8), (0, 0)))
    ada = _ada_call(c_all, w_ada, b_ada).reshape(DEPTH, c_all.shape[0], 6, D_MODEL)
    ada = jnp.pad(ada, ((0, 0), (0, 0), (0, 2), (0, 0)))
    m_mat, bend, cin, apow = _ssm_param_call(lam_re, lam_im, log_dt, ssm_b_re, ssm_b_im, ssm_c_re, ssm_c_im, ssm_d)
    xs = [x_prompt, x_sample]
    for l in range(DEPTH):
        lw = _layer_weights(p, l)
        prm = (m_mat[l * SSM_GROUPS:(l + 1) * SSM_GROUPS], bend[2 * l:2 * l + 2], cin[2 * l:2 * l + 2],
               apow[2 * l:2 * l + 2])
        xs = [_layer(xs[0], ada[l, :nbp], lw, prm), _layer(xs[1], ada[l, nbp:nbp + nbs], lw, prm)]
    return (xs[0], xs[1])
```

```python
import functools
import math

import jax
import jax.numpy as jnp
from jax import lax
from jax.experimental import pallas as pl
from jax.experimental.pallas import tpu as pltpu

F32 = jnp.float32
BF16 = jnp.bfloat16
HIGHEST = lax.Precision.HIGHEST

D_MODEL = 1024
DEPTH = 2
HEAD_DIM = 64
A_PAIRS = ((128, 1), (512, 4), (2048, 16))
A_GROUPS = 3
A_HEADS = 4
A_WIDTH = A_HEADS * HEAD_DIM
B_Q_HEADS = 8
B_KV_HEADS = 2
B_WINDOW = 128
B_WIDTH = B_Q_HEADS * HEAD_DIM
GRID_W = 64
C_HEADS = 4
C_WIN_R = 8
C_WIN_C = 16
C_WIDTH = C_HEADS * HEAD_DIM
D_WIDTH = 256
SSM_GROUP = 16
SSM_GROUPS = D_WIDTH // SSM_GROUP
SSM_STATE = 64
N_BRANCH = 4
MOE_GROUPS = 4
MOE_EPG = 8
N_EXPERTS = MOE_GROUPS * MOE_EPG
MOE_FF = 512
ALPHA = (2 * DEPTH) ** 0.25
LN_EPS = 1e-5
NEG = -1e30

QKV_COLS = 4096
MAIN_COLS = 2560
COL_BQ = 0
COL_A0 = 2
COL_BKV = 5
COL_CQ = 6
COL_DU = 9
A_QKV = 3 * A_WIDTH
ROW_TILE = 8
A_HALF = 64
ATTN_SQ = 128
B_INTERLEAVE = 8
MERGE_COLS = 256
SSM_CHUNK = 64
SSM_CW = SSM_CHUNK * SSM_GROUP
MOE_TM = 256
VMEM_LIMIT = 56 * 1024 * 1024


def _cparams(sem):
    return pltpu.CompilerParams(dimension_semantics=sem, vmem_limit_bytes=VMEM_LIMIT)


def _sigmoid(x):
    return 1.0 / (1.0 + jnp.exp(-x))


def _ln(x):
    mu = jnp.mean(x, axis=-1, keepdims=True)
    xc = x - mu
    var = jnp.mean(xc * xc, axis=-1, keepdims=True)
    return xc * lax.rsqrt(var + LN_EPS)


def _dot_nt(a, b, **kw):
    return lax.dot_general(a, b, (((1,), (1,)), ((), ())), preferred_element_type=F32, **kw)


def _split_head_pairs(q):
    lane = lax.broadcasted_iota(jnp.int32, q.shape, 1)
    even = (lane % 128) < HEAD_DIM
    zero = jnp.zeros_like(q)
    return jnp.where(even, q, zero), jnp.where(even, zero, q)


def _alibi(n):
    return [2.0 ** (-8.0 * (i + 1) / n) for i in range(n)]


def _ada_kernel(c_ref, w_ref, b_ref, o_ref):
    c = c_ref[...]
    sc = c * _sigmoid(c)
    o_ref[0] = jnp.dot(sc, w_ref[0], precision=HIGHEST, preferred_element_type=F32) + b_ref[0]


def _ada_call(c_all, w_ada, b_ada):
    nb = c_all.shape[0]
    return pl.pallas_call(
        _ada_kernel,
        grid=(DEPTH, 6),
        in_specs=[
            pl.BlockSpec((nb, D_MODEL), lambda l, j: (0, 0)),
            pl.BlockSpec((1, D_MODEL, D_MODEL), lambda l, j: (l, 0, j)),
            pl.BlockSpec((1, 1, D_MODEL), lambda l, j: (l, 0, j)),
        ],
        out_specs=pl.BlockSpec((1, nb, D_MODEL), lambda l, j: (l, 0, j)),
        out_shape=jax.ShapeDtypeStruct((DEPTH, nb, 6 * D_MODEL), F32),
        compiler_params=_cparams(("arbitrary", "arbitrary")),
        name="ada",
    )(c_all, w_ada, b_ada.reshape(DEPTH, 1, 6 * D_MODEL))


def _proj_kernel(x_ref, ada_ref, w_ref, h_ref, main_ref, f1_ref, f2_ref, scr_ref, *, chunk):
    x = x_ref[0]
    ada = ada_ref[0]
    tm = x.shape[0]
    h = _ln(x) * (1.0 + ada[1:2]) + ada[0:1]
    hb = h.astype(BF16)
    h_ref[0] = hb
    for j in range(MAIN_COLS // chunk):
        sl = slice(j * chunk, (j + 1) * chunk)
        main_ref[0, :, sl] = jnp.dot(hb, w_ref[:, sl], preferred_element_type=F32).astype(BF16)
    for g, f_ref in ((1, f1_ref), (2, f2_ref)):
        dil = A_PAIRS[g][1]
        c0 = MAIN_COLS + (g - 1) * A_QKV
        res = jnp.dot(hb, w_ref[:, c0:c0 + A_QKV], preferred_element_type=F32)
        for c in range(A_QKV // 128):
            scr_ref[c] = res[:, c * 128:(c + 1) * 128]
        for r in range(dil):
            for c in range(A_QKV // 128):
                lo = r * A_QKV + c * 128
                f_ref[0, :, lo:lo + 128] = scr_ref[c, pl.ds(r, tm // dil, stride=dil), :].astype(BF16)


def _proj_call(x, ada8, w_qkv, tm):
    bsz, seq, _ = x.shape
    d1, d2 = A_PAIRS[1][1], A_PAIRS[2][1]
    return pl.pallas_call(
        functools.partial(_proj_kernel, chunk=512),
        grid=(bsz, seq // tm),
        in_specs=[
            pl.BlockSpec((1, tm, D_MODEL), lambda b, i: (b, i, 0)),
            pl.BlockSpec((1, 8, D_MODEL), lambda b, i: (b, 0, 0)),
            pl.BlockSpec((D_MODEL, QKV_COLS), lambda b, i: (0, 0), pipeline_mode=pl.Buffered(1)),
        ],
        out_specs=[
            pl.BlockSpec((1, tm, D_MODEL), lambda b, i: (b, i, 0)),
            pl.BlockSpec((1, tm, MAIN_COLS), lambda b, i: (b, i, 0)),
            pl.BlockSpec((1, tm // d1, d1 * A_QKV), lambda b, i: (b, i, 0)),
            pl.BlockSpec((1, tm // d2, d2 * A_QKV), lambda b, i: (b, i, 0)),
        ],
        out_shape=[
            jax.ShapeDtypeStruct((bsz, seq, D_MODEL), BF16),
            jax.ShapeDtypeStruct((bsz, seq, MAIN_COLS), BF16),
            jax.ShapeDtypeStruct((bsz, seq // d1, d1 * A_QKV), BF16),
            jax.ShapeDtypeStruct((bsz, seq // d2, d2 * A_QKV), BF16),
        ],
        scratch_shapes=[pltpu.VMEM((A_QKV // 128, tm, 128), F32)],
        compiler_params=_cparams(("parallel", "parallel")),
        name="proj",
    )(x, ada8, w_qkv)


def _attn_a_kernel(q_ref, kc_ref, kp_ref, kn_ref, vc_ref, vp_ref, vn_ref, o_ref, lse_ref,
                   *, tq, sub, slopes, dil):
    i = pl.program_id(2)
    q = q_ref[0]
    k = jnp.concatenate([kp_ref[0], kc_ref[0], kn_ref[0]], axis=0)
    v = jnp.concatenate([vp_ref[0], vc_ref[0], vn_ref[0]], axis=0)
    sq = ATTN_SQ
    kw = sq + 2 * A_HALF
    row = lax.broadcasted_iota(jnp.int32, (sq, kw), 0)
    col = lax.broadcasted_iota(jnp.int32, (sq, kw), 1)
    rel = col - A_HALF - row
    dist = jnp.abs(rel).astype(F32) * float(dil)
    even_q, odd_q = _split_head_pairs(q)
    even_lanes = (lax.broadcasted_iota(jnp.int32, (sq, 128), 1) < HEAD_DIM)
    for j in range(tq // sq):
        kpos = i * tq + j * sq - A_HALF + col
        valid = (jnp.abs(rel) <= A_HALF) & (kpos >= 0) & (kpos < sub)
        rows = slice(j * sq, (j + 1) * sq)
        kj = k[j * sq:j * sq + kw]
        vj = v[j * sq:j * sq + kw]
        hs = range(A_HEADS)
        pss = [slice((h // 2) * 128, (h // 2 + 1) * 128) for h in hs]
        ss = [_dot_nt((even_q, odd_q)[h % 2][rows, ps], kj[:, ps]) for h, ps in zip(hs, pss)]
        ss = [jnp.where(valid, s - slopes[h] * dist, NEG) for s, h in zip(ss, hs)]
        ms = [jnp.max(s, axis=-1, keepdims=True) for s in ss]
        pp = [jnp.exp(s - m) for s, m in zip(ss, ms)]
        dens = [jnp.sum(p, axis=-1, keepdims=True) for p in pp]
        outs = [jnp.dot(p.astype(BF16), vj[:, ps], preferred_element_type=F32) / den
                for p, ps, den in zip(pp, pss, dens)]
        lses = [m + jnp.log(den) for m, den in zip(ms, dens)]
        for n in range(0, A_HEADS, 2):
            o_ref[0, rows, pss[n]] = jnp.where(even_lanes, outs[n], outs[n + 1]).astype(BF16)
            lse_ref[0, rows, pss[n]] = jnp.where(even_lanes, lses[n], lses[n + 1])


def _attn_a_call(folded, g, seq):
    bsz = folded.shape[0]
    _, dil = A_PAIRS[g]
    sub = seq // dil
    tq = min(512, sub)
    nq = sub // tq
    hb = tq // A_HALF
    nhb = sub // A_HALF
    nblk = folded.shape[2] // dil // 256
    cq = COL_A0 if g == 0 else 0
    slopes = _alibi(A_GROUPS * A_HEADS)[g * A_HEADS:(g + 1) * A_HEADS]

    def cur(c):
        return pl.BlockSpec((1, tq, 256), lambda b, r, i: (b, i, r * nblk + c))

    def prev(c):
        return pl.BlockSpec((1, A_HALF, 256), lambda b, r, i: (b, jnp.maximum(i * hb - 1, 0), r * nblk + c))

    def nxt(c):
        return pl.BlockSpec((1, A_HALF, 256), lambda b, r, i: (b, jnp.minimum((i + 1) * hb, nhb - 1), r * nblk + c))

    return pl.pallas_call(
        functools.partial(_attn_a_kernel, tq=tq, sub=sub, slopes=slopes, dil=dil),
        grid=(bsz, dil, nq),
        in_specs=[cur(cq), cur(cq + 1), prev(cq + 1), nxt(cq + 1), cur(cq + 2), prev(cq + 2), nxt(cq + 2)],
        out_specs=[
            pl.BlockSpec((1, tq, A_WIDTH), lambda b, r, i: (b, i, r)),
            pl.BlockSpec((1, tq, A_WIDTH), lambda b, r, i: (b, i, r)),
        ],
        out_shape=[
            jax.ShapeDtypeStruct((bsz, sub, dil * A_WIDTH), BF16),
            jax.ShapeDtypeStruct((bsz, sub, dil * A_WIDTH), F32),
        ],
        compiler_params=_cparams(("parallel", "parallel", "parallel")),
        name=f"attn_a{g}",
    )(folded, folded, folded, folded, folded, folded, folded)


def _attn_b_kernel(sink_ref, q_ref, kvc_ref, kvp_ref, kvn_ref, o_ref, *, tq, seq, slopes):
    i = pl.program_id(1)
    q = q_ref[0]
    kv = jnp.concatenate([kvp_ref[0], kvc_ref[0], kvn_ref[0]], axis=0)
    sq = ATTN_SQ
    kw = sq + 2 * B_WINDOW
    row = lax.broadcasted_iota(jnp.int32, (sq, kw), 0)
    col = lax.broadcasted_iota(jnp.int32, (sq, kw), 1)
    rel = col - B_WINDOW - row
    dist = jnp.abs(rel).astype(F32)
    rep = B_Q_HEADS // B_KV_HEADS
    even_q, odd_q = _split_head_pairs(q)
    even_lanes = (lax.broadcasted_iota(jnp.int32, (sq, 128), 1) < HEAD_DIM)

    def doubled(c):
        piece = kv[:, c * HEAD_DIM:(c + 1) * HEAD_DIM]
        return jnp.concatenate([piece, piece], axis=1)

    k2 = [doubled(g) for g in range(B_KV_HEADS)]
    v2 = [doubled(B_KV_HEADS + g) for g in range(B_KV_HEADS)]
    for j in range(tq // sq):
        kpos = i * tq + j * sq - B_WINDOW + col
        valid = (jnp.abs(rel) <= B_WINDOW) & (kpos >= 0) & (kpos < seq)
        rows = slice(j * sq, (j + 1) * sq)
        for h0 in range(0, B_Q_HEADS, B_INTERLEAVE):
            hs = range(h0, h0 + B_INTERLEAVE)
            qms = [(even_q, odd_q)[h % 2][rows, (h // 2) * 128:(h // 2 + 1) * 128] for h in hs]
            kjs = [k2[h // rep][j * sq:j * sq + kw] for h in hs]
            vjs = [v2[h // rep][j * sq:j * sq + kw] for h in hs]
            sks = [sink_ref[h] for h in hs]
            ss = [_dot_nt(qm, kj) for qm, kj in zip(qms, kjs)]
            ss = [jnp.where(valid, s - slopes[h] * dist, NEG) for s, h in zip(ss, hs)]
            ms = [jnp.maximum(jnp.max(s, axis=-1, keepdims=True), sk) for s, sk in zip(ss, sks)]
            pp = [jnp.exp(s - m) for s, m in zip(ss, ms)]
            dens = [jnp.sum(p, axis=-1, keepdims=True) + jnp.exp(sk - m) for p, sk, m in zip(pp, sks, ms)]
            outs = [jnp.dot(p.astype(BF16), vj, preferred_element_type=F32) / den
                    for p, vj, den in zip(pp, vjs, dens)]
            for n in range(0, B_INTERLEAVE, 2):
                ps = slice((h0 + n) // 2 * 128, ((h0 + n) // 2 + 1) * 128)
                o_ref[0, rows, ps] = jnp.where(even_lanes, outs[n], outs[n + 1]).astype(BF16)


def _attn_b_call(qkv, sink):
    bsz, seq, _ = qkv.shape
    tq = 512
    hb = tq // B_WINDOW
    nhb = seq // B_WINDOW
    return pl.pallas_call(
        functools.partial(_attn_b_kernel, tq=tq, seq=seq, slopes=_alibi(B_Q_HEADS)),
        grid=(bsz, seq // tq),
        in_specs=[
            pl.BlockSpec(memory_space=pltpu.SMEM),
            pl.BlockSpec((1, tq, B_WIDTH), lambda b, i: (b, i, COL_BQ)),
            pl.BlockSpec((1, tq, 256), lambda b, i: (b, i, COL_BKV)),
            pl.BlockSpec((1, B_WINDOW, 256), lambda b, i: (b, jnp.maximum(i * hb - 1, 0), COL_BKV)),
            pl.BlockSpec((1, B_WINDOW, 256), lambda b, i: (b, jnp.minimum((i + 1) * hb, nhb - 1), COL_BKV)),
        ],
        out_specs=pl.BlockSpec((1, tq, B_WIDTH), lambda b, i: (b, i, 0)),
        out_shape=jax.ShapeDtypeStruct((bsz, seq, B_WIDTH), BF16),
        compiler_params=_cparams(("parallel", "parallel")),
        name="attn_b",
    )(sink, qkv, qkv, qkv, qkv)


C_ROWS_PER_STEP = 8
C_INTERLEAVE_ROWS = 4


def _attn_c_kernel(q_ref, k_ref, v_ref, bias_ref, o_ref, *, rows):
    i = pl.program_id(1)
    kwin = C_WIN_R * GRID_W
    even_q, odd_q = _split_head_pairs(q_ref[0])
    even_lanes = (lax.broadcasted_iota(jnp.int32, (GRID_W, 128), 1) < HEAD_DIM)
    for jr0 in range(0, C_ROWS_PER_STEP, C_INTERLEAVE_ROWS):
        units = []
        for jr in range(jr0, jr0 + C_INTERLEAVE_ROWS):
            r = i * C_ROWS_PER_STEP + jr
            rs = jnp.clip(r - C_WIN_R // 2, 0, rows - C_WIN_R)
            start = pl.multiple_of(rs * GRID_W, GRID_W)
            kw = k_ref[0, pl.ds(start, kwin), :]
            vw = v_ref[0, pl.ds(start, kwin), :]
            for h in range(C_HEADS):
                units.append((slice(jr * GRID_W, (jr + 1) * GRID_W), h, r - rs, kw, vw))
        pss = [slice((h // 2) * 128, (h // 2 + 1) * 128) for _, h, _, _, _ in units]
        ss = [_dot_nt((even_q, odd_q)[h % 2][qrows, ps], kw[:, ps]) + bias_ref[h, delta]
              for (qrows, h, delta, kw, _), ps in zip(units, pss)]
        ms = [jnp.max(s, axis=-1, keepdims=True) for s in ss]
        pp = [jnp.exp(s - m) for s, m in zip(ss, ms)]
        dens = [jnp.sum(p, axis=-1, keepdims=True) for p in pp]
        outs = [jnp.dot(p.astype(BF16), vw[:, ps], preferred_element_type=F32) / den
                for p, (_, _, _, _, vw), ps, den in zip(pp, units, pss, dens)]
        for n in range(0, len(units), 2):
            o_ref[0, units[n][0], pss[n]] = jnp.where(even_lanes, outs[n], outs[n + 1]).astype(BF16)


def _c_bias_table(rpb):
    qc = jnp.arange(GRID_W)[:, None]
    kc = jnp.arange(GRID_W)[None, :]
    col_off = jnp.clip(kc - qc + (C_WIN_C - 1), 0, 2 * C_WIN_C - 2)
    sel = (col_off[None] == jnp.arange(2 * C_WIN_C - 1)[:, None, None]).astype(F32)
    band = jnp.einsum("hrc,cqk->hrqk", rpb.astype(F32), sel, precision=HIGHEST)
    cs = jnp.clip(qc - C_WIN_C // 2, 0, GRID_W - C_WIN_C)
    valid = (kc >= cs) & (kc < cs + C_WIN_C)
    band = jnp.where(valid[None, None], band, NEG)
    tabs = [band[:, C_WIN_R - 1 - d:2 * C_WIN_R - 1 - d].transpose(0, 2, 1, 3) for d in range(C_WIN_R)]
    return jnp.stack(tabs, axis=1).reshape(C_HEADS, C_WIN_R, GRID_W, C_WIN_R * GRID_W)


def _attn_c_call(qkv, bias_tab):
    bsz, seq, _ = qkv.shape
    rows = seq // GRID_W
    tq = C_ROWS_PER_STEP * GRID_W
    return pl.pallas_call(
        functools.partial(_attn_c_kernel, rows=rows),
        grid=(bsz, seq // tq),
        in_specs=[
            pl.BlockSpec((1, tq, C_WIDTH), lambda b, i: (b, i, COL_CQ)),
            pl.BlockSpec((1, seq, C_WIDTH), lambda b, i: (b, 0, COL_CQ + 1)),
            pl.BlockSpec((1, seq, C_WIDTH), lambda b, i: (b, 0, COL_CQ + 2)),
            pl.BlockSpec(bias_tab.shape, lambda b, i: (0, 0, 0, 0)),
        ],
        out_specs=pl.BlockSpec((1, tq, C_WIDTH), lambda b, i: (b, i, 0)),
        out_shape=jax.ShapeDtypeStruct((bsz, seq, C_WIDTH), BF16),
        compiler_params=_cparams(("parallel", "arbitrary")),
        name="attn_c",
    )(qkv, qkv, qkv, bias_tab)


def _cmul(ar, ai, br, bi):
    return ar * br - ai * bi, ar * bi + ai * br


def _ssm_param_kernel(lre_ref, lim_ref, ldt_ref, cre_ref, cim_ref, bre_ref, bim_ref, d_ref,
                      m_ref, bend_ref, cin_ref, apow_ref):
    tc, hh, pp = SSM_CHUNK, SSM_GROUP, SSM_STATE
    it = lax.broadcasted_iota(jnp.int32, (tc, pp), 0).astype(F32)
    lane = lax.broadcasted_iota(jnp.int32, (hh, 128), 1)
    subl = lax.broadcasted_iota(jnp.int32, (hh, 128), 0)
    tabs = []
    for d in range(2):
        lr, li = lre_ref[d, 0, 0], lim_ref[d, 0, 0]
        dt = jnp.exp(ldt_ref[d, 0, 0])

        def powers(kv):
            mag = jnp.exp(kv * (lr * dt))
            ang = kv * (li * dt)
            return mag * jnp.cos(ang), mag * jnp.sin(ang)

        one = jnp.ones((1, pp), F32)
        lbr, lbi = powers(one)
        den = lr * lr + li * li
        nr, ni = lbr - 1.0, lbi
        fr = (nr * lr + ni * li) / den
        fi = (ni * lr - nr * li) / den
        apr, api = powers(one * float(tc))
        apow_ref[d, 0] = jnp.concatenate([apr, api], axis=1)

        cr, ci = cre_ref[d, 0], cim_ref[d, 0]
        br, bi = bre_ref[d, 0], bim_ref[d, 0]

        wr, wi = powers(it + 1.0 if d == 0 else float(tc) - it)
        cwr, cwi = _cmul(cr[None], ci[None], wr[:, None, :], wi[:, None, :])
        cin = jnp.concatenate([cwr, -cwi], axis=2).reshape(tc * hh, 2 * pp)
        cin_ref[d, 0] = cin.astype(BF16)

        wr, wi = powers(float(tc - 1) - it if d == 0 else it)
        wfr, wfi = _cmul(wr, wi, fr, fi)
        er, ei = _cmul(br[None], bi[None], wfr[:, None, :], wfi[:, None, :])
        bend_ref[d, 0] = jnp.concatenate([er, ei], axis=2).reshape(tc * hh, 2 * pp).astype(BF16)

        wr, wi = powers(it if d == 0 else float(tc - 1) - it)
        wfr, wfi = _cmul(wr, wi, fr, fi)
        gr, gi = _cmul(cr[None], ci[None], wfr[:, None, :], wfi[:, None, :])
        gr = gr.reshape(tc * hh, pp)
        gi = gi.reshape(tc * hh, pp)
        tabs.append(_dot_nt(br, gr, precision=HIGHEST) - _dot_nt(bi, gi, precision=HIGHEST))

    ktf, ktb = tabs
    last = pltpu.roll(ktb[:, SSM_CW - 128:], hh, axis=1)
    diag = jnp.where(lane == subl, d_ref[0], 0.0)
    first = ktf[:, :128] + jnp.where(lane < hh, last + diag, 0.0)
    ktf = jnp.concatenate([first, ktf[:, 128:]], axis=1)
    lane_w = lax.broadcasted_iota(jnp.int32, (hh, SSM_CW), 1)
    ktb = jnp.where(lane_w < SSM_CW - hh, ktb, 0.0)
    zeros = jnp.zeros((hh, SSM_CW), F32)
    pf = jnp.concatenate([ktf, zeros], axis=1)
    pb = jnp.concatenate([ktb, zeros], axis=1)
    for j in range(tc):
        blk = pf if j == 0 else pltpu.roll(pf, j * hh, axis=1)
        sh = (tc - 1 - j) * hh
        blb = pb if sh == 0 else pltpu.roll(pb, 2 * SSM_CW - sh, axis=1)
        m_ref[0, j * hh:(j + 1) * hh, :] = (blk[:, :SSM_CW] + blb[:, :SSM_CW]).astype(BF16)


def _ssm_param_call(lam_re, lam_im, log_dt, b_re, b_im, c_re, c_im, d_skip):
    gg, pp, hh = SSM_GROUPS, SSM_STATE, SSM_GROUP
    nl = lam_re.shape[0]
    lre = lam_re.reshape(nl * 2, gg, 1, 1, pp)
    lim = lam_im.reshape(nl * 2, gg, 1, 1, pp)
    ldt = jnp.broadcast_to(log_dt.reshape(nl * 2, gg, 1, 1, 1), (nl * 2, gg, 1, 1, pp))
    cre = c_re.reshape(nl * 2, gg, hh, pp)
    cim = c_im.reshape(nl * 2, gg, hh, pp)
    bre = jnp.swapaxes(b_re, -1, -2).reshape(nl * 2, gg, hh, pp)
    bim = jnp.swapaxes(b_im, -1, -2).reshape(nl * 2, gg, hh, pp)
    dpad = jnp.pad(d_skip.reshape(nl * gg, 1, hh), ((0, 0), (0, 0), (0, 128 - hh)))

    def vec(l, g):
        return (l, g, 0, 0, 0)

    def mat(l, g):
        return (l, g, 0, 0)

    return pl.pallas_call(
        _ssm_param_kernel,
        grid=(nl, gg),
        in_specs=[
            pl.BlockSpec((2, 1, 1, 1, pp), vec), pl.BlockSpec((2, 1, 1, 1, pp), vec),
            pl.BlockSpec((2, 1, 1, 1, pp), vec),
            pl.BlockSpec((2, 1, hh, pp), mat), pl.BlockSpec((2, 1, hh, pp), mat),
            pl.BlockSpec((2, 1, hh, pp), mat), pl.BlockSpec((2, 1, hh, pp), mat),
            pl.BlockSpec((1, 1, 128), lambda l, g: (l * gg + g, 0, 0)),
        ],
        out_specs=[
            pl.BlockSpec((1, SSM_CW, SSM_CW), lambda l, g: (l * gg + g, 0, 0)),
            pl.BlockSpec((2, 1, SSM_CW, 2 * pp), mat),
            pl.BlockSpec((2, 1, SSM_CW, 2 * pp), mat),
            pl.BlockSpec((2, 1, 1, 2 * pp), mat),
        ],
        out_shape=[
            jax.ShapeDtypeStruct((nl * gg, SSM_CW, SSM_CW), BF16),
            jax.ShapeDtypeStruct((nl * 2, gg, SSM_CW, 2 * pp), BF16),
            jax.ShapeDtypeStruct((nl * 2, gg, SSM_CW, 2 * pp), BF16),
            jax.ShapeDtypeStruct((nl * 2, gg, 1, 2 * pp), F32),
        ],
        compiler_params=_cparams(("arbitrary", "arbitrary")),
        name="ssm_param",
    )(lre, lim, ldt, cre, cim, bre, bim, dpad)


def _ssm_state_kernel(u_ref, bend_ref, s_ref):
    for g in range(SSM_GROUPS):
        u = u_ref[0, g]
        for d in range(2):
            s_ref[0, d, :, g * 128:(g + 1) * 128] = jnp.dot(u, bend_ref[d, g], preferred_element_type=F32)


def _ssm_scan_kernel(s_ref, apow_ref, x_ref, *, nc):
    width = SSM_GROUPS * 128
    lane = lax.broadcasted_iota(jnp.int32, (1, width), 1)
    first_half = (lane % 128) < SSM_STATE
    for d in range(2):
        a = apow_ref[d]
        a_re = jnp.where(first_half, a, pltpu.roll(a, SSM_STATE, axis=1))
        a_im = jnp.where(first_half, -pltpu.roll(a, width - SSM_STATE, axis=1), a)

        def body(c, x):
            cc = c if d == 0 else nc - 1 - c
            x_ref[0, d, pl.ds(cc, 1), :] = x
            x_sw_lo = pltpu.roll(x, SSM_STATE, axis=1)
            x_sw_hi = pltpu.roll(x, width - SSM_STATE, axis=1)
            x_sw = jnp.where(first_half, x_sw_hi, x_sw_lo)
            return a_re * x + a_im * x_sw + s_ref[0, d, pl.ds(cc, 1), :]

        lax.fori_loop(0, nc, body, jnp.zeros((1, width), F32))


def _ssm_out_kernel(u_ref, m_ref, cin_ref, x_ref, y_ref):
    u = u_ref[0, 0]
    y = jnp.dot(u, m_ref[0], preferred_element_type=F32)
    for d in range(2):
        y = y + _dot_nt(x_ref[0, d].astype(BF16), cin_ref[d, 0])
    y_ref[0, 0] = y


def _ssm_call(qkv, prm):
    m_mat, bend, cin, apow = prm
    bsz, seq, _ = qkv.shape
    gg, tc, hh = SSM_GROUPS, SSM_CHUNK, SSM_GROUP
    nc = seq // tc
    u = qkv[:, :, COL_DU * 256:(COL_DU + 1) * 256]
    u = u.reshape(bsz, nc, tc, gg, hh).transpose(0, 3, 1, 2, 4).reshape(bsz, gg, nc, SSM_CW)
    width = gg * 128
    s = pl.pallas_call(
        _ssm_state_kernel,
        grid=(bsz,),
        in_specs=[
            pl.BlockSpec((1, gg, nc, SSM_CW), lambda b: (b, 0, 0, 0)),
            pl.BlockSpec((2, gg, SSM_CW, 128), lambda b: (0, 0, 0, 0)),
        ],
        out_specs=pl.BlockSpec((1, 2, nc, width), lambda b: (b, 0, 0, 0)),
        out_shape=jax.ShapeDtypeStruct((bsz, 2, nc, width), F32),
        compiler_params=_cparams(("parallel",)),
        name="ssm_state",
    )(u, bend)
    xprev = pl.pallas_call(
        functools.partial(_ssm_scan_kernel, nc=nc),
        grid=(bsz,),
        in_specs=[
            pl.BlockSpec((1, 2, nc, width), lambda b: (b, 0, 0, 0)),
            pl.BlockSpec((2, 1, width), lambda b: (0, 0, 0)),
        ],
        out_specs=pl.BlockSpec((1, 2, nc, width), lambda b: (b, 0, 0, 0)),
        out_shape=jax.ShapeDtypeStruct((bsz, 2, nc, width), F32),
        compiler_params=_cparams(("parallel",)),
        name="ssm_scan",
    )(s, apow.reshape(2, 1, width))
    y = pl.pallas_call(
        _ssm_out_kernel,
        grid=(gg, bsz),
        in_specs=[
            pl.BlockSpec((1, 1, nc, SSM_CW), lambda g, b: (b, g, 0, 0)),
            pl.BlockSpec((1, SSM_CW, SSM_CW), lambda g, b: (g, 0, 0)),
            pl.BlockSpec((2, 1, SSM_CW, 128), lambda g, b: (0, g, 0, 0)),
            pl.BlockSpec((1, 2, nc, 128), lambda g, b: (b, 0, 0, g)),
        ],
        out_specs=pl.BlockSpec((1, 1, nc, SSM_CW), lambda g, b: (b, g, 0, 0)),
        out_shape=jax.ShapeDtypeStruct((bsz, gg, nc, SSM_CW), F32),
        compiler_params=_cparams(("arbitrary", "arbitrary")),
        name="ssm_out",
    )(u, m_mat, cin, xprev)
    return y.reshape(bsz, gg, nc, tc, hh).transpose(0, 2, 3, 1, 4).reshape(bsz, seq, D_WIDTH)


def _merge_kernel(x_ref, h_ref, ada_ref,
                  oa0_ref, oa1_ref, oa2_ref, la0_ref, la1_ref, la2_ref, yb_ref, yc_ref, s5_ref,
                  wg_ref, bg_ref, wbr_ref, wout_ref, wglu_ref, bglu_ref, lng_ref, lnb_ref,
                  wr_ref, br_ref,
                  x1_ref, h2_ref, route_ref, o1_scr, l1_scr, o2_scr, l2_scr):
    ada = ada_ref[0]
    hb = h_ref[0]
    tm = hb.shape[0]
    for g, (o_ref, l_ref, o_scr, l_scr) in ((1, (oa1_ref, la1_ref, o1_scr, l1_scr)),
                                            (2, (oa2_ref, la2_ref, o2_scr, l2_scr))):
        dil = A_PAIRS[g][1]
        for r in range(dil):
            for c in range(A_WIDTH // 128):
                cs = slice(r * A_WIDTH + c * 128, r * A_WIDTH + (c + 1) * 128)
                o_scr[c, pl.ds(r, tm // dil, stride=dil), :] = o_ref[0, :, cs].astype(F32)
                l_scr[c, pl.ds(r, tm // dil, stride=dil), :] = l_ref[0, :, cs]

    def unfolded(scr):
        return jnp.concatenate([scr[c] for c in range(A_WIDTH // 128)], axis=1)

    l0, l1, l2 = la0_ref[0], unfolded(l1_scr), unfolded(l2_scr)
    lm = jnp.maximum(jnp.maximum(l0, l1), l2)
    e0, e1, e2 = jnp.exp(l0 - lm), jnp.exp(l1 - lm), jnp.exp(l2 - lm)
    ya = (e0 * oa0_ref[0].astype(F32) + e1 * unfolded(o1_scr) + e2 * unfolded(o2_scr)) / (e0 + e1 + e2)
    s5 = s5_ref[0]
    gel = 0.5 * s5 * (1.0 + jnp.tanh(math.sqrt(2.0 / math.pi) * (s5 + 0.044715 * (s5 * s5 * s5))))
    z = jnp.dot(gel.astype(BF16), wglu_ref[...], preferred_element_type=F32) + bglu_ref[...]
    yd = z[:, :D_WIDTH] * _sigmoid(z[:, D_WIDTH:])
    branches = (ya.astype(BF16), yb_ref[0], yc_ref[0], yd.astype(BF16))
    offs = (0, A_WIDTH, A_WIDTH + B_WIDTH, A_WIDTH + B_WIDTH + C_WIDTH, A_WIDTH + B_WIDTH + C_WIDTH + D_WIDTH)
    merged = []
    for c0 in range(0, D_MODEL, MERGE_COLS):
        css = [slice(n * D_MODEL + c0, n * D_MODEL + c0 + MERGE_COLS) for n in range(N_BRANCH)]
        pre = [jnp.dot(hb, wg_ref[:, cs], preferred_element_type=F32) + bg_ref[:, cs] for cs in css]
        brs = [jnp.dot(yb, wbr_ref[offs[n]:offs[n + 1], c0:c0 + MERGE_COLS], preferred_element_type=F32)
               for n, yb in enumerate(branches)]
        terms = [_sigmoid(g) * b for g, b in zip(pre, brs)]
        merged.append(((terms[0] + terms[1]) + terms[2]) + terms[3])
    merged = jnp.concatenate(merged, axis=1)
    mix = jnp.dot(merged.astype(BF16), wout_ref[...], preferred_element_type=F32)
    x1 = _ln(ALPHA * x_ref[0] + (1.0 + ada[2:3]) * mix) * lng_ref[...] + lnb_ref[...]
    x1_ref[0] = x1
    h2 = _ln(x1) * (1.0 + ada[4:5]) + ada[3:4]
    for s in range(ROW_TILE):
        h2_ref[pl.ds(s, tm, stride=ROW_TILE), :] = h2[:, s * 128:(s + 1) * 128]
    logit = jnp.dot(h2, wr_ref[...], precision=HIGHEST, preferred_element_type=F32) + br_ref[...]
    lane = lax.broadcasted_iota(jnp.int32, logit.shape, 1)
    ninf = -jnp.inf
    gmask = lane < MOE_GROUPS
    gl = jnp.where(gmask, logit, ninf)
    gmax = jnp.max(gl, axis=-1, keepdims=True)
    gsel = jnp.min(jnp.where(gl == gmax, lane, 1 << 20), axis=-1, keepdims=True)
    gw = 1.0 / jnp.sum(jnp.exp(gl - gmax), axis=-1, keepdims=True)
    lo = MOE_GROUPS + gsel * MOE_EPG
    emask = (lane >= lo) & (lane < lo + MOE_EPG)
    el = jnp.where(emask, logit, ninf)
    v1 = jnp.max(el, axis=-1, keepdims=True)
    i1 = jnp.min(jnp.where(el == v1, lane, 1 << 20), axis=-1, keepdims=True)
    el2 = jnp.where(lane == i1, ninf, el)
    v2 = jnp.max(el2, axis=-1, keepdims=True)
    i2 = jnp.min(jnp.where(el2 == v2, lane, 1 << 20), axis=-1, keepdims=True)
    t = jnp.exp(v2 - v1)
    w1 = gw / (1.0 + t)
    w2 = gw * t / (1.0 + t)
    route = jnp.where(lane == 0, (i1 - MOE_GROUPS).astype(F32),
                      jnp.where(lane == 1, (i2 - MOE_GROUPS).astype(F32),
                                jnp.where(lane == 2, w1, jnp.where(lane == 3, w2, 0.0))))
    route_ref[0] = route


def _merge_call(x, h, ada8, oa, la, yb, yc, s5, lw, tm):
    bsz, seq, _ = x.shape
    nt = seq // tm

    def tok(width):
        return pl.BlockSpec((1, tm, width), lambda b, i: (b, i, 0))

    def fold(g):
        dil = A_PAIRS[g][1]
        return pl.BlockSpec((1, tm // dil, dil * A_WIDTH), lambda b, i: (b, i, 0))

    def full(arr):
        return pl.BlockSpec(arr.shape, lambda b, i: (0,) * arr.ndim, pipeline_mode=pl.Buffered(1))

    weights = (lw["w_gate"], lw["b_gate"], lw["w_branch"], lw["w_out"], lw["w_glu"], lw["b_glu"],
               lw["ln1_g"], lw["ln1_b"], lw["w_route"], lw["b_route"])
    return pl.pallas_call(
        _merge_kernel,
        grid=(bsz, nt),
        in_specs=[tok(D_MODEL), tok(D_MODEL), pl.BlockSpec((1, 8, D_MODEL), lambda b, i: (b, 0, 0)),
                  tok(A_WIDTH), fold(1), fold(2), tok(A_WIDTH), fold(1), fold(2),
                  tok(B_WIDTH), tok(C_WIDTH), tok(D_WIDTH)] + [full(w) for w in weights],
        out_specs=[tok(D_MODEL), pl.BlockSpec((tm * ROW_TILE, 128), lambda b, i: (b * nt + i, 0)), tok(128)],
        out_shape=[
            jax.ShapeDtypeStruct((bsz, seq, D_MODEL), F32),
            jax.ShapeDtypeStruct((bsz * seq * ROW_TILE, 128), F32),
            jax.ShapeDtypeStruct((bsz, seq, 128), F32),
        ],
        scratch_shapes=[pltpu.VMEM((A_WIDTH // 128, tm, 128), F32)] * 4,
        compiler_params=_cparams(("parallel", "parallel")),
        name="merge",
    )(x, h, ada8, oa[0], oa[1], oa[2], la[0], la[1], la[2], yb, yc, s5, *weights)


def _moe_plan(eid, tm):
    ntok = eid.shape[0]
    npair = 2 * ntok
    blk = 256
    e_flat = jnp.concatenate([eid[:, 0], eid[:, 1]])
    experts = jnp.arange(N_EXPERTS, dtype=jnp.int32)
    oh = (e_flat[:, None] == experts[None, :]).astype(F32).reshape(npair // blk, blk, N_EXPERTS)
    tri = (jnp.arange(blk)[:, None] >= jnp.arange(blk)[None, :]).astype(F32)
    within = jnp.einsum("ij,bjk->bik", tri, oh, precision=HIGHEST)
    btot = within[:, -1, :]
    before = jnp.cumsum(btot, axis=0) - btot
    rank = (within - 1.0 + before[:, None, :]).reshape(npair, N_EXPERTS)
    oh = oh.reshape(npair, N_EXPERTS)
    counts = jnp.sum(btot, axis=0).astype(jnp.int32)
    pcounts = (counts + tm - 1) // tm * tm
    pend = jnp.cumsum(pcounts)
    pstart = pend - pcounts
    pos = jnp.sum(oh * (rank + pstart.astype(F32)[None, :]), axis=1).astype(jnp.int32)
    nrow = npair + N_EXPERTS * tm
    ntile = nrow // tm
    tile_expert = jnp.sum((pend[None, :] <= (jnp.arange(ntile, dtype=jnp.int32) * tm)[:, None]).astype(jnp.int32), axis=1)
    tile_expert = jnp.minimum(tile_expert, N_EXPERTS - 1)
    n_used = (pend[-1] // tm).astype(jnp.int32).reshape(1)
    padcnt = pcounts - counts
    padend = jnp.cumsum(padcnt)
    j = jnp.arange(N_EXPERTS * tm, dtype=jnp.int32)
    e_j = jnp.sum((padend[None, :] <= j[:, None]).astype(jnp.int32), axis=1)
    shift = pstart + counts - (padend - padcnt)
    row_pad = jnp.sum((e_j[:, None] == experts[None, :]).astype(jnp.int32) * shift[None, :], axis=1) + j
    zero_rows = jnp.where(e_j < N_EXPERTS, row_pad, pend[-1] + j - padend[-1])
    return pos, zero_rows, tile_expert, n_used


def _dispatch_kernel(pos_ref, zr_ref, h_ref, xs_ref, zbuf, sem, *, ntok, tm, nz):
    i = pl.program_id(0)

    def issue(r, carry):
        src = h_ref.at[pl.ds(pl.multiple_of(r * ROW_TILE, ROW_TILE), ROW_TILE)]
        for k in range(2):
            row = pos_ref[k * ntok + i * tm + r]
            dst = xs_ref.at[pl.ds(pl.multiple_of(row * ROW_TILE, ROW_TILE), ROW_TILE)]
            pltpu.make_async_copy(src, dst, sem).start(priority=k)
        return carry

    lax.fori_loop(0, tm, issue, 0)
    zbuf[...] = jnp.zeros_like(zbuf)

    def zero(r, carry):
        for k in range(2):
            row = zr_ref[i * nz + 2 * r + k]
            dst = xs_ref.at[pl.ds(pl.multiple_of(row * ROW_TILE, ROW_TILE), ROW_TILE)]
            pltpu.make_async_copy(zbuf, dst, sem).start(priority=k)
        return carry

    lax.fori_loop(0, nz // 2, zero, 0)
    for _ in range(2):
        pltpu.make_async_copy(h_ref, xs_ref.at[pl.ds(0, tm * ROW_TILE)], sem).wait()
    pltpu.make_async_copy(h_ref.at[pl.ds(0, nz * ROW_TILE)], xs_ref.at[pl.ds(0, nz * ROW_TILE)], sem).wait()


def _dispatch_call(h2t, pos, zero_rows, nrow, tm):
    ntok = pos.shape[0] // 2
    nstep = ntok // tm
    nz = zero_rows.shape[0] // nstep
    grid_spec = pltpu.PrefetchScalarGridSpec(
        num_scalar_prefetch=2,
        grid=(nstep,),
        in_specs=[pl.BlockSpec((tm * ROW_TILE, 128), lambda i, p, z: (i, 0))],
        out_specs=pl.BlockSpec(memory_space=pl.ANY),
        scratch_shapes=[pltpu.VMEM((ROW_TILE, 128), F32), pltpu.SemaphoreType.DMA(())],
    )
    return pl.pallas_call(
        functools.partial(_dispatch_kernel, ntok=ntok, tm=tm, nz=nz),
        grid_spec=grid_spec,
        out_shape=jax.ShapeDtypeStruct((nrow * ROW_TILE, 128), F32),
        compiler_params=_cparams(("arbitrary",)),
        name="moe_dispatch",
    )(pos, zero_rows, h2t)


def _ffn_kernel(te_ref, nu_ref, xs_ref, wup_ref, wdn_ref, ys_ref, *, tm):
    del te_ref
    i = pl.program_id(0)

    @pl.when(i < nu_ref[0])
    def _():
        x = jnp.concatenate([xs_ref[pl.ds(s, tm, stride=ROW_TILE), :] for s in range(ROW_TILE)], axis=1)
        a = jnp.dot(x.astype(BF16), wup_ref[0], preferred_element_type=F32)
        g = a[:, :MOE_FF]
        hid = g * _sigmoid(g) * a[:, MOE_FF:]
        y = jnp.dot(hid.astype(BF16), wdn_ref[0], preferred_element_type=F32)
        for s in range(ROW_TILE):
            ys_ref[pl.ds(s, tm, stride=ROW_TILE), :] = y[:, s * 128:(s + 1) * 128]

    @pl.when(i >= nu_ref[0])
    def _():
        ys_ref[...] = jnp.zeros_like(ys_ref)


def _ffn_call(xs, tile_expert, n_used, w_up, w_down):
    tm = MOE_TM
    ntile = tile_expert.shape[0]
    grid_spec = pltpu.PrefetchScalarGridSpec(
        num_scalar_prefetch=2,
        grid=(ntile,),
        in_specs=[
            pl.BlockSpec((tm * ROW_TILE, 128), lambda i, te, nu: (i, 0)),
            pl.BlockSpec((1, D_MODEL, 2 * MOE_FF), lambda i, te, nu: (te[i], 0, 0)),
            pl.BlockSpec((1, MOE_FF, D_MODEL), lambda i, te, nu: (te[i], 0, 0)),
        ],
        out_specs=pl.BlockSpec((tm * ROW_TILE, 128), lambda i, te, nu: (i, 0)),
    )
    return pl.pallas_call(
        functools.partial(_ffn_kernel, tm=tm),
        grid_spec=grid_spec,
        out_shape=jax.ShapeDtypeStruct(xs.shape, F32),
        compiler_params=_cparams(("arbitrary",)),
        name="moe_ffn",
    )(tile_expert, n_used, xs, w_up, w_down)


def _final_kernel(pos_ref, x1_ref, route_ref, ada_ref, lng_ref, lnb_ref, ys_ref, o_ref, buf, sem,
                  *, ntok, tm, nstep):
    i = pl.program_id(0)
    slot = i % 2

    def fetch(step, to_slot):
        def issue(r, carry):
            for k in range(2):
                row = pos_ref[k * ntok + step * tm + r]
                src = ys_ref.at[pl.ds(pl.multiple_of(row * ROW_TILE, ROW_TILE), ROW_TILE)]
                dst = buf.at[to_slot, k, pl.ds(pl.multiple_of(r * ROW_TILE, ROW_TILE), ROW_TILE)]
                pltpu.make_async_copy(src, dst, sem.at[to_slot]).start(priority=k)
            return carry

        lax.fori_loop(0, tm, issue, 0)

    @pl.when(i == 0)
    def _():
        fetch(0, 0)

    @pl.when(i + 1 < nstep)
    def _():
        fetch(i + 1, 1 - slot)

    for k in range(2):
        pltpu.make_async_copy(ys_ref.at[pl.ds(0, tm * ROW_TILE)], buf.at[slot, k], sem.at[slot]).wait()
    ada = ada_ref[0]
    route = route_ref[0]
    w0, w1 = route[:, 2:3], route[:, 3:4]
    ffn = jnp.concatenate(
        [w0 * buf[slot, 0, pl.ds(s, tm, stride=ROW_TILE), :] + w1 * buf[slot, 1, pl.ds(s, tm, stride=ROW_TILE), :]
         for s in range(ROW_TILE)], axis=1)
    o_ref[0] = _ln(ALPHA * x1_ref[0] + (1.0 + ada[5:6]) * ffn) * lng_ref[...] + lnb_ref[...]


def _final_call(x1, ys, pos, route, ada8, ln_g, ln_b, tm):
    bsz, seq, _ = x1.shape
    nt = seq // tm
    nstep = bsz * nt
    grid_spec = pltpu.PrefetchScalarGridSpec(
        num_scalar_prefetch=1,
        grid=(nstep,),
        in_specs=[
            pl.BlockSpec((1, tm, D_MODEL), lambda i, p: (i // nt, i % nt, 0)),
            pl.BlockSpec((1, tm, 128), lambda i, p: (i // nt, i % nt, 0)),
            pl.BlockSpec((1, 8, D_MODEL), lambda i, p: (i // nt, 0, 0)),
            pl.BlockSpec((1, D_MODEL), lambda i, p: (0, 0)),
            pl.BlockSpec((1, D_MODEL), lambda i, p: (0, 0)),
            pl.BlockSpec(memory_space=pl.ANY),
        ],
        out_specs=pl.BlockSpec((1, tm, D_MODEL), lambda i, p: (i // nt, i % nt, 0)),
        scratch_shapes=[pltpu.VMEM((2, 2, tm * ROW_TILE, 128), F32), pltpu.SemaphoreType.DMA((2,))],
    )
    return pl.pallas_call(
        functools.partial(_final_kernel, ntok=bsz * seq, tm=tm, nstep=nstep),
        grid_spec=grid_spec,
        out_shape=jax.ShapeDtypeStruct((bsz, seq, D_MODEL), F32),
        compiler_params=_cparams(("arbitrary",)),
        name="final",
    )(pos, x1, route, ada8, ln_g, ln_b, ys)


def _split_w_in(w_in):
    sizes = (768, 768, 768, 512, 128, 128, 256, 256, 256, 256, N_BRANCH * D_MODEL)
    pts = [0]
    for s in sizes:
        pts.append(pts[-1] + s)
    return [w_in[:, pts[n]:pts[n + 1]] for n in range(len(sizes))]


def _layer_weights(p, l):
    a_q, a_k, a_v, b_q, b_k, b_v, c_q, c_k, c_v, d_u, w_gate = _split_w_in(p["w_in"][l])
    scale = HEAD_DIM ** -0.5
    def a_cols(g):
        gs = slice(g * A_WIDTH, (g + 1) * A_WIDTH)
        return [a_q[:, gs] * scale, a_k[:, gs], a_v[:, gs]]

    cols = [b_q * scale] + a_cols(0) + [b_k, b_v, c_q * scale, c_k, c_v, d_u] + a_cols(1) + a_cols(2)
    w_route = jnp.concatenate([p["w_route_g"][l], p["w_route_e"][l]], axis=1)
    b_route = jnp.concatenate([p["b_route_g"][l], p["b_route_e"][l]])
    npad = 128 - w_route.shape[1]
    return {
        "w_qkv": jnp.concatenate(cols, axis=1).astype(BF16),
        "w_gate": w_gate.astype(BF16),
        "b_gate": p["b_gate"][l].reshape(1, -1),
        "w_branch": p["w_branch"][l].astype(BF16),
        "w_out": p["w_out"][l].astype(BF16),
        "w_glu": p["w_glu"][l].astype(BF16),
        "b_glu": p["b_glu"][l].reshape(1, -1),
        "ln1_g": p["ln1_g"][l].reshape(1, -1),
        "ln1_b": p["ln1_b"][l].reshape(1, -1),
        "ln2_g": p["ln2_g"][l].reshape(1, -1),
        "ln2_b": p["ln2_b"][l].reshape(1, -1),
        "w_route": jnp.pad(w_route, ((0, 0), (0, npad))),
        "b_route": jnp.pad(b_route, (0, npad)).reshape(1, -1),
        "w_up": p["w_up"][l].astype(BF16),
        "w_down": p["w_down"][l].astype(BF16),
        "b_sink": p["b_sink"][l],
        "c_bias": _c_bias_table(p["c_rpb"][l]),
    }


def _layer(x, ada8, lw, ssm_prm):
    bsz, seq, _ = x.shape
    tm, tm_mm = 256, 512
    h, main, fold1, fold2 = _proj_call(x, ada8, lw["w_qkv"], tm_mm)
    oa, la = zip(*[_attn_a_call(f, g, seq) for g, f in enumerate((main, fold1, fold2))])
    yb = _attn_b_call(main, lw["b_sink"])
    yc = _attn_c_call(main, lw["c_bias"])
    s5 = _ssm_call(main, ssm_prm)
    x1, h2t, route = _merge_call(x, h, ada8, oa, la, yb, yc, s5, lw, tm_mm)
    ntok = bsz * seq
    eid = route.reshape(ntok, 128)[:, :2].astype(jnp.int32)
    pos, zero_rows, tile_expert, n_used = _moe_plan(eid, MOE_TM)
    xs = _dispatch_call(h2t, pos, zero_rows, 2 * ntok + N_EXPERTS * MOE_TM, tm)
    ys = _ffn_call(xs, tile_expert, n_used, lw["w_up"], lw["w_down"])
    return _final_call(x1, ys, pos, route, ada8, lw["ln2_g"], lw["ln2_b"], tm)


def kernel(x_prompt, x_sample, c_prompt, c_sample, w_ada, b_ada, w_in, b_gate, b_sink, c_rpb, lam_re, lam_im,
           log_dt, ssm_b_re, ssm_b_im, ssm_c_re, ssm_c_im, ssm_d, w_glu, b_glu, w_branch, w_out, ln1_g, ln1_b,
           ln2_g, ln2_b, w_route_g, b_route_g, w_route_e, b_route_e, w_up, w_down):
    p = dict(w_in=w_in, b_gate=b_gate, b_sink=b_sink, c_rpb=c_rpb, w_glu=w_glu, b_glu=b_glu, w_branch=w_branch,
             w_out=w_out, ln1_g=ln1_g, ln1_b=ln1_b, ln2_g=ln2_g, ln2_b=ln2_b, w_route_g=w_route_g,
             b_route_g=b_route_g, w_route_e=w_route_e, b_route_e=b_route_e, w_up=w_up, w_down=w_down)
    nbp, nbs = c_prompt.shape[0], c_sample.shape[0]
    c_all = jnp.concatenate([c_prompt, c_sample], axis=0)
    c_all = jnp.pad(c_all, ((0, -c_all.shape[0] % 8), (0, 0)))
    ada = _ada_call(c_all, w_ada, b_ada).reshape(DEPTH, c_all.shape[0], 6, D_MODEL)
    ada = jnp.pad(ada, ((0, 0), (0, 0), (0, 2), (0, 0)))
    m_mat, bend, cin, apow = _ssm_param_call(lam_re, lam_im, log_dt, ssm_b_re, ssm_b_im, ssm_c_re, ssm_c_im, ssm_d)
    xs = [x_prompt, x_sample]
    for l in range(DEPTH):
        lw = _layer_weights(p, l)
        prm = (m_mat[l * SSM_GROUPS:(l + 1) * SSM_GROUPS], bend[2 * l:2 * l + 2], cin[2 * l:2 * l + 2],
               apow[2 * l:2 * l + 2])
        xs = [_layer(xs[0], ada[l, :nbp], lw, prm), _layer(xs[1], ada[l, nbp:nbp + nbs], lw, prm)]
    return (xs[0], xs[1])
```

```python
import functools
import math

import jax
import jax.numpy as jnp
from jax import lax
from jax.experimental import pallas as pl
from jax.experimental.pallas import tpu as pltpu

F32 = jnp.float32
BF16 = jnp.bfloat16
HIGHEST = lax.Precision.HIGHEST

D_MODEL = 1024
DEPTH = 2
HEAD_DIM = 64
A_PAIRS = ((128, 1), (512, 4), (2048, 16))
A_GROUPS = 3
A_HEADS = 4
A_WIDTH = A_HEADS * HEAD_DIM
B_Q_HEADS = 8
B_KV_HEADS = 2
B_WINDOW = 128
B_WIDTH = B_Q_HEADS * HEAD_DIM
GRID_W = 64
C_HEADS = 4
C_WIN_R = 8
C_WIN_C = 16
C_WIDTH = C_HEADS * HEAD_DIM
D_WIDTH = 256
SSM_GROUP = 16
SSM_GROUPS = D_WIDTH // SSM_GROUP
SSM_STATE = 64
N_BRANCH = 4
MOE_GROUPS = 4
MOE_EPG = 8
N_EXPERTS = MOE_GROUPS * MOE_EPG
MOE_FF = 512
ALPHA = (2 * DEPTH) ** 0.25
LN_EPS = 1e-5
NEG = -1e30

QKV_COLS = 4096
MAIN_COLS = 2560
COL_BQ = 0
COL_A0 = 2
COL_BKV = 5
COL_CQ = 6
COL_DU = 9
A_QKV = 3 * A_WIDTH
ROW_TILE = 8
A_HALF = 64
ATTN_SQ = 128
ATTN_TQ = 1024
B_INTERLEAVE = 8
MERGE_COLS = 256
SSM_CHUNK = 64
SSM_CW = SSM_CHUNK * SSM_GROUP
SSM_SCAN_BATCH = 4
MOE_TM = 256
ISSUE_UNROLL = 8
VMEM_LIMIT = 56 * 1024 * 1024


def _cparams(sem):
    return pltpu.CompilerParams(dimension_semantics=sem, vmem_limit_bytes=VMEM_LIMIT)


def _sigmoid(x):
    return 1.0 / (1.0 + jnp.exp(-x))


def _ln(x):
    mu = jnp.mean(x, axis=-1, keepdims=True)
    xc = x - mu
    var = jnp.mean(xc * xc, axis=-1, keepdims=True)
    return xc * lax.rsqrt(var + LN_EPS)


def _dot_nt(a, b, **kw):
    return lax.dot_general(a, b, (((1,), (1,)), ((), ())), preferred_element_type=F32, **kw)


def _split_head_pairs(q):
    lane = lax.broadcasted_iota(jnp.int32, q.shape, 1)
    even = (lane % 128) < HEAD_DIM
    zero = jnp.zeros_like(q)
    return jnp.where(even, q, zero), jnp.where(even, zero, q)


def _alibi(n):
    return [2.0 ** (-8.0 * (i + 1) / n) for i in range(n)]


def _ada_kernel(c_ref, w_ref, b_ref, o_ref):
    c = c_ref[...]
    sc = c * _sigmoid(c)
    o_ref[0] = jnp.dot(sc, w_ref[0], precision=HIGHEST, preferred_element_type=F32) + b_ref[0]


def _ada_call(c_all, w_ada, b_ada):
    nb = c_all.shape[0]
    return pl.pallas_call(
        _ada_kernel,
        grid=(DEPTH, 6),
        in_specs=[
            pl.BlockSpec((nb, D_MODEL), lambda l, j: (0, 0)),
            pl.BlockSpec((1, D_MODEL, D_MODEL), lambda l, j: (l, 0, j)),
            pl.BlockSpec((1, 1, D_MODEL), lambda l, j: (l, 0, j)),
        ],
        out_specs=pl.BlockSpec((1, nb, D_MODEL), lambda l, j: (l, 0, j)),
        out_shape=jax.ShapeDtypeStruct((DEPTH, nb, 6 * D_MODEL), F32),
        compiler_params=_cparams(("arbitrary", "arbitrary")),
        name="ada",
    )(c_all, w_ada, b_ada.reshape(DEPTH, 1, 6 * D_MODEL))


def _proj_kernel(x_ref, ada_ref, w_ref, h_ref, main_ref, f1_ref, f2_ref, scr_ref, *, chunk):
    x = x_ref[0]
    ada = ada_ref[0]
    tm = x.shape[0]
    h = _ln(x) * (1.0 + ada[1:2]) + ada[0:1]
    hb = h.astype(BF16)
    h_ref[0] = hb
    for j in range(MAIN_COLS // chunk):
        sl = slice(j * chunk, (j + 1) * chunk)
        main_ref[0, :, sl] = jnp.dot(hb, w_ref[:, sl], preferred_element_type=F32).astype(BF16)
    for g, f_ref in ((1, f1_ref), (2, f2_ref)):
        dil = A_PAIRS[g][1]
        c0 = MAIN_COLS + (g - 1) * A_QKV
        res = jnp.dot(hb, w_ref[:, c0:c0 + A_QKV], preferred_element_type=F32)
        for c in range(A_QKV // 128):
            scr_ref[c] = res[:, c * 128:(c + 1) * 128]
        for r in range(dil):
            for c in range(A_QKV // 128):
                lo = r * A_QKV + c * 128
                f_ref[0, :, lo:lo + 128] = scr_ref[c, pl.ds(r, tm // dil, stride=dil), :].astype(BF16)


def _proj_call(x, ada8, w_qkv, tm):
    bsz, seq, _ = x.shape
    d1, d2 = A_PAIRS[1][1], A_PAIRS[2][1]
    return pl.pallas_call(
        functools.partial(_proj_kernel, chunk=512),
        grid=(bsz, seq // tm),
        in_specs=[
            pl.BlockSpec((1, tm, D_MODEL), lambda b, i: (b, i, 0)),
            pl.BlockSpec((1, 8, D_MODEL), lambda b, i: (b, 0, 0)),
            pl.BlockSpec((D_MODEL, QKV_COLS), lambda b, i: (0, 0), pipeline_mode=pl.Buffered(1)),
        ],
        out_specs=[
            pl.BlockSpec((1, tm, D_MODEL), lambda b, i: (b, i, 0)),
            pl.BlockSpec((1, tm, MAIN_COLS), lambda b, i: (b, i, 0)),
            pl.BlockSpec((1, tm // d1, d1 * A_QKV), lambda b, i: (b, i, 0)),
            pl.BlockSpec((1, tm // d2, d2 * A_QKV), lambda b, i: (b, i, 0)),
        ],
        out_shape=[
            jax.ShapeDtypeStruct((bsz, seq, D_MODEL), BF16),
            jax.ShapeDtypeStruct((bsz, seq, MAIN_COLS), BF16),
            jax.ShapeDtypeStruct((bsz, seq // d1, d1 * A_QKV), BF16),
            jax.ShapeDtypeStruct((bsz, seq // d2, d2 * A_QKV), BF16),
        ],
        scratch_shapes=[pltpu.VMEM((A_QKV // 128, tm, 128), F32)],
        compiler_params=_cparams(("parallel", "parallel")),
        name="proj",
    )(x, ada8, w_qkv)


def _attn_a_kernel(q_ref, kc_ref, kp_ref, kn_ref, vc_ref, vp_ref, vn_ref, o_ref, lse_ref,
                   *, tq, sub, slopes, dil):
    i = pl.program_id(2)
    q = q_ref[0]
    k = jnp.concatenate([kp_ref[0], kc_ref[0], kn_ref[0]], axis=0)
    v = jnp.concatenate([vp_ref[0], vc_ref[0], vn_ref[0]], axis=0)
    sq = ATTN_SQ
    kw = sq + 2 * A_HALF
    row = lax.broadcasted_iota(jnp.int32, (sq, kw), 0)
    col = lax.broadcasted_iota(jnp.int32, (sq, kw), 1)
    rel = col - A_HALF - row
    dist = jnp.abs(rel).astype(F32) * float(dil)
    even_q, odd_q = _split_head_pairs(q)
    even_lanes = (lax.broadcasted_iota(jnp.int32, (sq, 128), 1) < HEAD_DIM)
    for j in range(tq // sq):
        kpos = i * tq + j * sq - A_HALF + col
        valid = (jnp.abs(rel) <= A_HALF) & (kpos >= 0) & (kpos < sub)
        rows = slice(j * sq, (j + 1) * sq)
        kj = k[j * sq:j * sq + kw]
        vj = v[j * sq:j * sq + kw]
        hs = range(A_HEADS)
        pss = [slice((h // 2) * 128, (h // 2 + 1) * 128) for h in hs]
        ss = [_dot_nt((even_q, odd_q)[h % 2][rows, ps], kj[:, ps]) for h, ps in zip(hs, pss)]
        ss = [jnp.where(valid, s - slopes[h] * dist, NEG) for s, h in zip(ss, hs)]
        ms = [jnp.max(s, axis=-1, keepdims=True) for s in ss]
        pp = [jnp.exp(s - m) for s, m in zip(ss, ms)]
        dens = [jnp.sum(p, axis=-1, keepdims=True) for p in pp]
        outs = [jnp.dot(p.astype(BF16), vj[:, ps], preferred_element_type=F32) / den
                for p, ps, den in zip(pp, pss, dens)]
        lses = [m + jnp.log(den) for m, den in zip(ms, dens)]
        for n in range(0, A_HEADS, 2):
            o_ref[0, rows, pss[n]] = jnp.where(even_lanes, outs[n], outs[n + 1]).astype(BF16)
            lse_ref[0, rows, pss[n]] = jnp.where(even_lanes, lses[n], lses[n + 1])


def _attn_a_call(folded, g, seq):
    bsz = folded.shape[0]
    _, dil = A_PAIRS[g]
    sub = seq // dil
    tq = min(ATTN_TQ, sub)
    nq = sub // tq
    hb = tq // A_HALF
    nhb = sub // A_HALF
    nblk = folded.shape[2] // dil // 256
    cq = COL_A0 if g == 0 else 0
    slopes = _alibi(A_GROUPS * A_HEADS)[g * A_HEADS:(g + 1) * A_HEADS]

    def cur(c):
        return pl.BlockSpec((1, tq, 256), lambda b, r, i: (b, i, r * nblk + c))

    def prev(c):
        return pl.BlockSpec((1, A_HALF, 256), lambda b, r, i: (b, jnp.maximum(i * hb - 1, 0), r * nblk + c))

    def nxt(c):
        return pl.BlockSpec((1, A_HALF, 256), lambda b, r, i: (b, jnp.minimum((i + 1) * hb, nhb - 1), r * nblk + c))

    return pl.pallas_call(
        functools.partial(_attn_a_kernel, tq=tq, sub=sub, slopes=slopes, dil=dil),
        grid=(bsz, dil, nq),
        in_specs=[cur(cq), cur(cq + 1), prev(cq + 1), nxt(cq + 1), cur(cq + 2), prev(cq + 2), nxt(cq + 2)],
        out_specs=[
            pl.BlockSpec((1, tq, A_WIDTH), lambda b, r, i: (b, i, r)),
            pl.BlockSpec((1, tq, A_WIDTH), lambda b, r, i: (b, i, r)),
        ],
        out_shape=[
            jax.ShapeDtypeStruct((bsz, sub, dil * A_WIDTH), BF16),
            jax.ShapeDtypeStruct((bsz, sub, dil * A_WIDTH), F32),
        ],
        compiler_params=_cparams(("parallel", "parallel", "parallel")),
        name=f"attn_a{g}",
    )(folded, folded, folded, folded, folded, folded, folded)


def _attn_b_kernel(sink_ref, q_ref, kvc_ref, kvp_ref, kvn_ref, o_ref, *, tq, seq, slopes):
    i = pl.program_id(1)
    q = q_ref[0]
    kv = jnp.concatenate([kvp_ref[0], kvc_ref[0], kvn_ref[0]], axis=0)
    sq = ATTN_SQ
    kw = sq + 2 * B_WINDOW
    row = lax.broadcasted_iota(jnp.int32, (sq, kw), 0)
    col = lax.broadcasted_iota(jnp.int32, (sq, kw), 1)
    rel = col - B_WINDOW - row
    dist = jnp.abs(rel).astype(F32)
    rep = B_Q_HEADS // B_KV_HEADS
    even_q, odd_q = _split_head_pairs(q)
    even_lanes = (lax.broadcasted_iota(jnp.int32, (sq, 128), 1) < HEAD_DIM)

    def doubled(c):
        piece = kv[:, c * HEAD_DIM:(c + 1) * HEAD_DIM]
        return jnp.concatenate([piece, piece], axis=1)

    k2 = [doubled(g) for g in range(B_KV_HEADS)]
    v2 = [doubled(B_KV_HEADS + g) for g in range(B_KV_HEADS)]
    for j in range(tq // sq):
        kpos = i * tq + j * sq - B_WINDOW + col
        valid = (jnp.abs(rel) <= B_WINDOW) & (kpos >= 0) & (kpos < seq)
        rows = slice(j * sq, (j + 1) * sq)
        for h0 in range(0, B_Q_HEADS, B_INTERLEAVE):
            hs = range(h0, h0 + B_INTERLEAVE)
            qms = [(even_q, odd_q)[h % 2][rows, (h // 2) * 128:(h // 2 + 1) * 128] for h in hs]
            kjs = [k2[h // rep][j * sq:j * sq + kw] for h in hs]
            vjs = [v2[h // rep][j * sq:j * sq + kw] for h in hs]
            sks = [sink_ref[h] for h in hs]
            ss = [_dot_nt(qm, kj) for qm, kj in zip(qms, kjs)]
            ss = [jnp.where(valid, s - slopes[h] * dist, NEG) for s, h in zip(ss, hs)]
            ms = [jnp.maximum(jnp.max(s, axis=-1, keepdims=True), sk) for s, sk in zip(ss, sks)]
            pp = [jnp.exp(s - m) for s, m in zip(ss, ms)]
            dens = [jnp.sum(p, axis=-1, keepdims=True) + jnp.exp(sk - m) for p, sk, m in zip(pp, sks, ms)]
            outs = [jnp.dot(p.astype(BF16), vj, preferred_element_type=F32) / den
                    for p, vj, den in zip(pp, vjs, dens)]
            for n in range(0, B_INTERLEAVE, 2):
                ps = slice((h0 + n) // 2 * 128, ((h0 + n) // 2 + 1) * 128)
                o_ref[0, rows, ps] = jnp.where(even_lanes, outs[n], outs[n + 1]).astype(BF16)


def _attn_b_call(qkv, sink):
    bsz, seq, _ = qkv.shape
    tq = min(ATTN_TQ, seq)
    hb = tq // B_WINDOW
    nhb = seq // B_WINDOW
    return pl.pallas_call(
        functools.partial(_attn_b_kernel, tq=tq, seq=seq, slopes=_alibi(B_Q_HEADS)),
        grid=(bsz, seq // tq),
        in_specs=[
            pl.BlockSpec(memory_space=pltpu.SMEM),
            pl.BlockSpec((1, tq, B_WIDTH), lambda b, i: (b, i, COL_BQ)),
            pl.BlockSpec((1, tq, 256), lambda b, i: (b, i, COL_BKV)),
            pl.BlockSpec((1, B_WINDOW, 256), lambda b, i: (b, jnp.maximum(i * hb - 1, 0), COL_BKV)),
            pl.BlockSpec((1, B_WINDOW, 256), lambda b, i: (b, jnp.minimum((i + 1) * hb, nhb - 1), COL_BKV)),
        ],
        out_specs=pl.BlockSpec((1, tq, B_WIDTH), lambda b, i: (b, i, 0)),
        out_shape=jax.ShapeDtypeStruct((bsz, seq, B_WIDTH), BF16),
        compiler_params=_cparams(("parallel", "parallel")),
        name="attn_b",
    )(sink, qkv, qkv, qkv, qkv)


C_ROWS_PER_STEP = 16
C_INTERLEAVE_ROWS = 4


def _attn_c_kernel(q_ref, k_ref, v_ref, bias_ref, o_ref, *, rows):
    i = pl.program_id(1)
    kwin = C_WIN_R * GRID_W
    even_q, odd_q = _split_head_pairs(q_ref[0])
    even_lanes = (lax.broadcasted_iota(jnp.int32, (GRID_W, 128), 1) < HEAD_DIM)
    for jr0 in range(0, C_ROWS_PER_STEP, C_INTERLEAVE_ROWS):
        units = []
        for jr in range(jr0, jr0 + C_INTERLEAVE_ROWS):
            r = i * C_ROWS_PER_STEP + jr
            rs = jnp.clip(r - C_WIN_R // 2, 0, rows - C_WIN_R)
            start = pl.multiple_of(rs * GRID_W, GRID_W)
            kw = k_ref[0, pl.ds(start, kwin), :]
            vw = v_ref[0, pl.ds(start, kwin), :]
            for h in range(C_HEADS):
                units.append((slice(jr * GRID_W, (jr + 1) * GRID_W), h, r - rs, kw, vw))
        pss = [slice((h // 2) * 128, (h // 2 + 1) * 128) for _, h, _, _, _ in units]
        ss = [_dot_nt((even_q, odd_q)[h % 2][qrows, ps], kw[:, ps]) + bias_ref[h, delta]
              for (qrows, h, delta, kw, _), ps in zip(units, pss)]
        ms = [jnp.max(s, axis=-1, keepdims=True) for s in ss]
        pp = [jnp.exp(s - m) for s, m in zip(ss, ms)]
        dens = [jnp.sum(p, axis=-1, keepdims=True) for p in pp]
        outs = [jnp.dot(p.astype(BF16), vw[:, ps], preferred_element_type=F32) / den
                for p, (_, _, _, _, vw), ps, den in zip(pp, units, pss, dens)]
        for n in range(0, len(units), 2):
            o_ref[0, units[n][0], pss[n]] = jnp.where(even_lanes, outs[n], outs[n + 1]).astype(BF16)


def _c_bias_table(rpb):
    qc = jnp.arange(GRID_W)[:, None]
    kc = jnp.arange(GRID_W)[None, :]
    col_off = jnp.clip(kc - qc + (C_WIN_C - 1), 0, 2 * C_WIN_C - 2)
    sel = (col_off[None] == jnp.arange(2 * C_WIN_C - 1)[:, None, None]).astype(F32)
    band = jnp.einsum("hrc,cqk->hrqk", rpb.astype(F32), sel, precision=HIGHEST)
    cs = jnp.clip(qc - C_WIN_C // 2, 0, GRID_W - C_WIN_C)
    valid = (kc >= cs) & (kc < cs + C_WIN_C)
    band = jnp.where(valid[None, None], band, NEG)
    tabs = [band[:, C_WIN_R - 1 - d:2 * C_WIN_R - 1 - d].transpose(0, 2, 1, 3) for d in range(C_WIN_R)]
    return jnp.stack(tabs, axis=1).reshape(C_HEADS, C_WIN_R, GRID_W, C_WIN_R * GRID_W)


def _attn_c_call(qkv, bias_tab):
    bsz, seq, _ = qkv.shape
    rows = seq // GRID_W
    tq = C_ROWS_PER_STEP * GRID_W
    return pl.pallas_call(
        functools.partial(_attn_c_kernel, rows=rows),
        grid=(bsz, seq // tq),
        in_specs=[
            pl.BlockSpec((1, tq, C_WIDTH), lambda b, i: (b, i, COL_CQ)),
            pl.BlockSpec((1, seq, C_WIDTH), lambda b, i: (b, 0, COL_CQ + 1)),
            pl.BlockSpec((1, seq, C_WIDTH), lambda b, i: (b, 0, COL_CQ + 2)),
            pl.BlockSpec(bias_tab.shape, lambda b, i: (0, 0, 0, 0)),
        ],
        out_specs=pl.BlockSpec((1, tq, C_WIDTH), lambda b, i: (b, i, 0)),
        out_shape=jax.ShapeDtypeStruct((bsz, seq, C_WIDTH), BF16),
        compiler_params=_cparams(("parallel", "arbitrary")),
        name="attn_c",
    )(qkv, qkv, qkv, bias_tab)


def _cmul(ar, ai, br, bi):
    return ar * br - ai * bi, ar * bi + ai * br


def _ssm_param_kernel(lre_ref, lim_ref, ldt_ref, cre_ref, cim_ref, bre_ref, bim_ref, d_ref,
                      m_ref, bend_ref, cin_ref, apow_ref):
    tc, hh, pp = SSM_CHUNK, SSM_GROUP, SSM_STATE
    it = lax.broadcasted_iota(jnp.int32, (tc, pp), 0).astype(F32)
    lane = lax.broadcasted_iota(jnp.int32, (hh, 128), 1)
    subl = lax.broadcasted_iota(jnp.int32, (hh, 128), 0)
    tabs = []
    for d in range(2):
        lr, li = lre_ref[d, 0, 0], lim_ref[d, 0, 0]
        dt = jnp.exp(ldt_ref[d, 0, 0])

        def powers(kv):
            mag = jnp.exp(kv * (lr * dt))
            ang = kv * (li * dt)
            return mag * jnp.cos(ang), mag * jnp.sin(ang)

        one = jnp.ones((1, pp), F32)
        lbr, lbi = powers(one)
        den = lr * lr + li * li
        nr, ni = lbr - 1.0, lbi
        fr = (nr * lr + ni * li) / den
        fi = (ni * lr - nr * li) / den
        apr, api = powers(one * float(tc))
        apow_ref[d, 0] = jnp.concatenate([apr, api], axis=1)

        cr, ci = cre_ref[d, 0], cim_ref[d, 0]
        br, bi = bre_ref[d, 0], bim_ref[d, 0]

        wr, wi = powers(it + 1.0 if d == 0 else float(tc) - it)
        cwr, cwi = _cmul(cr[None], ci[None], wr[:, None, :], wi[:, None, :])
        cin = jnp.concatenate([cwr, -cwi], axis=2).reshape(tc * hh, 2 * pp)
        cin_ref[d, 0] = cin.astype(BF16)

        wr, wi = powers(float(tc - 1) - it if d == 0 else it)
        wfr, wfi = _cmul(wr, wi, fr, fi)
        er, ei = _cmul(br[None], bi[None], wfr[:, None, :], wfi[:, None, :])
        bend_ref[d, 0] = jnp.concatenate([er, ei], axis=2).reshape(tc * hh, 2 * pp).astype(BF16)

        wr, wi = powers(it if d == 0 else float(tc - 1) - it)
        wfr, wfi = _cmul(wr, wi, fr, fi)
        gr, gi = _cmul(cr[None], ci[None], wfr[:, None, :], wfi[:, None, :])
        gr = gr.reshape(tc * hh, pp)
        gi = gi.reshape(tc * hh, pp)
        tabs.append(_dot_nt(br, gr, precision=HIGHEST) - _dot_nt(bi, gi, precision=HIGHEST))

    ktf, ktb = tabs
    last = pltpu.roll(ktb[:, SSM_CW - 128:], hh, axis=1)
    diag = jnp.where(lane == subl, d_ref[0], 0.0)
    first = ktf[:, :128] + jnp.where(lane < hh, last + diag, 0.0)
    ktf = jnp.concatenate([first, ktf[:, 128:]], axis=1)
    lane_w = lax.broadcasted_iota(jnp.int32, (hh, SSM_CW), 1)
    ktb = jnp.where(lane_w < SSM_CW - hh, ktb, 0.0)
    zeros = jnp.zeros((hh, SSM_CW), F32)
    pf = jnp.concatenate([ktf, zeros], axis=1)
    pb = jnp.concatenate([ktb, zeros], axis=1)
    for j in range(tc):
        blk = pf if j == 0 else pltpu.roll(pf, j * hh, axis=1)
        sh = (tc - 1 - j) * hh
        blb = pb if sh == 0 else pltpu.roll(pb, 2 * SSM_CW - sh, axis=1)
        m_ref[0, j * hh:(j + 1) * hh, :] = (blk[:, :SSM_CW] + blb[:, :SSM_CW]).astype(BF16)


def _ssm_param_call(lam_re, lam_im, log_dt, b_re, b_im, c_re, c_im, d_skip):
    gg, pp, hh = SSM_GROUPS, SSM_STATE, SSM_GROUP
    nl = lam_re.shape[0]
    lre = lam_re.reshape(nl * 2, gg, 1, 1, pp)
    lim = lam_im.reshape(nl * 2, gg, 1, 1, pp)
    ldt = jnp.broadcast_to(log_dt.reshape(nl * 2, gg, 1, 1, 1), (nl * 2, gg, 1, 1, pp))
    cre = c_re.reshape(nl * 2, gg, hh, pp)
    cim = c_im.reshape(nl * 2, gg, hh, pp)
    bre = jnp.swapaxes(b_re, -1, -2).reshape(nl * 2, gg, hh, pp)
    bim = jnp.swapaxes(b_im, -1, -2).reshape(nl * 2, gg, hh, pp)
    dpad = jnp.pad(d_skip.reshape(nl * gg, 1, hh), ((0, 0), (0, 0), (0, 128 - hh)))

    def vec(l, g):
        return (l, g, 0, 0, 0)

    def mat(l, g):
        return (l, g, 0, 0)

    return pl.pallas_call(
        _ssm_param_kernel,
        grid=(nl, gg),
        in_specs=[
            pl.BlockSpec((2, 1, 1, 1, pp), vec), pl.BlockSpec((2, 1, 1, 1, pp), vec),
            pl.BlockSpec((2, 1, 1, 1, pp), vec),
            pl.BlockSpec((2, 1, hh, pp), mat), pl.BlockSpec((2, 1, hh, pp), mat),
            pl.BlockSpec((2, 1, hh, pp), mat), pl.BlockSpec((2, 1, hh, pp), mat),
            pl.BlockSpec((1, 1, 128), lambda l, g: (l * gg + g, 0, 0)),
        ],
        out_specs=[
            pl.BlockSpec((1, SSM_CW, SSM_CW), lambda l, g: (l * gg + g, 0, 0)),
            pl.BlockSpec((2, 1, SSM_CW, 2 * pp), mat),
            pl.BlockSpec((2, 1, SSM_CW, 2 * pp), mat),
            pl.BlockSpec((2, 1, 1, 2 * pp), mat),
        ],
        out_shape=[
            jax.ShapeDtypeStruct((nl * gg, SSM_CW, SSM_CW), BF16),
            jax.ShapeDtypeStruct((nl * 2, gg, SSM_CW, 2 * pp), BF16),
            jax.ShapeDtypeStruct((nl * 2, gg, SSM_CW, 2 * pp), BF16),
            jax.ShapeDtypeStruct((nl * 2, gg, 1, 2 * pp), F32),
        ],
        compiler_params=_cparams(("arbitrary", "arbitrary")),
        name="ssm_param",
    )(lre, lim, ldt, cre, cim, bre, bim, dpad)


def _ssm_state_kernel(u_ref, bend_ref, s_ref):
    for g in range(SSM_GROUPS):
        u = u_ref[0, g]
        for d in range(2):
            s_ref[0, d, :, g * 128:(g + 1) * 128] = jnp.dot(u, bend_ref[d, g], preferred_element_type=F32)


def _ssm_scan_kernel(s_ref, apow_ref, x_ref, *, nc, bb):
    width = SSM_GROUPS * 128
    lane = lax.broadcasted_iota(jnp.int32, (1, width), 1)
    first_half = (lane % 128) < SSM_STATE
    a_re, a_im = [], []
    for d in range(2):
        a = apow_ref[d]
        a_re.append(jnp.where(first_half, a, pltpu.roll(a, SSM_STATE, axis=1)))
        a_im.append(jnp.where(first_half, -pltpu.roll(a, width - SSM_STATE, axis=1), a))

    def body(c, xs):
        new = []
        for n, x in enumerate(xs):
            b, d = divmod(n, 2)
            cc = c if d == 0 else nc - 1 - c
            x_ref[b, d, pl.ds(cc, 1), :] = x
            x_sw_lo = pltpu.roll(x, SSM_STATE, axis=1)
            x_sw_hi = pltpu.roll(x, width - SSM_STATE, axis=1)
            x_sw = jnp.where(first_half, x_sw_hi, x_sw_lo)
            new.append(a_re[d] * x + a_im[d] * x_sw + s_ref[b, d, pl.ds(cc, 1), :])
        return tuple(new)

    lax.fori_loop(0, nc, body, tuple(jnp.zeros((1, width), F32) for _ in range(2 * bb)))


def _ssm_out_kernel(u_ref, m_ref, cin_ref, x_ref, y_ref):
    bsz, _, nc, cw = u_ref.shape
    y = jnp.dot(u_ref[:, 0].reshape(bsz * nc, cw), m_ref[0], preferred_element_type=F32)
    for d in range(2):
        y = y + _dot_nt(x_ref[:, d].reshape(bsz * nc, 128).astype(BF16), cin_ref[d, 0])
    y_ref[:, 0] = y.reshape(bsz, nc, cw)


def _ssm_call(qkv, prm):
    m_mat, bend, cin, apow = prm
    bsz, seq, _ = qkv.shape
    gg, tc, hh = SSM_GROUPS, SSM_CHUNK, SSM_GROUP
    nc = seq // tc
    u = qkv[:, :, COL_DU * 256:(COL_DU + 1) * 256]
    u = u.reshape(bsz, nc, tc, gg, hh).transpose(0, 3, 1, 2, 4).reshape(bsz, gg, nc, SSM_CW)
    width = gg * 128
    bb = math.gcd(bsz, SSM_SCAN_BATCH)
    s = pl.pallas_call(
        _ssm_state_kernel,
        grid=(bsz,),
        in_specs=[
            pl.BlockSpec((1, gg, nc, SSM_CW), lambda b: (b, 0, 0, 0)),
            pl.BlockSpec((2, gg, SSM_CW, 128), lambda b: (0, 0, 0, 0)),
        ],
        out_specs=pl.BlockSpec((1, 2, nc, width), lambda b: (b, 0, 0, 0)),
        out_shape=jax.ShapeDtypeStruct((bsz, 2, nc, width), F32),
        compiler_params=_cparams(("parallel",)),
        name="ssm_state",
    )(u, bend)
    xprev = pl.pallas_call(
        functools.partial(_ssm_scan_kernel, nc=nc, bb=bb),
        grid=(bsz // bb,),
        in_specs=[
            pl.BlockSpec((bb, 2, nc, width), lambda b: (b, 0, 0, 0)),
            pl.BlockSpec((2, 1, width), lambda b: (0, 0, 0)),
        ],
        out_specs=pl.BlockSpec((bb, 2, nc, width), lambda b: (b, 0, 0, 0)),
        out_shape=jax.ShapeDtypeStruct((bsz, 2, nc, width), F32),
        compiler_params=_cparams(("parallel",)),
        name="ssm_scan",
    )(s, apow.reshape(2, 1, width))
    y = pl.pallas_call(
        _ssm_out_kernel,
        grid=(gg,),
        in_specs=[
            pl.BlockSpec((bsz, 1, nc, SSM_CW), lambda g: (0, g, 0, 0)),
            pl.BlockSpec((1, SSM_CW, SSM_CW), lambda g: (g, 0, 0)),
            pl.BlockSpec((2, 1, SSM_CW, 128), lambda g: (0, g, 0, 0)),
            pl.BlockSpec((bsz, 2, nc, 128), lambda g: (0, 0, 0, g)),
        ],
        out_specs=pl.BlockSpec((bsz, 1, nc, SSM_CW), lambda g: (0, g, 0, 0)),
        out_shape=jax.ShapeDtypeStruct((bsz, gg, nc, SSM_CW), F32),
        compiler_params=_cparams(("arbitrary",)),
        name="ssm_out",
    )(u, m_mat, cin, xprev)
    return y.reshape(bsz, gg, nc, tc, hh).transpose(0, 2, 3, 1, 4).reshape(bsz, seq, D_WIDTH)


def _merge_kernel(x_ref, h_ref, ada_ref,
                  oa0_ref, oa1_ref, oa2_ref, la0_ref, la1_ref, la2_ref, yb_ref, yc_ref, s5_ref,
                  wg_ref, bg_ref, wbr_ref, wout_ref, wglu_ref, bglu_ref, lng_ref, lnb_ref,
                  wr_ref, br_ref,
                  x1_ref, h2_ref, route_ref, o1_scr, l1_scr, o2_scr, l2_scr):
    ada = ada_ref[0]
    hb = h_ref[0]
    tm = hb.shape[0]
    for g, (o_ref, l_ref, o_scr, l_scr) in ((1, (oa1_ref, la1_ref, o1_scr, l1_scr)),
                                            (2, (oa2_ref, la2_ref, o2_scr, l2_scr))):
        dil = A_PAIRS[g][1]
        for r in range(dil):
            for c in range(A_WIDTH // 128):
                cs = slice(r * A_WIDTH + c * 128, r * A_WIDTH + (c + 1) * 128)
                o_scr[c, pl.ds(r, tm // dil, stride=dil), :] = o_ref[0, :, cs].astype(F32)
                l_scr[c, pl.ds(r, tm // dil, stride=dil), :] = l_ref[0, :, cs]

    def unfolded(scr):
        return jnp.concatenate([scr[c] for c in range(A_WIDTH // 128)], axis=1)

    l0, l1, l2 = la0_ref[0], unfolded(l1_scr), unfolded(l2_scr)
    lm = jnp.maximum(jnp.maximum(l0, l1), l2)
    e0, e1, e2 = jnp.exp(l0 - lm), jnp.exp(l1 - lm), jnp.exp(l2 - lm)
    ya = (e0 * oa0_ref[0].astype(F32) + e1 * unfolded(o1_scr) + e2 * unfolded(o2_scr)) / (e0 + e1 + e2)
    s5 = s5_ref[0]
    gel = 0.5 * s5 * (1.0 + jnp.tanh(math.sqrt(2.0 / math.pi) * (s5 + 0.044715 * (s5 * s5 * s5))))
    z = jnp.dot(gel.astype(BF16), wglu_ref[...], preferred_element_type=F32) + bglu_ref[...]
    yd = z[:, :D_WIDTH] * _sigmoid(z[:, D_WIDTH:])
    branches = (ya.astype(BF16), yb_ref[0], yc_ref[0], yd.astype(BF16))
    offs = (0, A_WIDTH, A_WIDTH + B_WIDTH, A_WIDTH + B_WIDTH + C_WIDTH, A_WIDTH + B_WIDTH + C_WIDTH + D_WIDTH)
    def chunk_dots(c0):
        css = [slice(n * D_MODEL + c0, n * D_MODEL + c0 + MERGE_COLS) for n in range(N_BRANCH)]
        pre = [jnp.dot(hb, wg_ref[:, cs], preferred_element_type=F32) + bg_ref[:, cs] for cs in css]
        brs = [jnp.dot(yb, wbr_ref[offs[n]:offs[n + 1], c0:c0 + MERGE_COLS], preferred_element_type=F32)
               for n, yb in enumerate(branches)]
        return pre, brs

    merged = []
    nxt = chunk_dots(0)
    for c0 in range(0, D_MODEL, MERGE_COLS):
        pre, brs = nxt
        if c0 + MERGE_COLS < D_MODEL:
            nxt = chunk_dots(c0 + MERGE_COLS)
        terms = [_sigmoid(g) * b for g, b in zip(pre, brs)]
        merged.append(((terms[0] + terms[1]) + terms[2]) + terms[3])
    merged = jnp.concatenate(merged, axis=1)
    mix = jnp.dot(merged.astype(BF16), wout_ref[...], preferred_element_type=F32)
    x1 = _ln(ALPHA * x_ref[0] + (1.0 + ada[2:3]) * mix) * lng_ref[...] + lnb_ref[...]
    x1_ref[0] = x1
    h2 = _ln(x1) * (1.0 + ada[4:5]) + ada[3:4]
    for s in range(ROW_TILE):
        h2_ref[pl.ds(s, tm, stride=ROW_TILE), :] = h2[:, s * 128:(s + 1) * 128]
    logit = jnp.dot(h2, wr_ref[...], precision=HIGHEST, preferred_element_type=F32) + br_ref[...]
    lane = lax.broadcasted_iota(jnp.int32, logit.shape, 1)
    ninf = -jnp.inf
    gmask = lane < MOE_GROUPS
    gl = jnp.where(gmask, logit, ninf)
    gmax = jnp.max(gl, axis=-1, keepdims=True)
    gsel = jnp.min(jnp.where(gl == gmax, lane, 1 << 20), axis=-1, keepdims=True)
    gw = 1.0 / jnp.sum(jnp.exp(gl - gmax), axis=-1, keepdims=True)
    lo = MOE_GROUPS + gsel * MOE_EPG
    emask = (lane >= lo) & (lane < lo + MOE_EPG)
    el = jnp.where(emask, logit, ninf)
    v1 = jnp.max(el, axis=-1, keepdims=True)
    i1 = jnp.min(jnp.where(el == v1, lane, 1 << 20), axis=-1, keepdims=True)
    el2 = jnp.where(lane == i1, ninf, el)
    v2 = jnp.max(el2, axis=-1, keepdims=True)
    i2 = jnp.min(jnp.where(el2 == v2, lane, 1 << 20), axis=-1, keepdims=True)
    t = jnp.exp(v2 - v1)
    w1 = gw / (1.0 + t)
    w2 = gw * t / (1.0 + t)
    route = jnp.where(lane == 0, (i1 - MOE_GROUPS).astype(F32),
                      jnp.where(lane == 1, (i2 - MOE_GROUPS).astype(F32),
                                jnp.where(lane == 2, w1, jnp.where(lane == 3, w2, 0.0))))
    route_ref[0] = route


def _merge_call(x, h, ada8, oa, la, yb, yc, s5, lw, tm):
    bsz, seq, _ = x.shape
    nt = seq // tm

    def tok(width):
        return pl.BlockSpec((1, tm, width), lambda b, i: (b, i, 0))

    def fold(g):
        dil = A_PAIRS[g][1]
        return pl.BlockSpec((1, tm // dil, dil * A_WIDTH), lambda b, i: (b, i, 0))

    def full(arr):
        return pl.BlockSpec(arr.shape, lambda b, i: (0,) * arr.ndim, pipeline_mode=pl.Buffered(1))

    weights = (lw["w_gate"], lw["b_gate"], lw["w_branch"], lw["w_out"], lw["w_glu"], lw["b_glu"],
               lw["ln1_g"], lw["ln1_b"], lw["w_route"], lw["b_route"])
    return pl.pallas_call(
        _merge_kernel,
        grid=(bsz, nt),
        in_specs=[tok(D_MODEL), tok(D_MODEL), pl.BlockSpec((1, 8, D_MODEL), lambda b, i: (b, 0, 0)),
                  tok(A_WIDTH), fold(1), fold(2), tok(A_WIDTH), fold(1), fold(2),
                  tok(B_WIDTH), tok(C_WIDTH), tok(D_WIDTH)] + [full(w) for w in weights],
        out_specs=[tok(D_MODEL), pl.BlockSpec((tm * ROW_TILE, 128), lambda b, i: (b * nt + i, 0)), tok(128)],
        out_shape=[
            jax.ShapeDtypeStruct((bsz, seq, D_MODEL), F32),
            jax.ShapeDtypeStruct((bsz * seq * ROW_TILE, 128), F32),
            jax.ShapeDtypeStruct((bsz, seq, 128), F32),
        ],
        scratch_shapes=[pltpu.VMEM((A_WIDTH // 128, tm, 128), F32)] * 4,
        compiler_params=_cparams(("parallel", "parallel")),
        name="merge",
    )(x, h, ada8, oa[0], oa[1], oa[2], la[0], la[1], la[2], yb, yc, s5, *weights)


def _moe_plan(eid, tm):
    ntok = eid.shape[0]
    npair = 2 * ntok
    blk = 256
    e_flat = jnp.concatenate([eid[:, 0], eid[:, 1]])
    experts = jnp.arange(N_EXPERTS, dtype=jnp.int32)
    oh = (e_flat[:, None] == experts[None, :]).astype(F32).reshape(npair // blk, blk, N_EXPERTS)
    tri = (jnp.arange(blk)[:, None] >= jnp.arange(blk)[None, :]).astype(F32)
    within = jnp.einsum("ij,bjk->bik", tri, oh, precision=HIGHEST)
    btot = within[:, -1, :]
    before = jnp.cumsum(btot, axis=0) - btot
    rank = (within - 1.0 + before[:, None, :]).reshape(npair, N_EXPERTS)
    oh = oh.reshape(npair, N_EXPERTS)
    counts = jnp.sum(btot, axis=0).astype(jnp.int32)
    pcounts = (counts + tm - 1) // tm * tm
    pend = jnp.cumsum(pcounts)
    pstart = pend - pcounts
    pos = jnp.sum(oh * (rank + pstart.astype(F32)[None, :]), axis=1).astype(jnp.int32)
    nrow = npair + N_EXPERTS * tm
    ntile = nrow // tm
    tile_expert = jnp.sum((pend[None, :] <= (jnp.arange(ntile, dtype=jnp.int32) * tm)[:, None]).astype(jnp.int32), axis=1)
    tile_expert = jnp.minimum(tile_expert, N_EXPERTS - 1)
    n_used = (pend[-1] // tm).astype(jnp.int32).reshape(1)
    padcnt = pcounts - counts
    padend = jnp.cumsum(padcnt)
    j = jnp.arange(N_EXPERTS * tm, dtype=jnp.int32)
    e_j = jnp.sum((padend[None, :] <= j[:, None]).astype(jnp.int32), axis=1)
    shift = pstart + counts - (padend - padcnt)
    row_pad = jnp.sum((e_j[:, None] == experts[None, :]).astype(jnp.int32) * shift[None, :], axis=1) + j
    zero_rows = jnp.where(e_j < N_EXPERTS, row_pad, pend[-1] + j - padend[-1])
    return pos, zero_rows, tile_expert, n_used


def _dispatch_kernel(pos_ref, zr_ref, h_ref, xs_ref, zbuf, sem, *, ntok, tm, nz):
    i = pl.program_id(0)

    def issue(r, carry):
        src = h_ref.at[pl.ds(pl.multiple_of(r * ROW_TILE, ROW_TILE), ROW_TILE)]
        for k in range(2):
            row = pos_ref[k * ntok + i * tm + r]
            dst = xs_ref.at[pl.ds(pl.multiple_of(row * ROW_TILE, ROW_TILE), ROW_TILE)]
            pltpu.make_async_copy(src, dst, sem).start(priority=k)
        return carry

    lax.fori_loop(0, tm, issue, 0, unroll=ISSUE_UNROLL)
    zbuf[...] = jnp.zeros_like(zbuf)

    def zero(r, carry):
        for k in range(2):
            row = zr_ref[i * nz + 2 * r + k]
            dst = xs_ref.at[pl.ds(pl.multiple_of(row * ROW_TILE, ROW_TILE), ROW_TILE)]
            pltpu.make_async_copy(zbuf, dst, sem).start(priority=k)
        return carry

    lax.fori_loop(0, nz // 2, zero, 0)
    for _ in range(2):
        pltpu.make_async_copy(h_ref, xs_ref.at[pl.ds(0, tm * ROW_TILE)], sem).wait()
    pltpu.make_async_copy(h_ref.at[pl.ds(0, nz * ROW_TILE)], xs_ref.at[pl.ds(0, nz * ROW_TILE)], sem).wait()


def _dispatch_call(h2t, pos, zero_rows, nrow, tm):
    ntok = pos.shape[0] // 2
    nstep = ntok // tm
    nz = zero_rows.shape[0] // nstep
    grid_spec = pltpu.PrefetchScalarGridSpec(
        num_scalar_prefetch=2,
        grid=(nstep,),
        in_specs=[pl.BlockSpec((tm * ROW_TILE, 128), lambda i, p, z: (i, 0))],
        out_specs=pl.BlockSpec(memory_space=pl.ANY),
        scratch_shapes=[pltpu.VMEM((ROW_TILE, 128), F32), pltpu.SemaphoreType.DMA(())],
    )
    return pl.pallas_call(
        functools.partial(_dispatch_kernel, ntok=ntok, tm=tm, nz=nz),
        grid_spec=grid_spec,
        out_shape=jax.ShapeDtypeStruct((nrow * ROW_TILE, 128), F32),
        compiler_params=_cparams(("arbitrary",)),
        name="moe_dispatch",
    )(pos, zero_rows, h2t)


def _ffn_kernel(te_ref, nu_ref, xs_ref, wup_ref, wdn_ref, ys_ref, *, tm):
    del te_ref
    i = pl.program_id(0)

    @pl.when(i < nu_ref[0])
    def _():
        x = jnp.concatenate([xs_ref[pl.ds(s, tm, stride=ROW_TILE), :] for s in range(ROW_TILE)], axis=1)
        a = jnp.dot(x.astype(BF16), wup_ref[0], preferred_element_type=F32)
        g = a[:, :MOE_FF]
        hid = g * _sigmoid(g) * a[:, MOE_FF:]
        y = jnp.dot(hid.astype(BF16), wdn_ref[0], preferred_element_type=F32)
        for s in range(ROW_TILE):
            ys_ref[pl.ds(s, tm, stride=ROW_TILE), :] = y[:, s * 128:(s + 1) * 128]

    @pl.when(i >= nu_ref[0])
    def _():
        ys_ref[...] = jnp.zeros_like(ys_ref)


def _ffn_call(xs, tile_expert, n_used, w_up, w_down):
    tm = MOE_TM
    ntile = tile_expert.shape[0]
    grid_spec = pltpu.PrefetchScalarGridSpec(
        num_scalar_prefetch=2,
        grid=(ntile,),
        in_specs=[
            pl.BlockSpec((tm * ROW_TILE, 128), lambda i, te, nu: (i, 0)),
            pl.BlockSpec((1, D_MODEL, 2 * MOE_FF), lambda i, te, nu: (te[i], 0, 0)),
            pl.BlockSpec((1, MOE_FF, D_MODEL), lambda i, te, nu: (te[i], 0, 0)),
        ],
        out_specs=pl.BlockSpec((tm * ROW_TILE, 128), lambda i, te, nu: (i, 0)),
    )
    return pl.pallas_call(
        functools.partial(_ffn_kernel, tm=tm),
        grid_spec=grid_spec,
        out_shape=jax.ShapeDtypeStruct(xs.shape, F32),
        compiler_params=_cparams(("arbitrary",)),
        name="moe_ffn",
    )(tile_expert, n_used, xs, w_up, w_down)


def _final_kernel(pos_ref, x1_ref, route_ref, ada_ref, lng_ref, lnb_ref, ys_ref, o_ref, buf, sem,
                  *, ntok, tm, nstep):
    i = pl.program_id(0)
    slot = i % 2

    def fetch(step, to_slot):
        def issue(r, carry):
            for k in range(2):
                row = pos_ref[k * ntok + step * tm + r]
                src = ys_ref.at[pl.ds(pl.multiple_of(row * ROW_TILE, ROW_TILE), ROW_TILE)]
                dst = buf.at[to_slot, k, pl.ds(pl.multiple_of(r * ROW_TILE, ROW_TILE), ROW_TILE)]
                pltpu.make_async_copy(src, dst, sem.at[to_slot]).start(priority=k)
            return carry

        lax.fori_loop(0, tm, issue, 0, unroll=ISSUE_UNROLL)

    @pl.when(i == 0)
    def _():
        fetch(0, 0)

    @pl.when(i + 1 < nstep)
    def _():
        fetch(i + 1, 1 - slot)

    for k in range(2):
        pltpu.make_async_copy(ys_ref.at[pl.ds(0, tm * ROW_TILE)], buf.at[slot, k], sem.at[slot]).wait()
    ada = ada_ref[0]
    route = route_ref[0]
    w0, w1 = route[:, 2:3], route[:, 3:4]
    ffn = jnp.concatenate(
        [w0 * buf[slot, 0, pl.ds(s, tm, stride=ROW_TILE), :] + w1 * buf[slot, 1, pl.ds(s, tm, stride=ROW_TILE), :]
         for s in range(ROW_TILE)], axis=1)
    o_ref[0] = _ln(ALPHA * x1_ref[0] + (1.0 + ada[5:6]) * ffn) * lng_ref[...] + lnb_ref[...]


def _final_call(x1, ys, pos, route, ada8, ln_g, ln_b, tm):
    bsz, seq, _ = x1.shape
    nt = seq // tm
    nstep = bsz * nt
    grid_spec = pltpu.PrefetchScalarGridSpec(
        num_scalar_prefetch=1,
        grid=(nstep,),
        in_specs=[
            pl.BlockSpec((1, tm, D_MODEL), lambda i, p: (i // nt, i % nt, 0)),
            pl.BlockSpec((1, tm, 128), lambda i, p: (i // nt, i % nt, 0)),
            pl.BlockSpec((1, 8, D_MODEL), lambda i, p: (i // nt, 0, 0)),
            pl.BlockSpec((1, D_MODEL), lambda i, p: (0, 0)),
            pl.BlockSpec((1, D_MODEL), lambda i, p: (0, 0)),
            pl.BlockSpec(memory_space=pl.ANY),
        ],
        out_specs=pl.BlockSpec((1, tm, D_MODEL), lambda i, p: (i // nt, i % nt, 0)),
        scratch_shapes=[pltpu.VMEM((2, 2, tm * ROW_TILE, 128), F32), pltpu.SemaphoreType.DMA((2,))],
    )
    return pl.pallas_call(
        functools.partial(_final_kernel, ntok=bsz * seq, tm=tm, nstep=nstep),
        grid_spec=grid_spec,
        out_shape=jax.ShapeDtypeStruct((bsz, seq, D_MODEL), F32),
        compiler_params=_cparams(("arbitrary",)),
        name="final",
    )(pos, x1, route, ada8, ln_g, ln_b, ys)


def _split_w_in(w_in):
    sizes = (768, 768, 768, 512, 128, 128, 256, 256, 256, 256, N_BRANCH * D_MODEL)
    pts = [0]
    for s in sizes:
        pts.append(pts[-1] + s)
    return [w_in[:, pts[n]:pts[n + 1]] for n in range(len(sizes))]


def _layer_weights(p, l):
    a_q, a_k, a_v, b_q, b_k, b_v, c_q, c_k, c_v, d_u, w_gate = _split_w_in(p["w_in"][l])
    scale = HEAD_DIM ** -0.5
    def a_cols(g):
        gs = slice(g * A_WIDTH, (g + 1) * A_WIDTH)
        return [a_q[:, gs] * scale, a_k[:, gs], a_v[:, gs]]

    cols = [b_q * scale] + a_cols(0) + [b_k, b_v, c_q * scale, c_k, c_v, d_u] + a_cols(1) + a_cols(2)
    w_route = jnp.concatenate([p["w_route_g"][l], p["w_route_e"][l]], axis=1)
    b_route = jnp.concatenate([p["b_route_g"][l], p["b_route_e"][l]])
    npad = 128 - w_route.shape[1]
    return {
        "w_qkv": jnp.concatenate(cols, axis=1).astype(BF16),
        "w_gate": w_gate.astype(BF16),
        "b_gate": p["b_gate"][l].reshape(1, -1),
        "w_branch": p["w_branch"][l].astype(BF16),
        "w_out": p["w_out"][l].astype(BF16),
        "w_glu": p["w_glu"][l].astype(BF16),
        "b_glu": p["b_glu"][l].reshape(1, -1),
        "ln1_g": p["ln1_g"][l].reshape(1, -1),
        "ln1_b": p["ln1_b"][l].reshape(1, -1),
        "ln2_g": p["ln2_g"][l].reshape(1, -1),
        "ln2_b": p["ln2_b"][l].reshape(1, -1),
        "w_route": jnp.pad(w_route, ((0, 0), (0, npad))),
        "b_route": jnp.pad(b_route, (0, npad)).reshape(1, -1),
        "w_up": p["w_up"][l].astype(BF16),
        "w_down": p["w_down"][l].astype(BF16),
        "b_sink": p["b_sink"][l],
        "c_bias": _c_bias_table(p["c_rpb"][l]),
    }


def _layer(x, ada8, lw, ssm_prm):
    bsz, seq, _ = x.shape
    tm, tm_mm = 256, 512
    h, main, fold1, fold2 = _proj_call(x, ada8, lw["w_qkv"], tm_mm)
    oa, la = zip(*[_attn_a_call(f, g, seq) for g, f in enumerate((main, fold1, fold2))])
    yb = _attn_b_call(main, lw["b_sink"])
    yc = _attn_c_call(main, lw["c_bias"])
    s5 = _ssm_call(main, ssm_prm)
    x1, h2t, route = _merge_call(x, h, ada8, oa, la, yb, yc, s5, lw, tm_mm)
    ntok = bsz * seq
    eid = route.reshape(ntok, 128)[:, :2].astype(jnp.int32)
    pos, zero_rows, tile_expert, n_used = _moe_plan(eid, MOE_TM)
    xs = _dispatch_call(h2t, pos, zero_rows, 2 * ntok + N_EXPERTS * MOE_TM, tm)
    ys = _ffn_call(xs, tile_expert, n_used, lw["w_up"], lw["w_down"])
    return _final_call(x1, ys, pos, route, ada8, lw["ln2_g"], lw["ln2_b"], tm)


def kernel(x_prompt, x_sample, c_prompt, c_sample, w_ada, b_ada, w_in, b_gate, b_sink, c_rpb, lam_re, lam_im,
           log_dt, ssm_b_re, ssm_b_im, ssm_c_re, ssm_c_im, ssm_d, w_glu, b_glu, w_branch, w_out, ln1_g, ln1_b,
           ln2_g, ln2_b, w_route_g, b_route_g, w_route_e, b_route_e, w_up, w_down):
    p = dict(w_in=w_in, b_gate=b_gate, b_sink=b_sink, c_rpb=c_rpb, w_glu=w_glu, b_glu=b_glu, w_branch=w_branch,
             w_out=w_out, ln1_g=ln1_g, ln1_b=ln1_b, ln2_g=ln2_g, ln2_b=ln2_b, w_route_g=w_route_g,
             b_route_g=b_route_g, w_route_e=w_route_e, b_route_e=b_route_e, w_up=w_up, w_down=w_down)
    nbp, nbs = c_prompt.shape[0], c_sample.shape[0]
    c_all = jnp.concatenate([c_prompt, c_sample], axis=0)
    c_all = jnp.pad(c_all, ((0, -c_all.shape[0] % 8), (0, 0)))
    ada = _ada_call(c_all, w_ada, b_ada).reshape(DEPTH, c_all.shape[0], 6, D_MODEL)
    ada = jnp.pad(ada, ((0, 0), (0, 0), (0, 2), (0, 0)))
    m_mat, bend, cin, apow = _ssm_param_call(lam_re, lam_im, log_dt, ssm_b_re, ssm_b_im, ssm_c_re, ssm_c_im, ssm_d)
    xs = [x_prompt, x_sample]
    for l in range(DEPTH):
        lw = _layer_weights(p, l)
        prm = (m_mat[l * SSM_GROUPS:(l + 1) * SSM_GROUPS], bend[2 * l:2 * l + 2], cin[2 * l:2 * l + 2],
               apow[2 * l:2 * l + 2])
        xs = [_layer(xs[0], ada[l, :nbp], lw, prm), _layer(xs[1], ada[l, nbp:nbp + nbs], lw, prm)]
    return (xs[0], xs[1])
```

```python
import functools
import math

import jax
import jax.numpy as jnp
from jax import lax
from jax.experimental import pallas as pl
from jax.experimental.pallas import tpu as pltpu

F32 = jnp.float32
BF16 = jnp.bfloat16
HIGHEST = lax.Precision.HIGHEST

D_MODEL = 1024
DEPTH = 2
HEAD_DIM = 64
A_PAIRS = ((128, 1), (512, 4), (2048, 16))
A_GROUPS = 3
A_HEADS = 4
A_WIDTH = A_HEADS * HEAD_DIM
B_Q_HEADS = 8
B_KV_HEADS = 2
B_WINDOW = 128
B_WIDTH = B_Q_HEADS * HEAD_DIM
GRID_W = 64
C_HEADS = 4
C_WIN_R = 8
C_WIN_C = 16
C_WIDTH = C_HEADS * HEAD_DIM
D_WIDTH = 256
SSM_GROUP = 16
SSM_GROUPS = D_WIDTH // SSM_GROUP
SSM_STATE = 64
N_BRANCH = 4
MOE_GROUPS = 4
MOE_EPG = 8
N_EXPERTS = MOE_GROUPS * MOE_EPG
MOE_FF = 512
ALPHA = (2 * DEPTH) ** 0.25
LN_EPS = 1e-5
NEG = -1e30

QKV_COLS = 4096
MAIN_COLS = 2560
COL_BQ = 0
COL_A0 = 2
COL_BKV = 5
COL_CQ = 6
COL_DU = 9
A_QKV = 3 * A_WIDTH
ROW_TILE = 8
A_HALF = 64
ATTN_SQ = 128
ATTN_TQ = 1024
B_INTERLEAVE = 8
MERGE_COLS = 256
SSM_CHUNK = 64
SSM_CW = SSM_CHUNK * SSM_GROUP
SSM_SCAN_BATCH = 4
MOE_TM = 256
ISSUE_UNROLL = 8
VMEM_LIMIT = 56 * 1024 * 1024


def _cparams(sem):
    return pltpu.CompilerParams(dimension_semantics=sem, vmem_limit_bytes=VMEM_LIMIT)


def _sigmoid(x):
    return 1.0 / (1.0 + jnp.exp(-x))


def _ln(x):
    mu = jnp.mean(x, axis=-1, keepdims=True)
    xc = x - mu
    var = jnp.mean(xc * xc, axis=-1, keepdims=True)
    return xc * lax.rsqrt(var + LN_EPS)


def _dot_nt(a, b, **kw):
    return lax.dot_general(a, b, (((1,), (1,)), ((), ())), preferred_element_type=F32, **kw)


def _split_head_pairs(q):
    lane = lax.broadcasted_iota(jnp.int32, q.shape, 1)
    even = (lane % 128) < HEAD_DIM
    zero = jnp.zeros_like(q)
    return jnp.where(even, q, zero), jnp.where(even, zero, q)


def _alibi(n):
    return [2.0 ** (-8.0 * (i + 1) / n) for i in range(n)]


def _ada_kernel(c_ref, w_ref, b_ref, o_ref):
    c = c_ref[...]
    sc = c * _sigmoid(c)
    o_ref[0] = jnp.dot(sc, w_ref[0], precision=HIGHEST, preferred_element_type=F32) + b_ref[0]


def _ada_call(c_all, w_ada, b_ada):
    nb = c_all.shape[0]
    return pl.pallas_call(
        _ada_kernel,
        grid=(DEPTH, 6),
        in_specs=[
            pl.BlockSpec((nb, D_MODEL), lambda l, j: (0, 0)),
            pl.BlockSpec((1, D_MODEL, D_MODEL), lambda l, j: (l, 0, j)),
            pl.BlockSpec((1, 1, D_MODEL), lambda l, j: (l, 0, j)),
        ],
        out_specs=pl.BlockSpec((1, nb, D_MODEL), lambda l, j: (l, 0, j)),
        out_shape=jax.ShapeDtypeStruct((DEPTH, nb, 6 * D_MODEL), F32),
        compiler_params=_cparams(("arbitrary", "arbitrary")),
        name="ada",
    )(c_all, w_ada, b_ada.reshape(DEPTH, 1, 6 * D_MODEL))


def _proj_kernel(x_ref, ada_ref, w_ref, h_ref, main_ref, f1_ref, f2_ref, scr_ref, *, chunk):
    x = x_ref[0]
    ada = ada_ref[0]
    tm = x.shape[0]
    h = _ln(x) * (1.0 + ada[1:2]) + ada[0:1]
    hb = h.astype(BF16)
    h_ref[0] = hb
    for j in range(MAIN_COLS // chunk):
        sl = slice(j * chunk, (j + 1) * chunk)
        main_ref[0, :, sl] = jnp.dot(hb, w_ref[:, sl], preferred_element_type=F32).astype(BF16)
    for g, f_ref in ((1, f1_ref), (2, f2_ref)):
        dil = A_PAIRS[g][1]
        c0 = MAIN_COLS + (g - 1) * A_QKV
        res = jnp.dot(hb, w_ref[:, c0:c0 + A_QKV], preferred_element_type=F32)
        for c in range(A_QKV // 128):
            scr_ref[c] = res[:, c * 128:(c + 1) * 128]
        for r in range(dil):
            for c in range(A_QKV // 128):
                lo = r * A_QKV + c * 128
                f_ref[0, :, lo:lo + 128] = scr_ref[c, pl.ds(r, tm // dil, stride=dil), :].astype(BF16)


def _proj_call(x, ada8, w_qkv, tm):
    bsz, seq, _ = x.shape
    d1, d2 = A_PAIRS[1][1], A_PAIRS[2][1]
    return pl.pallas_call(
        functools.partial(_proj_kernel, chunk=512),
        grid=(bsz, seq // tm),
        in_specs=[
            pl.BlockSpec((1, tm, D_MODEL), lambda b, i: (b, i, 0)),
            pl.BlockSpec((1, 8, D_MODEL), lambda b, i: (b, 0, 0)),
            pl.BlockSpec((D_MODEL, QKV_COLS), lambda b, i: (0, 0), pipeline_mode=pl.Buffered(1)),
        ],
        out_specs=[
            pl.BlockSpec((1, tm, D_MODEL), lambda b, i: (b, i, 0)),
            pl.BlockSpec((1, tm, MAIN_COLS), lambda b, i: (b, i, 0)),
            pl.BlockSpec((1, tm // d1, d1 * A_QKV), lambda b, i: (b, i, 0)),
            pl.BlockSpec((1, tm // d2, d2 * A_QKV), lambda b, i: (b, i, 0)),
        ],
        out_shape=[
            jax.ShapeDtypeStruct((bsz, seq, D_MODEL), BF16),
            jax.ShapeDtypeStruct((bsz, seq, MAIN_COLS), BF16),
            jax.ShapeDtypeStruct((bsz, seq // d1, d1 * A_QKV), BF16),
            jax.ShapeDtypeStruct((bsz, seq // d2, d2 * A_QKV), BF16),
        ],
        scratch_shapes=[pltpu.VMEM((A_QKV // 128, tm, 128), F32)],
        compiler_params=_cparams(("parallel", "parallel")),
        name="proj",
    )(x, ada8, w_qkv)


def _attn_a_kernel(q_ref, kc_ref, kp_ref, kn_ref, vc_ref, vp_ref, vn_ref, o_ref, lse_ref,
                   *, tq, sub, slopes, dil):
    i = pl.program_id(2)
    q = q_ref[0]
    k = jnp.concatenate([kp_ref[0], kc_ref[0], kn_ref[0]], axis=0)
    v = jnp.concatenate([vp_ref[0], vc_ref[0], vn_ref[0]], axis=0)
    sq = ATTN_SQ
    kw = sq + 2 * A_HALF
    row = lax.broadcasted_iota(jnp.int32, (sq, kw), 0)
    col = lax.broadcasted_iota(jnp.int32, (sq, kw), 1)
    rel = col - A_HALF - row
    dist = jnp.abs(rel).astype(F32) * float(dil)
    even_q, odd_q = _split_head_pairs(q)
    even_lanes = (lax.broadcasted_iota(jnp.int32, (sq, 128), 1) < HEAD_DIM)
    for j in range(tq // sq):
        kpos = i * tq + j * sq - A_HALF + col
        valid = (jnp.abs(rel) <= A_HALF) & (kpos >= 0) & (kpos < sub)
        rows = slice(j * sq, (j + 1) * sq)
        kj = k[j * sq:j * sq + kw]
        vj = v[j * sq:j * sq + kw]
        hs = range(A_HEADS)
        pss = [slice((h // 2) * 128, (h // 2 + 1) * 128) for h in hs]
        ss = [_dot_nt((even_q, odd_q)[h % 2][rows, ps], kj[:, ps]) for h, ps in zip(hs, pss)]
        ss = [jnp.where(valid, s - slopes[h] * dist, NEG) for s, h in zip(ss, hs)]
        ms = [jnp.max(s, axis=-1, keepdims=True) for s in ss]
        pp = [jnp.exp(s - m) for s, m in zip(ss, ms)]
        dens = [jnp.sum(p, axis=-1, keepdims=True) for p in pp]
        outs = [jnp.dot(p.astype(BF16), vj[:, ps], preferred_element_type=F32) / den
                for p, ps, den in zip(pp, pss, dens)]
        lses = [m + jnp.log(den) for m, den in zip(ms, dens)]
        for n in range(0, A_HEADS, 2):
            o_ref[0, rows, pss[n]] = jnp.where(even_lanes, outs[n], outs[n + 1]).astype(BF16)
            lse_ref[0, rows, pss[n]] = jnp.where(even_lanes, lses[n], lses[n + 1])


def _attn_a_call(folded, g, seq):
    bsz = folded.shape[0]
    _, dil = A_PAIRS[g]
    sub = seq // dil
    tq = min(ATTN_TQ, sub)
    nq = sub // tq
    hb = tq // A_HALF
    nhb = sub // A_HALF
    nblk = folded.shape[2] // dil // 256
    cq = COL_A0 if g == 0 else 0
    slopes = _alibi(A_GROUPS * A_HEADS)[g * A_HEADS:(g + 1) * A_HEADS]

    def cur(c):
        return pl.BlockSpec((1, tq, 256), lambda b, r, i: (b, i, r * nblk + c))

    def prev(c):
        return pl.BlockSpec((1, A_HALF, 256), lambda b, r, i: (b, jnp.maximum(i * hb - 1, 0), r * nblk + c))

    def nxt(c):
        return pl.BlockSpec((1, A_HALF, 256), lambda b, r, i: (b, jnp.minimum((i + 1) * hb, nhb - 1), r * nblk + c))

    return pl.pallas_call(
        functools.partial(_attn_a_kernel, tq=tq, sub=sub, slopes=slopes, dil=dil),
        grid=(bsz, dil, nq),
        in_specs=[cur(cq), cur(cq + 1), prev(cq + 1), nxt(cq + 1), cur(cq + 2), prev(cq + 2), nxt(cq + 2)],
        out_specs=[
            pl.BlockSpec((1, tq, A_WIDTH), lambda b, r, i: (b, i, r)),
            pl.BlockSpec((1, tq, A_WIDTH), lambda b, r, i: (b, i, r)),
        ],
        out_shape=[
            jax.ShapeDtypeStruct((bsz, sub, dil * A_WIDTH), BF16),
            jax.ShapeDtypeStruct((bsz, sub, dil * A_WIDTH), F32),
        ],
        compiler_params=_cparams(("parallel", "parallel", "parallel")),
        name=f"attn_a{g}",
    )(folded, folded, folded, folded, folded, folded, folded)


def _attn_b_kernel(sink_ref, q_ref, kvc_ref, kvp_ref, kvn_ref, o_ref, *, tq, seq, slopes):
    i = pl.program_id(1)
    q = q_ref[0]
    kv = jnp.concatenate([kvp_ref[0], kvc_ref[0], kvn_ref[0]], axis=0)
    sq = ATTN_SQ
    kw = sq + 2 * B_WINDOW
    row = lax.broadcasted_iota(jnp.int32, (sq, kw), 0)
    col = lax.broadcasted_iota(jnp.int32, (sq, kw), 1)
    rel = col - B_WINDOW - row
    dist = jnp.abs(rel).astype(F32)
    rep = B_Q_HEADS // B_KV_HEADS
    even_q, odd_q = _split_head_pairs(q)
    even_lanes = (lax.broadcasted_iota(jnp.int32, (sq, 128), 1) < HEAD_DIM)

    def doubled(c):
        piece = kv[:, c * HEAD_DIM:(c + 1) * HEAD_DIM]
        return jnp.concatenate([piece, piece], axis=1)

    k2 = [doubled(g) for g in range(B_KV_HEADS)]
    v2 = [doubled(B_KV_HEADS + g) for g in range(B_KV_HEADS)]
    for j in range(tq // sq):
        kpos = i * tq + j * sq - B_WINDOW + col
        valid = (jnp.abs(rel) <= B_WINDOW) & (kpos >= 0) & (kpos < seq)
        rows = slice(j * sq, (j + 1) * sq)
        for h0 in range(0, B_Q_HEADS, B_INTERLEAVE):
            hs = range(h0, h0 + B_INTERLEAVE)
            qms = [(even_q, odd_q)[h % 2][rows, (h // 2) * 128:(h // 2 + 1) * 128] for h in hs]
            kjs = [k2[h // rep][j * sq:j * sq + kw] for h in hs]
            vjs = [v2[h // rep][j * sq:j * sq + kw] for h in hs]
            sks = [sink_ref[h] for h in hs]
            ss = [_dot_nt(qm, kj) for qm, kj in zip(qms, kjs)]
            ss = [jnp.where(valid, s - slopes[h] * dist, NEG) for s, h in zip(ss, hs)]
            ms = [jnp.maximum(jnp.max(s, axis=-1, keepdims=True), sk) for s, sk in zip(ss, sks)]
            pp = [jnp.exp(s - m) for s, m in zip(ss, ms)]
            dens = [jnp.sum(p, axis=-1, keepdims=True) + jnp.exp(sk - m) for p, sk, m in zip(pp, sks, ms)]
            outs = [jnp.dot(p.astype(BF16), vj, preferred_element_type=F32) / den
                    for p, vj, den in zip(pp, vjs, dens)]
            for n in range(0, B_INTERLEAVE, 2):
                ps = slice((h0 + n) // 2 * 128, ((h0 + n) // 2 + 1) * 128)
                o_ref[0, rows, ps] = jnp.where(even_lanes, outs[n], outs[n + 1]).astype(BF16)


def _attn_b_call(qkv, sink):
    bsz, seq, _ = qkv.shape
    tq = min(ATTN_TQ, seq)
    hb = tq // B_WINDOW
    nhb = seq // B_WINDOW
    return pl.pallas_call(
        functools.partial(_attn_b_kernel, tq=tq, seq=seq, slopes=_alibi(B_Q_HEADS)),
        grid=(bsz, seq // tq),
        in_specs=[
            pl.BlockSpec(memory_space=pltpu.SMEM),
            pl.BlockSpec((1, tq, B_WIDTH), lambda b, i: (b, i, COL_BQ)),
            pl.BlockSpec((1, tq, 256), lambda b, i: (b, i, COL_BKV)),
            pl.BlockSpec((1, B_WINDOW, 256), lambda b, i: (b, jnp.maximum(i * hb - 1, 0), COL_BKV)),
            pl.BlockSpec((1, B_WINDOW, 256), lambda b, i: (b, jnp.minimum((i + 1) * hb, nhb - 1), COL_BKV)),
        ],
        out_specs=pl.BlockSpec((1, tq, B_WIDTH), lambda b, i: (b, i, 0)),
        out_shape=jax.ShapeDtypeStruct((bsz, seq, B_WIDTH), BF16),
        compiler_params=_cparams(("parallel", "parallel")),
        name="attn_b",
    )(sink, qkv, qkv, qkv, qkv)


C_ROWS_PER_STEP = 16
C_INTERLEAVE_ROWS = 4


def _attn_c_kernel(q_ref, k_ref, v_ref, bias_ref, o_ref, *, rows):
    i = pl.program_id(1)
    kwin = C_WIN_R * GRID_W
    even_q, odd_q = _split_head_pairs(q_ref[0])
    even_lanes = (lax.broadcasted_iota(jnp.int32, (GRID_W, 128), 1) < HEAD_DIM)
    for jr0 in range(0, C_ROWS_PER_STEP, C_INTERLEAVE_ROWS):
        units = []
        for jr in range(jr0, jr0 + C_INTERLEAVE_ROWS):
            r = i * C_ROWS_PER_STEP + jr
            rs = jnp.clip(r - C_WIN_R // 2, 0, rows - C_WIN_R)
            start = pl.multiple_of(rs * GRID_W, GRID_W)
            kw = k_ref[0, pl.ds(start, kwin), :]
            vw = v_ref[0, pl.ds(start, kwin), :]
            for h in range(C_HEADS):
                units.append((slice(jr * GRID_W, (jr + 1) * GRID_W), h, r - rs, kw, vw))
        pss = [slice((h // 2) * 128, (h // 2 + 1) * 128) for _, h, _, _, _ in units]
        ss = [_dot_nt((even_q, odd_q)[h % 2][qrows, ps], kw[:, ps]) + bias_ref[h, delta]
              for (qrows, h, delta, kw, _), ps in zip(units, pss)]
        ms = [jnp.max(s, axis=-1, keepdims=True) for s in ss]
        pp = [jnp.exp(s - m) for s, m in zip(ss, ms)]
        dens = [jnp.sum(p, axis=-1, keepdims=True) for p in pp]
        outs = [jnp.dot(p.astype(BF16), vw[:, ps], preferred_element_type=F32) / den
                for p, (_, _, _, _, vw), ps, den in zip(pp, units, pss, dens)]
        for n in range(0, len(units), 2):
            o_ref[0, units[n][0], pss[n]] = jnp.where(even_lanes, outs[n], outs[n + 1]).astype(BF16)


def _c_bias_table(rpb):
    qc = jnp.arange(GRID_W)[:, None]
    kc = jnp.arange(GRID_W)[None, :]
    col_off = jnp.clip(kc - qc + (C_WIN_C - 1), 0, 2 * C_WIN_C - 2)
    sel = (col_off[None] == jnp.arange(2 * C_WIN_C - 1)[:, None, None]).astype(F32)
    band = jnp.einsum("hrc,cqk->hrqk", rpb.astype(F32), sel, precision=HIGHEST)
    cs = jnp.clip(qc - C_WIN_C // 2, 0, GRID_W - C_WIN_C)
    valid = (kc >= cs) & (kc < cs + C_WIN_C)
    band = jnp.where(valid[None, None], band, NEG)
    tabs = [band[:, C_WIN_R - 1 - d:2 * C_WIN_R - 1 - d].transpose(0, 2, 1, 3) for d in range(C_WIN_R)]
    return jnp.stack(tabs, axis=1).reshape(C_HEADS, C_WIN_R, GRID_W, C_WIN_R * GRID_W)


def _attn_c_call(qkv, bias_tab):
    bsz, seq, _ = qkv.shape
    rows = seq // GRID_W
    tq = C_ROWS_PER_STEP * GRID_W
    return pl.pallas_call(
        functools.partial(_attn_c_kernel, rows=rows),
        grid=(bsz, seq // tq),
        in_specs=[
            pl.BlockSpec((1, tq, C_WIDTH), lambda b, i: (b, i, COL_CQ)),
            pl.BlockSpec((1, seq, C_WIDTH), lambda b, i: (b, 0, COL_CQ + 1)),
            pl.BlockSpec((1, seq, C_WIDTH), lambda b, i: (b, 0, COL_CQ + 2)),
            pl.BlockSpec(bias_tab.shape, lambda b, i: (0, 0, 0, 0)),
        ],
        out_specs=pl.BlockSpec((1, tq, C_WIDTH), lambda b, i: (b, i, 0)),
        out_shape=jax.ShapeDtypeStruct((bsz, seq, C_WIDTH), BF16),
        compiler_params=_cparams(("parallel", "arbitrary")),
        name="attn_c",
    )(qkv, qkv, qkv, bias_tab)


def _cmul(ar, ai, br, bi):
    return ar * br - ai * bi, ar * bi + ai * br


def _ssm_param_kernel(lre_ref, lim_ref, ldt_ref, cre_ref, cim_ref, bre_ref, bim_ref, d_ref,
                      m_ref, bend_ref, cin_ref, apow_ref):
    tc, hh, pp = SSM_CHUNK, SSM_GROUP, SSM_STATE
    it = lax.broadcasted_iota(jnp.int32, (tc, pp), 0).astype(F32)
    lane = lax.broadcasted_iota(jnp.int32, (hh, 128), 1)
    subl = lax.broadcasted_iota(jnp.int32, (hh, 128), 0)
    tabs = []
    for d in range(2):
        lr, li = lre_ref[d, 0, 0], lim_ref[d, 0, 0]
        dt = jnp.exp(ldt_ref[d, 0, 0])

        def powers(kv):
            mag = jnp.exp(kv * (lr * dt))
            ang = kv * (li * dt)
            return mag * jnp.cos(ang), mag * jnp.sin(ang)

        one = jnp.ones((1, pp), F32)
        lbr, lbi = powers(one)
        den = lr * lr + li * li
        nr, ni = lbr - 1.0, lbi
        fr = (nr * lr + ni * li) / den
        fi = (ni * lr - nr * li) / den
        apr, api = powers(one * float(tc))
        apow_ref[d, 0] = jnp.concatenate([apr, api], axis=1)

        cr, ci = cre_ref[d, 0], cim_ref[d, 0]
        br, bi = bre_ref[d, 0], bim_ref[d, 0]

        wr, wi = powers(it + 1.0 if d == 0 else float(tc) - it)
        cwr, cwi = _cmul(cr[None], ci[None], wr[:, None, :], wi[:, None, :])
        cin = jnp.concatenate([cwr, -cwi], axis=2).reshape(tc * hh, 2 * pp)
        cin_ref[d, 0] = cin.astype(BF16)

        wr, wi = powers(float(tc - 1) - it if d == 0 else it)
        wfr, wfi = _cmul(wr, wi, fr, fi)
        er, ei = _cmul(br[None], bi[None], wfr[:, None, :], wfi[:, None, :])
        bend_ref[d, 0] = jnp.concatenate([er, ei], axis=2).reshape(tc * hh, 2 * pp).astype(BF16)

        wr, wi = powers(it if d == 0 else float(tc - 1) - it)
        wfr, wfi = _cmul(wr, wi, fr, fi)
        gr, gi = _cmul(cr[None], ci[None], wfr[:, None, :], wfi[:, None, :])
        gr = gr.reshape(tc * hh, pp)
        gi = gi.reshape(tc * hh, pp)
        tabs.append(_dot_nt(br, gr, precision=HIGHEST) - _dot_nt(bi, gi, precision=HIGHEST))

    ktf, ktb = tabs
    last = pltpu.roll(ktb[:, SSM_CW - 128:], hh, axis=1)
    diag = jnp.where(lane == subl, d_ref[0], 0.0)
    first = ktf[:, :128] + jnp.where(lane < hh, last + diag, 0.0)
    ktf = jnp.concatenate([first, ktf[:, 128:]], axis=1)
    lane_w = lax.broadcasted_iota(jnp.int32, (hh, SSM_CW), 1)
    ktb = jnp.where(lane_w < SSM_CW - hh, ktb, 0.0)
    zeros = jnp.zeros((hh, SSM_CW), F32)
    pf = jnp.concatenate([ktf, zeros], axis=1)
    pb = jnp.concatenate([ktb, zeros], axis=1)
    for j in range(tc):
        blk = pf if j == 0 else pltpu.roll(pf, j * hh, axis=1)
        sh = (tc - 1 - j) * hh
        blb = pb if sh == 0 else pltpu.roll(pb, 2 * SSM_CW - sh, axis=1)
        m_ref[0, j * hh:(j + 1) * hh, :] = (blk[:, :SSM_CW] + blb[:, :SSM_CW]).astype(BF16)


def _ssm_param_call(lam_re, lam_im, log_dt, b_re, b_im, c_re, c_im, d_skip):
    gg, pp, hh = SSM_GROUPS, SSM_STATE, SSM_GROUP
    nl = lam_re.shape[0]
    lre = lam_re.reshape(nl * 2, gg, 1, 1, pp)
    lim = lam_im.reshape(nl * 2, gg, 1, 1, pp)
    ldt = jnp.broadcast_to(log_dt.reshape(nl * 2, gg, 1, 1, 1), (nl * 2, gg, 1, 1, pp))
    cre = c_re.reshape(nl * 2, gg, hh, pp)
    cim = c_im.reshape(nl * 2, gg, hh, pp)
    bre = jnp.swapaxes(b_re, -1, -2).reshape(nl * 2, gg, hh, pp)
    bim = jnp.swapaxes(b_im, -1, -2).reshape(nl * 2, gg, hh, pp)
    dpad = jnp.pad(d_skip.reshape(nl * gg, 1, hh), ((0, 0), (0, 0), (0, 128 - hh)))

    def vec(l, g):
        return (l, g, 0, 0, 0)

    def mat(l, g):
        return (l, g, 0, 0)

    return pl.pallas_call(
        _ssm_param_kernel,
        grid=(nl, gg),
        in_specs=[
            pl.BlockSpec((2, 1, 1, 1, pp), vec), pl.BlockSpec((2, 1, 1, 1, pp), vec),
            pl.BlockSpec((2, 1, 1, 1, pp), vec),
            pl.BlockSpec((2, 1, hh, pp), mat), pl.BlockSpec((2, 1, hh, pp), mat),
            pl.BlockSpec((2, 1, hh, pp), mat), pl.BlockSpec((2, 1, hh, pp), mat),
            pl.BlockSpec((1, 1, 128), lambda l, g: (l * gg + g, 0, 0)),
        ],
        out_specs=[
            pl.BlockSpec((1, SSM_CW, SSM_CW), lambda l, g: (l * gg + g, 0, 0)),
            pl.BlockSpec((2, 1, SSM_CW, 2 * pp), mat),
            pl.BlockSpec((2, 1, SSM_CW, 2 * pp), mat),
            pl.BlockSpec((2, 1, 1, 2 * pp), mat),
        ],
        out_shape=[
            jax.ShapeDtypeStruct((nl * gg, SSM_CW, SSM_CW), BF16),
            jax.ShapeDtypeStruct((nl * 2, gg, SSM_CW, 2 * pp), BF16),
            jax.ShapeDtypeStruct((nl * 2, gg, SSM_CW, 2 * pp), BF16),
            jax.ShapeDtypeStruct((nl * 2, gg, 1, 2 * pp), F32),
        ],
        compiler_params=_cparams(("arbitrary", "arbitrary")),
        name="ssm_param",
    )(lre, lim, ldt, cre, cim, bre, bim, dpad)


def _ssm_state_kernel(u_ref, bend_ref, s_ref):
    for g in range(SSM_GROUPS):
        u = u_ref[0, g]
        for d in range(2):
            s_ref[0, d, :, g * 128:(g + 1) * 128] = jnp.dot(u, bend_ref[d, g], preferred_element_type=F32)


def _ssm_scan_kernel(s_ref, apow_ref, x_ref, *, nc, bb):
    width = SSM_GROUPS * 128
    lane = lax.broadcasted_iota(jnp.int32, (1, width), 1)
    first_half = (lane % 128) < SSM_STATE
    a_re, a_im = [], []
    for d in range(2):
        a = apow_ref[d]
        a_re.append(jnp.where(first_half, a, pltpu.roll(a, SSM_STATE, axis=1)))
        a_im.append(jnp.where(first_half, -pltpu.roll(a, width - SSM_STATE, axis=1), a))

    def body(c, xs):
        new = []
        for n, x in enumerate(xs):
            b, d = divmod(n, 2)
            cc = c if d == 0 else nc - 1 - c
            x_ref[b, d, pl.ds(cc, 1), :] = x
            x_sw_lo = pltpu.roll(x, SSM_STATE, axis=1)
            x_sw_hi = pltpu.roll(x, width - SSM_STATE, axis=1)
            x_sw = jnp.where(first_half, x_sw_hi, x_sw_lo)
            new.append(a_re[d] * x + a_im[d] * x_sw + s_ref[b, d, pl.ds(cc, 1), :])
        return tuple(new)

    lax.fori_loop(0, nc, body, tuple(jnp.zeros((1, width), F32) for _ in range(2 * bb)))


def _ssm_out_kernel(u_ref, m_ref, cin_ref, x_ref, y_ref):
    bsz, _, nc, cw = u_ref.shape
    y = jnp.dot(u_ref[:, 0].reshape(bsz * nc, cw), m_ref[0], preferred_element_type=F32)
    for d in range(2):
        y = y + _dot_nt(x_ref[:, d].reshape(bsz * nc, 128).astype(BF16), cin_ref[d, 0])
    y_ref[:, 0] = y.reshape(bsz, nc, cw).astype(BF16)


def _ssm_call(qkv, prm):
    m_mat, bend, cin, apow = prm
    bsz, seq, _ = qkv.shape
    gg, tc, hh = SSM_GROUPS, SSM_CHUNK, SSM_GROUP
    nc = seq // tc
    u = qkv[:, :, COL_DU * 256:(COL_DU + 1) * 256]
    u = u.reshape(bsz, nc, tc, gg, hh).transpose(0, 3, 1, 2, 4).reshape(bsz, gg, nc, SSM_CW)
    width = gg * 128
    bb = math.gcd(bsz, SSM_SCAN_BATCH)
    s = pl.pallas_call(
        _ssm_state_kernel,
        grid=(bsz,),
        in_specs=[
            pl.BlockSpec((1, gg, nc, SSM_CW), lambda b: (b, 0, 0, 0)),
            pl.BlockSpec((2, gg, SSM_CW, 128), lambda b: (0, 0, 0, 0)),
        ],
        out_specs=pl.BlockSpec((1, 2, nc, width), lambda b: (b, 0, 0, 0)),
        out_shape=jax.ShapeDtypeStruct((bsz, 2, nc, width), F32),
        compiler_params=_cparams(("parallel",)),
        name="ssm_state",
    )(u, bend)
    xprev = pl.pallas_call(
        functools.partial(_ssm_scan_kernel, nc=nc, bb=bb),
        grid=(bsz // bb,),
        in_specs=[
            pl.BlockSpec((bb, 2, nc, width), lambda b: (b, 0, 0, 0)),
            pl.BlockSpec((2, 1, width), lambda b: (0, 0, 0)),
        ],
        out_specs=pl.BlockSpec((bb, 2, nc, width), lambda b: (b, 0, 0, 0)),
        out_shape=jax.ShapeDtypeStruct((bsz, 2, nc, width), F32),
        compiler_params=_cparams(("parallel",)),
        name="ssm_scan",
    )(s, apow.reshape(2, 1, width))
    y = pl.pallas_call(
        _ssm_out_kernel,
        grid=(gg,),
        in_specs=[
            pl.BlockSpec((bsz, 1, nc, SSM_CW), lambda g: (0, g, 0, 0)),
            pl.BlockSpec((1, SSM_CW, SSM_CW), lambda g: (g, 0, 0)),
            pl.BlockSpec((2, 1, SSM_CW, 128), lambda g: (0, g, 0, 0)),
            pl.BlockSpec((bsz, 2, nc, 128), lambda g: (0, 0, 0, g)),
        ],
        out_specs=pl.BlockSpec((bsz, 1, nc, SSM_CW), lambda g: (0, g, 0, 0)),
        out_shape=jax.ShapeDtypeStruct((bsz, gg, nc, SSM_CW), BF16),
        compiler_params=_cparams(("arbitrary",)),
        name="ssm_out",
    )(u, m_mat, cin, xprev)
    return y.reshape(bsz, gg, nc, tc, hh).transpose(0, 2, 3, 1, 4).reshape(bsz, seq, D_WIDTH)


def _merge_kernel(x_ref, h_ref, ada_ref,
                  oa0_ref, oa1_ref, oa2_ref, la0_ref, la1_ref, la2_ref, yb_ref, yc_ref, s5_ref,
                  wg_ref, bg_ref, wbr_ref, wout_ref, wglu_ref, bglu_ref, lng_ref, lnb_ref,
                  wr_ref, br_ref,
                  x1_ref, h2_ref, route_ref, o1_scr, l1_scr, o2_scr, l2_scr):
    ada = ada_ref[0]
    hb = h_ref[0]
    tm = hb.shape[0]
    for g, (o_ref, l_ref, o_scr, l_scr) in ((1, (oa1_ref, la1_ref, o1_scr, l1_scr)),
                                            (2, (oa2_ref, la2_ref, o2_scr, l2_scr))):
        dil = A_PAIRS[g][1]
        for r in range(dil):
            for c in range(A_WIDTH // 128):
                cs = slice(r * A_WIDTH + c * 128, r * A_WIDTH + (c + 1) * 128)
                o_scr[c, pl.ds(r, tm // dil, stride=dil), :] = o_ref[0, :, cs].astype(F32)
                l_scr[c, pl.ds(r, tm // dil, stride=dil), :] = l_ref[0, :, cs]

    def unfolded(scr):
        return jnp.concatenate([scr[c] for c in range(A_WIDTH // 128)], axis=1)

    l0, l1, l2 = la0_ref[0], unfolded(l1_scr), unfolded(l2_scr)
    lm = jnp.maximum(jnp.maximum(l0, l1), l2)
    e0, e1, e2 = jnp.exp(l0 - lm), jnp.exp(l1 - lm), jnp.exp(l2 - lm)
    ya = (e0 * oa0_ref[0].astype(F32) + e1 * unfolded(o1_scr) + e2 * unfolded(o2_scr)) / (e0 + e1 + e2)
    s5 = s5_ref[0].astype(F32)
    gel = 0.5 * s5 * (1.0 + jnp.tanh(math.sqrt(2.0 / math.pi) * (s5 + 0.044715 * (s5 * s5 * s5))))
    z = jnp.dot(gel.astype(BF16), wglu_ref[...], preferred_element_type=F32) + bglu_ref[...]
    yd = z[:, :D_WIDTH] * _sigmoid(z[:, D_WIDTH:])
    branches = (ya.astype(BF16), yb_ref[0], yc_ref[0], yd.astype(BF16))
    offs = (0, A_WIDTH, A_WIDTH + B_WIDTH, A_WIDTH + B_WIDTH + C_WIDTH, A_WIDTH + B_WIDTH + C_WIDTH + D_WIDTH)
    def chunk_dots(c0):
        css = [slice(n * D_MODEL + c0, n * D_MODEL + c0 + MERGE_COLS) for n in range(N_BRANCH)]
        pre = [jnp.dot(hb, wg_ref[:, cs], preferred_element_type=F32) + bg_ref[:, cs] for cs in css]
        brs = [jnp.dot(yb, wbr_ref[offs[n]:offs[n + 1], c0:c0 + MERGE_COLS], preferred_element_type=F32)
               for n, yb in enumerate(branches)]
        return pre, brs

    merged = []
    nxt = chunk_dots(0)
    for c0 in range(0, D_MODEL, MERGE_COLS):
        pre, brs = nxt
        if c0 + MERGE_COLS < D_MODEL:
            nxt = chunk_dots(c0 + MERGE_COLS)
        terms = [_sigmoid(g) * b for g, b in zip(pre, brs)]
        merged.append(((terms[0] + terms[1]) + terms[2]) + terms[3])
    merged = jnp.concatenate(merged, axis=1)
    mix = jnp.dot(merged.astype(BF16), wout_ref[...], preferred_element_type=F32)
    x1 = _ln(ALPHA * x_ref[0] + (1.0 + ada[2:3]) * mix) * lng_ref[...] + lnb_ref[...]
    x1_ref[0] = x1
    h2 = _ln(x1) * (1.0 + ada[4:5]) + ada[3:4]
    for s in range(ROW_TILE):
        h2_ref[pl.ds(s, tm, stride=ROW_TILE), :] = h2[:, s * 128:(s + 1) * 128]
    h2_hi = h2.astype(BF16)
    h2_lo = (h2 - h2_hi.astype(F32)).astype(BF16)
    logit = (jnp.dot(h2_hi, wr_ref[0], preferred_element_type=F32)
             + (jnp.dot(h2_lo, wr_ref[0], preferred_element_type=F32)
                + jnp.dot(h2_hi, wr_ref[1], preferred_element_type=F32))) + br_ref[...]
    lane = lax.broadcasted_iota(jnp.int32, logit.shape, 1)
    ninf = -jnp.inf
    gmask = lane < MOE_GROUPS
    gl = jnp.where(gmask, logit, ninf)
    gmax = jnp.max(gl, axis=-1, keepdims=True)
    gsel = jnp.min(jnp.where(gl == gmax, lane, 1 << 20), axis=-1, keepdims=True)
    gw = 1.0 / jnp.sum(jnp.exp(gl - gmax), axis=-1, keepdims=True)
    lo = MOE_GROUPS + gsel * MOE_EPG
    emask = (lane >= lo) & (lane < lo + MOE_EPG)
    el = jnp.where(emask, logit, ninf)
    v1 = jnp.max(el, axis=-1, keepdims=True)
    i1 = jnp.min(jnp.where(el == v1, lane, 1 << 20), axis=-1, keepdims=True)
    el2 = jnp.where(lane == i1, ninf, el)
    v2 = jnp.max(el2, axis=-1, keepdims=True)
    i2 = jnp.min(jnp.where(el2 == v2, lane, 1 << 20), axis=-1, keepdims=True)
    t = jnp.exp(v2 - v1)
    w1 = gw / (1.0 + t)
    w2 = gw * t / (1.0 + t)
    route = jnp.where(lane == 0, (i1 - MOE_GROUPS).astype(F32),
                      jnp.where(lane == 1, (i2 - MOE_GROUPS).astype(F32),
                                jnp.where(lane == 2, w1, jnp.where(lane == 3, w2, 0.0))))
    route_ref[0] = route


def _merge_call(x, h, ada8, oa, la, yb, yc, s5, lw, tm):
    bsz, seq, _ = x.shape
    nt = seq // tm

    def tok(width):
        return pl.BlockSpec((1, tm, width), lambda b, i: (b, i, 0))

    def fold(g):
        dil = A_PAIRS[g][1]
        return pl.BlockSpec((1, tm // dil, dil * A_WIDTH), lambda b, i: (b, i, 0))

    def full(arr):
        return pl.BlockSpec(arr.shape, lambda b, i: (0,) * arr.ndim, pipeline_mode=pl.Buffered(1))

    weights = (lw["w_gate"], lw["b_gate"], lw["w_branch"], lw["w_out"], lw["w_glu"], lw["b_glu"],
               lw["ln1_g"], lw["ln1_b"], lw["w_route"], lw["b_route"])
    return pl.pallas_call(
        _merge_kernel,
        grid=(bsz, nt),
        in_specs=[tok(D_MODEL), tok(D_MODEL), pl.BlockSpec((1, 8, D_MODEL), lambda b, i: (b, 0, 0)),
                  tok(A_WIDTH), fold(1), fold(2), tok(A_WIDTH), fold(1), fold(2),
                  tok(B_WIDTH), tok(C_WIDTH), tok(D_WIDTH)] + [full(w) for w in weights],
        out_specs=[tok(D_MODEL), pl.BlockSpec((tm * ROW_TILE, 128), lambda b, i: (b * nt + i, 0)), tok(128)],
        out_shape=[
            jax.ShapeDtypeStruct((bsz, seq, D_MODEL), F32),
            jax.ShapeDtypeStruct((bsz * seq * ROW_TILE, 128), F32),
            jax.ShapeDtypeStruct((bsz, seq, 128), F32),
        ],
        scratch_shapes=[pltpu.VMEM((A_WIDTH // 128, tm, 128), F32)] * 4,
        compiler_params=_cparams(("parallel", "parallel")),
        name="merge",
    )(x, h, ada8, oa[0], oa[1], oa[2], la[0], la[1], la[2], yb, yc, s5, *weights)


def _moe_plan(eid, tm):
    ntok = eid.shape[0]
    npair = 2 * ntok
    blk = 256
    e_flat = jnp.concatenate([eid[:, 0], eid[:, 1]])
    experts = jnp.arange(N_EXPERTS, dtype=jnp.int32)
    oh = (e_flat[:, None] == experts[None, :]).astype(F32).reshape(npair // blk, blk, N_EXPERTS)
    tri = (jnp.arange(blk)[:, None] >= jnp.arange(blk)[None, :]).astype(F32)
    within = jnp.einsum("ij,bjk->bik", tri, oh, precision=HIGHEST)
    btot = within[:, -1, :]
    before = jnp.cumsum(btot, axis=0) - btot
    rank = (within - 1.0 + before[:, None, :]).reshape(npair, N_EXPERTS)
    oh = oh.reshape(npair, N_EXPERTS)
    counts = jnp.sum(btot, axis=0).astype(jnp.int32)
    pcounts = (counts + tm - 1) // tm * tm
    pend = jnp.cumsum(pcounts)
    pstart = pend - pcounts
    pos = jnp.sum(oh * (rank + pstart.astype(F32)[None, :]), axis=1).astype(jnp.int32)
    nrow = npair + N_EXPERTS * tm
    ntile = nrow // tm
    tile_expert = jnp.sum((pend[None, :] <= (jnp.arange(ntile, dtype=jnp.int32) * tm)[:, None]).astype(jnp.int32), axis=1)
    tile_expert = jnp.minimum(tile_expert, N_EXPERTS - 1)
    n_used = (pend[-1] // tm).astype(jnp.int32).reshape(1)
    padcnt = pcounts - counts
    padend = jnp.cumsum(padcnt)
    j = jnp.arange(N_EXPERTS * tm, dtype=jnp.int32)
    e_j = jnp.sum((padend[None, :] <= j[:, None]).astype(jnp.int32), axis=1)
    shift = pstart + counts - (padend - padcnt)
    row_pad = jnp.sum((e_j[:, None] == experts[None, :]).astype(jnp.int32) * shift[None, :], axis=1) + j
    zero_rows = jnp.where(e_j < N_EXPERTS, row_pad, pend[-1] + j - padend[-1])
    return pos, zero_rows, tile_expert, n_used


def _dispatch_kernel(pos_ref, zr_ref, h_ref, xs_ref, zbuf, sem, *, ntok, tm, nz):
    i = pl.program_id(0)

    def issue(r, carry):
        src = h_ref.at[pl.ds(pl.multiple_of(r * ROW_TILE, ROW_TILE), ROW_TILE)]
        for k in range(2):
            row = pos_ref[k * ntok + i * tm + r]
            dst = xs_ref.at[pl.ds(pl.multiple_of(row * ROW_TILE, ROW_TILE), ROW_TILE)]
            pltpu.make_async_copy(src, dst, sem).start(priority=k)
        return carry

    lax.fori_loop(0, tm, issue, 0, unroll=ISSUE_UNROLL)
    zbuf[...] = jnp.zeros_like(zbuf)

    def zero(r, carry):
        for k in range(2):
            row = zr_ref[i * nz + 2 * r + k]
            dst = xs_ref.at[pl.ds(pl.multiple_of(row * ROW_TILE, ROW_TILE), ROW_TILE)]
            pltpu.make_async_copy(zbuf, dst, sem).start(priority=k)
        return carry

    lax.fori_loop(0, nz // 2, zero, 0)
    for _ in range(2):
        pltpu.make_async_copy(h_ref, xs_ref.at[pl.ds(0, tm * ROW_TILE)], sem).wait()
    pltpu.make_async_copy(h_ref.at[pl.ds(0, nz * ROW_TILE)], xs_ref.at[pl.ds(0, nz * ROW_TILE)], sem).wait()


def _dispatch_call(h2t, pos, zero_rows, nrow, tm):
    ntok = pos.shape[0] // 2
    nstep = ntok // tm
    nz = zero_rows.shape[0] // nstep
    grid_spec = pltpu.PrefetchScalarGridSpec(
        num_scalar_prefetch=2,
        grid=(nstep,),
        in_specs=[pl.BlockSpec((tm * ROW_TILE, 128), lambda i, p, z: (i, 0))],
        out_specs=pl.BlockSpec(memory_space=pl.ANY),
        scratch_shapes=[pltpu.VMEM((ROW_TILE, 128), F32), pltpu.SemaphoreType.DMA(())],
    )
    return pl.pallas_call(
        functools.partial(_dispatch_kernel, ntok=ntok, tm=tm, nz=nz),
        grid_spec=grid_spec,
        out_shape=jax.ShapeDtypeStruct((nrow * ROW_TILE, 128), F32),
        compiler_params=_cparams(("arbitrary",)),
        name="moe_dispatch",
    )(pos, zero_rows, h2t)


def _ffn_kernel(te_ref, nu_ref, xs_ref, wup_ref, wdn_ref, ys_ref, *, tm):
    del te_ref
    i = pl.program_id(0)

    @pl.when(i < nu_ref[0])
    def _():
        x = jnp.concatenate([xs_ref[pl.ds(s, tm, stride=ROW_TILE), :] for s in range(ROW_TILE)], axis=1)
        a = jnp.dot(x.astype(BF16), wup_ref[0], preferred_element_type=F32)
        g = a[:, :MOE_FF]
        hid = g * _sigmoid(g) * a[:, MOE_FF:]
        y = jnp.dot(hid.astype(BF16), wdn_ref[0], preferred_element_type=F32)
        for s in range(ROW_TILE):
            ys_ref[pl.ds(s, tm, stride=ROW_TILE), :] = y[:, s * 128:(s + 1) * 128]

    @pl.when(i >= nu_ref[0])
    def _():
        ys_ref[...] = jnp.zeros_like(ys_ref)


def _ffn_call(xs, tile_expert, n_used, w_up, w_down):
    tm = MOE_TM
    ntile = tile_expert.shape[0]
    grid_spec = pltpu.PrefetchScalarGridSpec(
        num_scalar_prefetch=2,
        grid=(ntile,),
        in_specs=[
            pl.BlockSpec((tm * ROW_TILE, 128), lambda i, te, nu: (i, 0)),
            pl.BlockSpec((1, D_MODEL, 2 * MOE_FF), lambda i, te, nu: (te[i], 0, 0)),
            pl.BlockSpec((1, MOE_FF, D_MODEL), lambda i, te, nu: (te[i], 0, 0)),
        ],
        out_specs=pl.BlockSpec((tm * ROW_TILE, 128), lambda i, te, nu: (i, 0)),
    )
    return pl.pallas_call(
        functools.partial(_ffn_kernel, tm=tm),
        grid_spec=grid_spec,
        out_shape=jax.ShapeDtypeStruct(xs.shape, F32),
        compiler_params=_cparams(("arbitrary",)),
        name="moe_ffn",
    )(tile_expert, n_used, xs, w_up, w_down)


def _final_kernel(pos_ref, x1_ref, route_ref, ada_ref, lng_ref, lnb_ref, ys_ref, o_ref, buf, sem,
                  *, ntok, tm, nstep):
    i = pl.program_id(0)
    slot = i % 2

    def fetch(step, to_slot):
        def issue(r, carry):
            for k in range(2):
                row = pos_ref[k * ntok + step * tm + r]
                src = ys_ref.at[pl.ds(pl.multiple_of(row * ROW_TILE, ROW_TILE), ROW_TILE)]
                dst = buf.at[to_slot, k, pl.ds(pl.multiple_of(r * ROW_TILE, ROW_TILE), ROW_TILE)]
                pltpu.make_async_copy(src, dst, sem.at[to_slot]).start(priority=k)
            return carry

        lax.fori_loop(0, tm, issue, 0, unroll=ISSUE_UNROLL)

    @pl.when(i == 0)
    def _():
        fetch(0, 0)

    @pl.when(i + 1 < nstep)
    def _():
        fetch(i + 1, 1 - slot)

    for k in range(2):
        pltpu.make_async_copy(ys_ref.at[pl.ds(0, tm * ROW_TILE)], buf.at[slot, k], sem.at[slot]).wait()
    ada = ada_ref[0]
    route = route_ref[0]
    w0, w1 = route[:, 2:3], route[:, 3:4]
    ffn = jnp.concatenate(
        [w0 * buf[slot, 0, pl.ds(s, tm, stride=ROW_TILE), :] + w1 * buf[slot, 1, pl.ds(s, tm, stride=ROW_TILE), :]
         for s in range(ROW_TILE)], axis=1)
    o_ref[0] = _ln(ALPHA * x1_ref[0] + (1.0 + ada[5:6]) * ffn) * lng_ref[...] + lnb_ref[...]


def _final_call(x1, ys, pos, route, ada8, ln_g, ln_b, tm):
    bsz, seq, _ = x1.shape
    nt = seq // tm
    nstep = bsz * nt
    grid_spec = pltpu.PrefetchScalarGridSpec(
        num_scalar_prefetch=1,
        grid=(nstep,),
        in_specs=[
            pl.BlockSpec((1, tm, D_MODEL), lambda i, p: (i // nt, i % nt, 0)),
            pl.BlockSpec((1, tm, 128), lambda i, p: (i // nt, i % nt, 0)),
            pl.BlockSpec((1, 8, D_MODEL), lambda i, p: (i // nt, 0, 0)),
            pl.BlockSpec((1, D_MODEL), lambda i, p: (0, 0)),
            pl.BlockSpec((1, D_MODEL), lambda i, p: (0, 0)),
            pl.BlockSpec(memory_space=pl.ANY),
        ],
        out_specs=pl.BlockSpec((1, tm, D_MODEL), lambda i, p: (i // nt, i % nt, 0)),
        scratch_shapes=[pltpu.VMEM((2, 2, tm * ROW_TILE, 128), F32), pltpu.SemaphoreType.DMA((2,))],
    )
    return pl.pallas_call(
        functools.partial(_final_kernel, ntok=bsz * seq, tm=tm, nstep=nstep),
        grid_spec=grid_spec,
        out_shape=jax.ShapeDtypeStruct((bsz, seq, D_MODEL), F32),
        compiler_params=_cparams(("arbitrary",)),
        name="final",
    )(pos, x1, route, ada8, ln_g, ln_b, ys)


def _split_w_in(w_in):
    sizes = (768, 768, 768, 512, 128, 128, 256, 256, 256, 256, N_BRANCH * D_MODEL)
    pts = [0]
    for s in sizes:
        pts.append(pts[-1] + s)
    return [w_in[:, pts[n]:pts[n + 1]] for n in range(len(sizes))]


def _hi_lo(w):
    hi = w.astype(BF16)
    return jnp.stack([hi, (w - hi.astype(F32)).astype(BF16)])


def _layer_weights(p, l):
    a_q, a_k, a_v, b_q, b_k, b_v, c_q, c_k, c_v, d_u, w_gate = _split_w_in(p["w_in"][l])
    scale = HEAD_DIM ** -0.5
    def a_cols(g):
        gs = slice(g * A_WIDTH, (g + 1) * A_WIDTH)
        return [a_q[:, gs] * scale, a_k[:, gs], a_v[:, gs]]

    cols = [b_q * scale] + a_cols(0) + [b_k, b_v, c_q * scale, c_k, c_v, d_u] + a_cols(1) + a_cols(2)
    w_route = jnp.concatenate([p["w_route_g"][l], p["w_route_e"][l]], axis=1)
    b_route = jnp.concatenate([p["b_route_g"][l], p["b_route_e"][l]])
    npad = 128 - w_route.shape[1]
    return {
        "w_qkv": jnp.concatenate(cols, axis=1).astype(BF16),
        "w_gate": w_gate.astype(BF16),
        "b_gate": p["b_gate"][l].reshape(1, -1),
        "w_branch": p["w_branch"][l].astype(BF16),
        "w_out": p["w_out"][l].astype(BF16),
        "w_glu": p["w_glu"][l].astype(BF16),
        "b_glu": p["b_glu"][l].reshape(1, -1),
        "ln1_g": p["ln1_g"][l].reshape(1, -1),
        "ln1_b": p["ln1_b"][l].reshape(1, -1),
        "ln2_g": p["ln2_g"][l].reshape(1, -1),
        "ln2_b": p["ln2_b"][l].reshape(1, -1),
        "w_route": _hi_lo(jnp.pad(w_route, ((0, 0), (0, npad)))),
        "b_route": jnp.pad(b_route, (0, npad)).reshape(1, -1),
        "w_up": p["w_up"][l].astype(BF16),
        "w_down": p["w_down"][l].astype(BF16),
        "b_sink": p["b_sink"][l],
        "c_bias": _c_bias_table(p["c_rpb"][l]),
    }


def _layer(x, ada8, lw, ssm_prm):
    bsz, seq, _ = x.shape
    tm, tm_mm = 256, 512
    h, main, fold1, fold2 = _proj_call(x, ada8, lw["w_qkv"], tm_mm)
    oa, la = zip(*[_attn_a_call(f, g, seq) for g, f in enumerate((main, fold1, fold2))])
    yb = _attn_b_call(main, lw["b_sink"])
    yc = _attn_c_call(main, lw["c_bias"])
    s5 = _ssm_call(main, ssm_prm)
    x1, h2t, route = _merge_call(x, h, ada8, oa, la, yb, yc, s5, lw, tm_mm)
    ntok = bsz * seq
    eid = route.reshape(ntok, 128)[:, :2].astype(jnp.int32)
    pos, zero_rows, tile_expert, n_used = _moe_plan(eid, MOE_TM)
    xs = _dispatch_call(h2t, pos, zero_rows, 2 * ntok + N_EXPERTS * MOE_TM, tm)
    ys = _ffn_call(xs, tile_expert, n_used, lw["w_up"], lw["w_down"])
    return _final_call(x1, ys, pos, route, ada8, lw["ln2_g"], lw["ln2_b"], tm)


def kernel(x_prompt, x_sample, c_prompt, c_sample, w_ada, b_ada, w_in, b_gate, b_sink, c_rpb, lam_re, lam_im,
           log_dt, ssm_b_re, ssm_b_im, ssm_c_re, ssm_c_im, ssm_d, w_glu, b_glu, w_branch, w_out, ln1_g, ln1_b,
           ln2_g, ln2_b, w_route_g, b_route_g, w_route_e, b_route_e, w_up, w_down):
    p = dict(w_in=w_in, b_gate=b_gate, b_sink=b_sink, c_rpb=c_rpb, w_glu=w_glu, b_glu=b_glu, w_branch=w_branch,
             w_out=w_out, ln1_g=ln1_g, ln1_b=ln1_b, ln2_g=ln2_g, ln2_b=ln2_b, w_route_g=w_route_g,
             b_route_g=b_route_g, w_route_e=w_route_e, b_route_e=b_route_e, w_up=w_up, w_down=w_down)
    nbp, nbs = c_prompt.shape[0], c_sample.shape[0]
    c_all = jnp.concatenate([c_prompt, c_sample], axis=0)
    c_all = jnp.pad(c_all, ((0, -c_all.shape[0] % 8), (0, 0)))
    ada = _ada_call(c_all, w_ada, b_ada).reshape(DEPTH, c_all.shape[0], 6, D_MODEL)
    ada = jnp.pad(ada, ((0, 0), (0, 0), (0, 2), (0, 0)))
    m_mat, bend, cin, apow = _ssm_param_call(lam_re, lam_im, log_dt, ssm_b_re, ssm_b_im, ssm_c_re, ssm_c_im, ssm_d)
    xs = [x_prompt, x_sample]
    for l in range(DEPTH):
        lw = _layer_weights(p, l)
        prm = (m_mat[l * SSM_GROUPS:(l + 1) * SSM_GROUPS], bend[2 * l:2 * l + 2], cin[2 * l:2 * l + 2],
               apow[2 * l:2 * l + 2])
        xs = [_layer(xs[0], ada[l, :nbp], lw, prm), _layer(xs[1], ada[l, nbp:nbp + nbs], lw, prm)]
    return (xs[0], xs[1])
```

```python
import functools
import math

import jax
import jax.numpy as jnp
from jax import lax
from jax.experimental import pallas as pl
from jax.experimental.pallas import tpu as pltpu

F32 = jnp.float32
BF16 = jnp.bfloat16
HIGHEST = lax.Precision.HIGHEST

D_MODEL = 1024
DEPTH = 2
HEAD_DIM = 64
A_PAIRS = ((128, 1), (512, 4), (2048, 16))
A_GROUPS = 3
A_HEADS = 4
A_WIDTH = A_HEADS * HEAD_DIM
B_Q_HEADS = 8
B_KV_HEADS = 2
B_WINDOW = 128
B_WIDTH = B_Q_HEADS * HEAD_DIM
GRID_W = 64
C_HEADS = 4
C_WIN_R = 8
C_WIN_C = 16
C_WIDTH = C_HEADS * HEAD_DIM
D_WIDTH = 256
SSM_GROUP = 16
SSM_GROUPS = D_WIDTH // SSM_GROUP
SSM_STATE = 64
N_BRANCH = 4
MOE_GROUPS = 4
MOE_EPG = 8
N_EXPERTS = MOE_GROUPS * MOE_EPG
MOE_FF = 512
ALPHA = (2 * DEPTH) ** 0.25
LN_EPS = 1e-5
NEG = -1e30

QKV_COLS = 4096
MAIN_COLS = 2560
COL_BKV = 3
COL_BQ = 2
COL_CQ = 6
COL_DU = 9
A_QKV = 3 * A_WIDTH
A_STEP_QUERIES = 2048
ROW_TILE = 8
A_HALF = 64
ATTN_SQ = 128
ATTN_TQ = 1024
B_INTERLEAVE = 8
MERGE_COLS = 256
SSM_CHUNK = 64
SSM_CW = SSM_CHUNK * SSM_GROUP
SSM_SCAN_BATCH = 4
MOE_TM = 256
ISSUE_UNROLL = 8
VMEM_LIMIT = 56 * 1024 * 1024


def _cparams(sem):
    return pltpu.CompilerParams(dimension_semantics=sem, vmem_limit_bytes=VMEM_LIMIT)


def _sigmoid(x):
    return 1.0 / (1.0 + jnp.exp(-x))


def _ln(x):
    mu = jnp.mean(x, axis=-1, keepdims=True)
    xc = x - mu
    var = jnp.mean(xc * xc, axis=-1, keepdims=True)
    return xc * lax.rsqrt(var + LN_EPS)


def _dot_nt(a, b, **kw):
    return lax.dot_general(a, b, (((1,), (1,)), ((), ())), preferred_element_type=F32, **kw)


def _split_head_pairs(q):
    lane = lax.broadcasted_iota(jnp.int32, q.shape, 1)
    even = (lane % 128) < HEAD_DIM
    zero = jnp.zeros_like(q)
    return jnp.where(even, q, zero), jnp.where(even, zero, q)


def _alibi(n):
    return [2.0 ** (-8.0 * (i + 1) / n) for i in range(n)]


def _ada_kernel(c_ref, w_ref, b_ref, o_ref):
    c = c_ref[...]
    sc = c * _sigmoid(c)
    o_ref[0] = jnp.dot(sc, w_ref[0], precision=HIGHEST, preferred_element_type=F32) + b_ref[0]


def _ada_call(c_all, w_ada, b_ada):
    nb = c_all.shape[0]
    return pl.pallas_call(
        _ada_kernel,
        grid=(DEPTH, 6),
        in_specs=[
            pl.BlockSpec((nb, D_MODEL), lambda l, j: (0, 0)),
            pl.BlockSpec((1, D_MODEL, D_MODEL), lambda l, j: (l, 0, j)),
            pl.BlockSpec((1, 1, D_MODEL), lambda l, j: (l, 0, j)),
        ],
        out_specs=pl.BlockSpec((1, nb, D_MODEL), lambda l, j: (l, 0, j)),
        out_shape=jax.ShapeDtypeStruct((DEPTH, nb, 6 * D_MODEL), F32),
        compiler_params=_cparams(("arbitrary", "arbitrary")),
        name="ada",
    )(c_all, w_ada, b_ada.reshape(DEPTH, 1, 6 * D_MODEL))


def _proj_kernel(x_ref, ada_ref, w_ref, h_ref, main_ref, f1_ref, f2_ref, scr_ref, *, chunk):
    x = x_ref[0]
    ada = ada_ref[0]
    tm = x.shape[0]
    h = _ln(x) * (1.0 + ada[1:2]) + ada[0:1]
    hb = h.astype(BF16)
    h_ref[0] = hb
    for j in range(MAIN_COLS // chunk):
        sl = slice(j * chunk, (j + 1) * chunk)
        main_ref[0, :, sl] = jnp.dot(hb, w_ref[:, sl], preferred_element_type=F32).astype(BF16)
    for g, f_ref in ((1, f1_ref), (2, f2_ref)):
        dil = A_PAIRS[g][1]
        c0 = MAIN_COLS + (g - 1) * A_QKV
        res = jnp.dot(hb, w_ref[:, c0:c0 + A_QKV], preferred_element_type=F32)
        for c in range(A_QKV // 128):
            scr_ref[c] = res[:, c * 128:(c + 1) * 128]
        for r in range(dil):
            for c in range(A_QKV // 128):
                lo = r * A_QKV + c * 128
                f_ref[0, :, lo:lo + 128] = scr_ref[c, pl.ds(r, tm // dil, stride=dil), :].astype(BF16)


def _proj_call(x, ada8, w_qkv, tm):
    bsz, seq, _ = x.shape
    d1, d2 = A_PAIRS[1][1], A_PAIRS[2][1]
    return pl.pallas_call(
        functools.partial(_proj_kernel, chunk=512),
        grid=(bsz, seq // tm),
        in_specs=[
            pl.BlockSpec((1, tm, D_MODEL), lambda b, i: (b, i, 0)),
            pl.BlockSpec((1, 8, D_MODEL), lambda b, i: (b, 0, 0)),
            pl.BlockSpec((D_MODEL, QKV_COLS), lambda b, i: (0, 0), pipeline_mode=pl.Buffered(1)),
        ],
        out_specs=[
            pl.BlockSpec((1, tm, D_MODEL), lambda b, i: (b, i, 0)),
            pl.BlockSpec((1, tm, MAIN_COLS), lambda b, i: (b, i, 0)),
            pl.BlockSpec((1, tm // d1, d1 * A_QKV), lambda b, i: (b, i, 0)),
            pl.BlockSpec((1, tm // d2, d2 * A_QKV), lambda b, i: (b, i, 0)),
        ],
        out_shape=[
            jax.ShapeDtypeStruct((bsz, seq, D_MODEL), BF16),
            jax.ShapeDtypeStruct((bsz, seq, MAIN_COLS), BF16),
            jax.ShapeDtypeStruct((bsz, seq // d1, d1 * A_QKV), BF16),
            jax.ShapeDtypeStruct((bsz, seq // d2, d2 * A_QKV), BF16),
        ],
        scratch_shapes=[pltpu.VMEM((A_QKV // 128, tm, 128), F32)],
        compiler_params=_cparams(("parallel", "parallel")),
        name="proj",
    )(x, ada8, w_qkv)


def _attn_a_kernel(cur_ref, prev_ref, nxt_ref, o_ref, lse_ref, *, tq, sub, slopes, dil, nres):
    for res in range(nres):
        c0 = res * A_QKV

        def cols(ref, n):
            return ref[0, :, c0 + n * A_WIDTH:c0 + (n + 1) * A_WIDTH]

        k = jnp.concatenate([cols(prev_ref, 1), cols(cur_ref, 1), cols(nxt_ref, 1)], axis=0)
        v = jnp.concatenate([cols(prev_ref, 2), cols(cur_ref, 2), cols(nxt_ref, 2)], axis=0)
        _attn_a_residue(cols(cur_ref, 0), k, v, o_ref, lse_ref, res * A_WIDTH,
                        tq=tq, sub=sub, slopes=slopes, dil=dil)


def _attn_a_residue(q, k, v, o_ref, lse_ref, out_col, *, tq, sub, slopes, dil):
    i = pl.program_id(2)
    sq = ATTN_SQ
    kw = sq + 2 * A_HALF
    row = lax.broadcasted_iota(jnp.int32, (sq, kw), 0)
    col = lax.broadcasted_iota(jnp.int32, (sq, kw), 1)
    rel = col - A_HALF - row
    dist = jnp.abs(rel).astype(F32) * float(dil)
    even_q, odd_q = _split_head_pairs(q)
    even_lanes = (lax.broadcasted_iota(jnp.int32, (sq, 128), 1) < HEAD_DIM)
    for j in range(tq // sq):
        kpos = i * tq + j * sq - A_HALF + col
        valid = (jnp.abs(rel) <= A_HALF) & (kpos >= 0) & (kpos < sub)
        rows = slice(j * sq, (j + 1) * sq)
        kj = k[j * sq:j * sq + kw]
        vj = v[j * sq:j * sq + kw]
        hs = range(A_HEADS)
        pss = [slice((h // 2) * 128, (h // 2 + 1) * 128) for h in hs]
        ss = [_dot_nt((even_q, odd_q)[h % 2][rows, ps], kj[:, ps]) for h, ps in zip(hs, pss)]
        ss = [jnp.where(valid, s - slopes[h] * dist, NEG) for s, h in zip(ss, hs)]
        ms = [jnp.max(s, axis=-1, keepdims=True) for s in ss]
        pp = [jnp.exp(s - m) for s, m in zip(ss, ms)]
        dens = [jnp.sum(p, axis=-1, keepdims=True) for p in pp]
        outs = [jnp.dot(p.astype(BF16), vj[:, ps], preferred_element_type=F32) / den
                for p, ps, den in zip(pp, pss, dens)]
        lses = [m + jnp.log(den) for m, den in zip(ms, dens)]
        for n in range(0, A_HEADS, 2):
            ocols = slice(out_col + n // 2 * 128, out_col + (n // 2 + 1) * 128)
            o_ref[0, rows, ocols] = jnp.where(even_lanes, outs[n], outs[n + 1]).astype(BF16)
            lse_ref[0, rows, ocols] = jnp.where(even_lanes, lses[n], lses[n + 1])


def _attn_a_call(folded, g, seq):
    bsz = folded.shape[0]
    _, dil = A_PAIRS[g]
    sub = seq // dil
    tq = min(ATTN_TQ, sub)
    nq = sub // tq
    hb = tq // A_HALF
    nhb = sub // A_HALF
    nres = max(1, min(dil, A_STEP_QUERIES // tq))
    slopes = _alibi(A_GROUPS * A_HEADS)[g * A_HEADS:(g + 1) * A_HEADS]
    wide = nres * A_QKV

    return pl.pallas_call(
        functools.partial(_attn_a_kernel, tq=tq, sub=sub, slopes=slopes, dil=dil, nres=nres),
        grid=(bsz, dil // nres, nq),
        in_specs=[
            pl.BlockSpec((1, tq, wide), lambda b, r, i: (b, i, r)),
            pl.BlockSpec((1, A_HALF, wide), lambda b, r, i: (b, jnp.maximum(i * hb - 1, 0), r)),
            pl.BlockSpec((1, A_HALF, wide), lambda b, r, i: (b, jnp.minimum((i + 1) * hb, nhb - 1), r)),
        ],
        out_specs=[
            pl.BlockSpec((1, tq, nres * A_WIDTH), lambda b, r, i: (b, i, r)),
            pl.BlockSpec((1, tq, nres * A_WIDTH), lambda b, r, i: (b, i, r)),
        ],
        out_shape=[
            jax.ShapeDtypeStruct((bsz, sub, dil * A_WIDTH), BF16),
            jax.ShapeDtypeStruct((bsz, sub, dil * A_WIDTH), F32),
        ],
        compiler_params=_cparams(("parallel", "parallel", "parallel")),
        name=f"attn_a{g}",
    )(folded, folded, folded)


def _attn_b_kernel(sink_ref, q_ref, kvc_ref, kvp_ref, kvn_ref, o_ref, *, tq, seq, slopes):
    i = pl.program_id(1)
    q = q_ref[0]
    kv = jnp.concatenate([kvp_ref[0], kvc_ref[0], kvn_ref[0]], axis=0)
    sq = ATTN_SQ
    kw = sq + 2 * B_WINDOW
    row = lax.broadcasted_iota(jnp.int32, (sq, kw), 0)
    col = lax.broadcasted_iota(jnp.int32, (sq, kw), 1)
    rel = col - B_WINDOW - row
    dist = jnp.abs(rel).astype(F32)
    rep = B_Q_HEADS // B_KV_HEADS
    even_q, odd_q = _split_head_pairs(q)
    even_lanes = (lax.broadcasted_iota(jnp.int32, (sq, 128), 1) < HEAD_DIM)

    def doubled(c):
        piece = kv[:, c * HEAD_DIM:(c + 1) * HEAD_DIM]
        return jnp.concatenate([piece, piece], axis=1)

    k2 = [doubled(g) for g in range(B_KV_HEADS)]
    v2 = [doubled(B_KV_HEADS + g) for g in range(B_KV_HEADS)]
    for j in range(tq // sq):
        kpos = i * tq + j * sq - B_WINDOW + col
        valid = (jnp.abs(rel) <= B_WINDOW) & (kpos >= 0) & (kpos < seq)
        rows = slice(j * sq, (j + 1) * sq)
        for h0 in range(0, B_Q_HEADS, B_INTERLEAVE):
            hs = range(h0, h0 + B_INTERLEAVE)
            qms = [(even_q, odd_q)[h % 2][rows, (h // 2) * 128:(h // 2 + 1) * 128] for h in hs]
            kjs = [k2[h // rep][j * sq:j * sq + kw] for h in hs]
            vjs = [v2[h // rep][j * sq:j * sq + kw] for h in hs]
            sks = [sink_ref[h] for h in hs]
            ss = [_dot_nt(qm, kj) for qm, kj in zip(qms, kjs)]
            ss = [jnp.where(valid, s - slopes[h] * dist, NEG) for s, h in zip(ss, hs)]
            ms = [jnp.maximum(jnp.max(s, axis=-1, keepdims=True), sk) for s, sk in zip(ss, sks)]
            pp = [jnp.exp(s - m) for s, m in zip(ss, ms)]
            dens = [jnp.sum(p, axis=-1, keepdims=True) + jnp.exp(sk - m) for p, sk, m in zip(pp, sks, ms)]
            outs = [jnp.dot(p.astype(BF16), vj, preferred_element_type=F32) / den
                    for p, vj, den in zip(pp, vjs, dens)]
            for n in range(0, B_INTERLEAVE, 2):
                ps = slice((h0 + n) // 2 * 128, ((h0 + n) // 2 + 1) * 128)
                o_ref[0, rows, ps] = jnp.where(even_lanes, outs[n], outs[n + 1]).astype(BF16)


def _attn_b_call(qkv, sink):
    bsz, seq, _ = qkv.shape
    tq = min(ATTN_TQ, seq)
    hb = tq // B_WINDOW
    nhb = seq // B_WINDOW
    return pl.pallas_call(
        functools.partial(_attn_b_kernel, tq=tq, seq=seq, slopes=_alibi(B_Q_HEADS)),
        grid=(bsz, seq // tq),
        in_specs=[
            pl.BlockSpec(memory_space=pltpu.SMEM),
            pl.BlockSpec((1, tq, B_WIDTH), lambda b, i: (b, i, COL_BQ)),
            pl.BlockSpec((1, tq, 256), lambda b, i: (b, i, COL_BKV)),
            pl.BlockSpec((1, B_WINDOW, 256), lambda b, i: (b, jnp.maximum(i * hb - 1, 0), COL_BKV)),
            pl.BlockSpec((1, B_WINDOW, 256), lambda b, i: (b, jnp.minimum((i + 1) * hb, nhb - 1), COL_BKV)),
        ],
        out_specs=pl.BlockSpec((1, tq, B_WIDTH), lambda b, i: (b, i, 0)),
        out_shape=jax.ShapeDtypeStruct((bsz, seq, B_WIDTH), BF16),
        compiler_params=_cparams(("parallel", "parallel")),
        name="attn_b",
    )(sink, qkv, qkv, qkv, qkv)


C_ROWS_PER_STEP = 16
C_INTERLEAVE_ROWS = 4


def _attn_c_kernel(q_ref, k_ref, v_ref, bias_ref, o_ref, *, rows):
    i = pl.program_id(1)
    kwin = C_WIN_R * GRID_W
    even_q, odd_q = _split_head_pairs(q_ref[0])
    even_lanes = (lax.broadcasted_iota(jnp.int32, (GRID_W, 128), 1) < HEAD_DIM)
    for jr0 in range(0, C_ROWS_PER_STEP, C_INTERLEAVE_ROWS):
        units = []
        for jr in range(jr0, jr0 + C_INTERLEAVE_ROWS):
            r = i * C_ROWS_PER_STEP + jr
            rs = jnp.clip(r - C_WIN_R // 2, 0, rows - C_WIN_R)
            start = pl.multiple_of(rs * GRID_W, GRID_W)
            kw = k_ref[0, pl.ds(start, kwin), :]
            vw = v_ref[0, pl.ds(start, kwin), :]
            for h in range(C_HEADS):
                units.append((slice(jr * GRID_W, (jr + 1) * GRID_W), h, r - rs, kw, vw))
        pss = [slice((h // 2) * 128, (h // 2 + 1) * 128) for _, h, _, _, _ in units]
        ss = [_dot_nt((even_q, odd_q)[h % 2][qrows, ps], kw[:, ps]) + bias_ref[h, delta]
              for (qrows, h, delta, kw, _), ps in zip(units, pss)]
        ms = [jnp.max(s, axis=-1, keepdims=True) for s in ss]
        pp = [jnp.exp(s - m) for s, m in zip(ss, ms)]
        dens = [jnp.sum(p, axis=-1, keepdims=True) for p in pp]
        outs = [jnp.dot(p.astype(BF16), vw[:, ps], preferred_element_type=F32) / den
                for p, (_, _, _, _, vw), ps, den in zip(pp, units, pss, dens)]
        for n in range(0, len(units), 2):
            o_ref[0, units[n][0], pss[n]] = jnp.where(even_lanes, outs[n], outs[n + 1]).astype(BF16)


def _c_bias_table(rpb):
    qc = jnp.arange(GRID_W)[:, None]
    kc = jnp.arange(GRID_W)[None, :]
    col_off = jnp.clip(kc - qc + (C_WIN_C - 1), 0, 2 * C_WIN_C - 2)
    sel = (col_off[None] == jnp.arange(2 * C_WIN_C - 1)[:, None, None]).astype(F32)
    band = jnp.einsum("hrc,cqk->hrqk", rpb.astype(F32), sel, precision=HIGHEST)
    cs = jnp.clip(qc - C_WIN_C // 2, 0, GRID_W - C_WIN_C)
    valid = (kc >= cs) & (kc < cs + C_WIN_C)
    band = jnp.where(valid[None, None], band, NEG)
    tabs = [band[:, C_WIN_R - 1 - d:2 * C_WIN_R - 1 - d].transpose(0, 2, 1, 3) for d in range(C_WIN_R)]
    return jnp.stack(tabs, axis=1).reshape(C_HEADS, C_WIN_R, GRID_W, C_WIN_R * GRID_W)


def _attn_c_call(qkv, bias_tab):
    bsz, seq, _ = qkv.shape
    rows = seq // GRID_W
    tq = C_ROWS_PER_STEP * GRID_W
    return pl.pallas_call(
        functools.partial(_attn_c_kernel, rows=rows),
        grid=(bsz, seq // tq),
        in_specs=[
            pl.BlockSpec((1, tq, C_WIDTH), lambda b, i: (b, i, COL_CQ)),
            pl.BlockSpec((1, seq, C_WIDTH), lambda b, i: (b, 0, COL_CQ + 1)),
            pl.BlockSpec((1, seq, C_WIDTH), lambda b, i: (b, 0, COL_CQ + 2)),
            pl.BlockSpec(bias_tab.shape, lambda b, i: (0, 0, 0, 0)),
        ],
        out_specs=pl.BlockSpec((1, tq, C_WIDTH), lambda b, i: (b, i, 0)),
        out_shape=jax.ShapeDtypeStruct((bsz, seq, C_WIDTH), BF16),
        compiler_params=_cparams(("parallel", "arbitrary")),
        name="attn_c",
    )(qkv, qkv, qkv, bias_tab)


def _cmul(ar, ai, br, bi):
    return ar * br - ai * bi, ar * bi + ai * br


def _ssm_param_kernel(lre_ref, lim_ref, ldt_ref, cre_ref, cim_ref, bre_ref, bim_ref, d_ref,
                      m_ref, bend_ref, cin_ref, apow_ref):
    tc, hh, pp = SSM_CHUNK, SSM_GROUP, SSM_STATE
    it = lax.broadcasted_iota(jnp.int32, (tc, pp), 0).astype(F32)
    lane = lax.broadcasted_iota(jnp.int32, (hh, 128), 1)
    subl = lax.broadcasted_iota(jnp.int32, (hh, 128), 0)
    tabs = []
    for d in range(2):
        lr, li = lre_ref[d, 0, 0], lim_ref[d, 0, 0]
        dt = jnp.exp(ldt_ref[d, 0, 0])

        def powers(kv):
            mag = jnp.exp(kv * (lr * dt))
            ang = kv * (li * dt)
            return mag * jnp.cos(ang), mag * jnp.sin(ang)

        one = jnp.ones((1, pp), F32)
        lbr, lbi = powers(one)
        den = lr * lr + li * li
        nr, ni = lbr - 1.0, lbi
        fr = (nr * lr + ni * li) / den
        fi = (ni * lr - nr * li) / den
        apr, api = powers(one * float(tc))
        apow_ref[d, 0] = jnp.concatenate([apr, api], axis=1)

        cr, ci = cre_ref[d, 0], cim_ref[d, 0]
        br, bi = bre_ref[d, 0], bim_ref[d, 0]

        wr, wi = powers(it + 1.0 if d == 0 else float(tc) - it)
        cwr, cwi = _cmul(cr[None], ci[None], wr[:, None, :], wi[:, None, :])
        cin = jnp.concatenate([cwr, -cwi], axis=2).reshape(tc * hh, 2 * pp)
        cin_ref[d, 0] = cin.astype(BF16)

        wr, wi = powers(float(tc - 1) - it if d == 0 else it)
        wfr, wfi = _cmul(wr, wi, fr, fi)
        er, ei = _cmul(br[None], bi[None], wfr[:, None, :], wfi[:, None, :])
        bend_ref[d, 0] = jnp.concatenate([er, ei], axis=2).reshape(tc * hh, 2 * pp).astype(BF16)

        wr, wi = powers(it if d == 0 else float(tc - 1) - it)
        wfr, wfi = _cmul(wr, wi, fr, fi)
        gr, gi = _cmul(cr[None], ci[None], wfr[:, None, :], wfi[:, None, :])
        gr = gr.reshape(tc * hh, pp)
        gi = gi.reshape(tc * hh, pp)
        tabs.append(_dot_nt(br, gr, precision=HIGHEST) - _dot_nt(bi, gi, precision=HIGHEST))

    ktf, ktb = tabs
    last = pltpu.roll(ktb[:, SSM_CW - 128:], hh, axis=1)
    diag = jnp.where(lane == subl, d_ref[0], 0.0)
    first = ktf[:, :128] + jnp.where(lane < hh, last + diag, 0.0)
    ktf = jnp.concatenate([first, ktf[:, 128:]], axis=1)
    lane_w = lax.broadcasted_iota(jnp.int32, (hh, SSM_CW), 1)
    ktb = jnp.where(lane_w < SSM_CW - hh, ktb, 0.0)
    zeros = jnp.zeros((hh, SSM_CW), F32)
    pf = jnp.concatenate([ktf, zeros], axis=1)
    pb = jnp.concatenate([ktb, zeros], axis=1)
    for j in range(tc):
        blk = pf if j == 0 else pltpu.roll(pf, j * hh, axis=1)
        sh = (tc - 1 - j) * hh
        blb = pb if sh == 0 else pltpu.roll(pb, 2 * SSM_CW - sh, axis=1)
        m_ref[0, j * hh:(j + 1) * hh, :] = (blk[:, :SSM_CW] + blb[:, :SSM_CW]).astype(BF16)


def _ssm_param_call(lam_re, lam_im, log_dt, b_re, b_im, c_re, c_im, d_skip):
    gg, pp, hh = SSM_GROUPS, SSM_STATE, SSM_GROUP
    nl = lam_re.shape[0]
    lre = lam_re.reshape(nl * 2, gg, 1, 1, pp)
    lim = lam_im.reshape(nl * 2, gg, 1, 1, pp)
    ldt = jnp.broadcast_to(log_dt.reshape(nl * 2, gg, 1, 1, 1), (nl * 2, gg, 1, 1, pp))
    cre = c_re.reshape(nl * 2, gg, hh, pp)
    cim = c_im.reshape(nl * 2, gg, hh, pp)
    bre = jnp.swapaxes(b_re, -1, -2).reshape(nl * 2, gg, hh, pp)
    bim = jnp.swapaxes(b_im, -1, -2).reshape(nl * 2, gg, hh, pp)
    dpad = jnp.pad(d_skip.reshape(nl * gg, 1, hh), ((0, 0), (0, 0), (0, 128 - hh)))

    def vec(l, g):
        return (l, g, 0, 0, 0)

    def mat(l, g):
        return (l, g, 0, 0)

    return pl.pallas_call(
        _ssm_param_kernel,
        grid=(nl, gg),
        in_specs=[
            pl.BlockSpec((2, 1, 1, 1, pp), vec), pl.BlockSpec((2, 1, 1, 1, pp), vec),
            pl.BlockSpec((2, 1, 1, 1, pp), vec),
            pl.BlockSpec((2, 1, hh, pp), mat), pl.BlockSpec((2, 1, hh, pp), mat),
            pl.BlockSpec((2, 1, hh, pp), mat), pl.BlockSpec((2, 1, hh, pp), mat),
            pl.BlockSpec((1, 1, 128), lambda l, g: (l * gg + g, 0, 0)),
        ],
        out_specs=[
            pl.BlockSpec((1, SSM_CW, SSM_CW), lambda l, g: (l * gg + g, 0, 0)),
            pl.BlockSpec((2, 1, SSM_CW, 2 * pp), mat),
            pl.BlockSpec((2, 1, SSM_CW, 2 * pp), mat),
            pl.BlockSpec((2, 1, 1, 2 * pp), mat),
        ],
        out_shape=[
            jax.ShapeDtypeStruct((nl * gg, SSM_CW, SSM_CW), BF16),
            jax.ShapeDtypeStruct((nl * 2, gg, SSM_CW, 2 * pp), BF16),
            jax.ShapeDtypeStruct((nl * 2, gg, SSM_CW, 2 * pp), BF16),
            jax.ShapeDtypeStruct((nl * 2, gg, 1, 2 * pp), F32),
        ],
        compiler_params=_cparams(("arbitrary", "arbitrary")),
        name="ssm_param",
    )(lre, lim, ldt, cre, cim, bre, bim, dpad)


def _ssm_state_kernel(u_ref, bend_ref, s_ref):
    for g in range(SSM_GROUPS):
        u = u_ref[0, g]
        for d in range(2):
            s_ref[0, d, :, g * 128:(g + 1) * 128] = jnp.dot(u, bend_ref[d, g], preferred_element_type=F32)


def _ssm_scan_kernel(s_ref, apow_ref, x_ref, *, nc, bb):
    width = SSM_GROUPS * 128
    lane = lax.broadcasted_iota(jnp.int32, (1, width), 1)
    first_half = (lane % 128) < SSM_STATE
    a_re, a_im = [], []
    for d in range(2):
        a = apow_ref[d]
        a_re.append(jnp.where(first_half, a, pltpu.roll(a, SSM_STATE, axis=1)))
        a_im.append(jnp.where(first_half, -pltpu.roll(a, width - SSM_STATE, axis=1), a))

    def body(c, xs):
        new = []
        for n, x in enumerate(xs):
            b, d = divmod(n, 2)
            cc = c if d == 0 else nc - 1 - c
            x_ref[b, d, pl.ds(cc, 1), :] = x
            x_sw_lo = pltpu.roll(x, SSM_STATE, axis=1)
            x_sw_hi = pltpu.roll(x, width - SSM_STATE, axis=1)
            x_sw = jnp.where(first_half, x_sw_hi, x_sw_lo)
            new.append(a_re[d] * x + a_im[d] * x_sw + s_ref[b, d, pl.ds(cc, 1), :])
        return tuple(new)

    lax.fori_loop(0, nc, body, tuple(jnp.zeros((1, width), F32) for _ in range(2 * bb)))


def _ssm_out_kernel(u_ref, m_ref, cin_ref, x_ref, y_ref):
    bsz, _, nc, cw = u_ref.shape
    y = jnp.dot(u_ref[:, 0].reshape(bsz * nc, cw), m_ref[0], preferred_element_type=F32)
    for d in range(2):
        y = y + _dot_nt(x_ref[:, d].reshape(bsz * nc, 128).astype(BF16), cin_ref[d, 0])
    y_ref[:, 0] = y.reshape(bsz, nc, cw).astype(BF16)


def _ssm_call(qkv, prm):
    m_mat, bend, cin, apow = prm
    bsz, seq, _ = qkv.shape
    gg, tc, hh = SSM_GROUPS, SSM_CHUNK, SSM_GROUP
    nc = seq // tc
    u = qkv[:, :, COL_DU * 256:(COL_DU + 1) * 256]
    u = u.reshape(bsz, nc, tc, gg, hh).transpose(0, 3, 1, 2, 4).reshape(bsz, gg, nc, SSM_CW)
    width = gg * 128
    bb = math.gcd(bsz, SSM_SCAN_BATCH)
    s = pl.pallas_call(
        _ssm_state_kernel,
        grid=(bsz,),
        in_specs=[
            pl.BlockSpec((1, gg, nc, SSM_CW), lambda b: (b, 0, 0, 0)),
            pl.BlockSpec((2, gg, SSM_CW, 128), lambda b: (0, 0, 0, 0)),
        ],
        out_specs=pl.BlockSpec((1, 2, nc, width), lambda b: (b, 0, 0, 0)),
        out_shape=jax.ShapeDtypeStruct((bsz, 2, nc, width), F32),
        compiler_params=_cparams(("parallel",)),
        name="ssm_state",
    )(u, bend)
    xprev = pl.pallas_call(
        functools.partial(_ssm_scan_kernel, nc=nc, bb=bb),
        grid=(bsz // bb,),
        in_specs=[
            pl.BlockSpec((bb, 2, nc, width), lambda b: (b, 0, 0, 0)),
            pl.BlockSpec((2, 1, width), lambda b: (0, 0, 0)),
        ],
        out_specs=pl.BlockSpec((bb, 2, nc, width), lambda b: (b, 0, 0, 0)),
        out_shape=jax.ShapeDtypeStruct((bsz, 2, nc, width), F32),
        compiler_params=_cparams(("parallel",)),
        name="ssm_scan",
    )(s, apow.reshape(2, 1, width))
    y = pl.pallas_call(
        _ssm_out_kernel,
        grid=(gg,),
        in_specs=[
            pl.BlockSpec((bsz, 1, nc, SSM_CW), lambda g: (0, g, 0, 0)),
            pl.BlockSpec((1, SSM_CW, SSM_CW), lambda g: (g, 0, 0)),
            pl.BlockSpec((2, 1, SSM_CW, 128), lambda g: (0, g, 0, 0)),
            pl.BlockSpec((bsz, 2, nc, 128), lambda g: (0, 0, 0, g)),
        ],
        out_specs=pl.BlockSpec((bsz, 1, nc, SSM_CW), lambda g: (0, g, 0, 0)),
        out_shape=jax.ShapeDtypeStruct((bsz, gg, nc, SSM_CW), BF16),
        compiler_params=_cparams(("arbitrary",)),
        name="ssm_out",
    )(u, m_mat, cin, xprev)
    return y.reshape(bsz, gg, nc, tc, hh).transpose(0, 2, 3, 1, 4).reshape(bsz, seq, D_WIDTH)


def _merge_kernel(x_ref, h_ref, ada_ref,
                  oa0_ref, oa1_ref, oa2_ref, la0_ref, la1_ref, la2_ref, yb_ref, yc_ref, s5_ref,
                  wg_ref, bg_ref, wbr_ref, wout_ref, wglu_ref, bglu_ref, lng_ref, lnb_ref,
                  wr_ref, br_ref,
                  x1_ref, h2_ref, route_ref, o1_scr, l1_scr, o2_scr, l2_scr):
    ada = ada_ref[0]
    hb = h_ref[0]
    tm = hb.shape[0]
    for g, (o_ref, l_ref, o_scr, l_scr) in ((1, (oa1_ref, la1_ref, o1_scr, l1_scr)),
                                            (2, (oa2_ref, la2_ref, o2_scr, l2_scr))):
        dil = A_PAIRS[g][1]
        for r in range(dil):
            for c in range(A_WIDTH // 128):
                cs = slice(r * A_WIDTH + c * 128, r * A_WIDTH + (c + 1) * 128)
                o_scr[c, pl.ds(r, tm // dil, stride=dil), :] = o_ref[0, :, cs].astype(F32)
                l_scr[c, pl.ds(r, tm // dil, stride=dil), :] = l_ref[0, :, cs]

    def unfolded(scr):
        return jnp.concatenate([scr[c] for c in range(A_WIDTH // 128)], axis=1)

    l0, l1, l2 = la0_ref[0], unfolded(l1_scr), unfolded(l2_scr)
    lm = jnp.maximum(jnp.maximum(l0, l1), l2)
    e0, e1, e2 = jnp.exp(l0 - lm), jnp.exp(l1 - lm), jnp.exp(l2 - lm)
    ya = (e0 * oa0_ref[0].astype(F32) + e1 * unfolded(o1_scr) + e2 * unfolded(o2_scr)) / (e0 + e1 + e2)
    s5 = s5_ref[0].astype(F32)
    gel = 0.5 * s5 * (1.0 + jnp.tanh(math.sqrt(2.0 / math.pi) * (s5 + 0.044715 * (s5 * s5 * s5))))
    z = jnp.dot(gel.astype(BF16), wglu_ref[...], preferred_element_type=F32) + bglu_ref[...]
    yd = z[:, :D_WIDTH] * _sigmoid(z[:, D_WIDTH:])
    branches = (ya.astype(BF16), yb_ref[0], yc_ref[0], yd.astype(BF16))
    offs = (0, A_WIDTH, A_WIDTH + B_WIDTH, A_WIDTH + B_WIDTH + C_WIDTH, A_WIDTH + B_WIDTH + C_WIDTH + D_WIDTH)
    def chunk_dots(c0):
        css = [slice(n * D_MODEL + c0, n * D_MODEL + c0 + MERGE_COLS) for n in range(N_BRANCH)]
        pre = [jnp.dot(hb, wg_ref[:, cs], preferred_element_type=F32) + bg_ref[:, cs] for cs in css]
        brs = [jnp.dot(yb, wbr_ref[offs[n]:offs[n + 1], c0:c0 + MERGE_COLS], preferred_element_type=F32)
               for n, yb in enumerate(branches)]
        return pre, brs

    merged = []
    nxt = chunk_dots(0)
    for c0 in range(0, D_MODEL, MERGE_COLS):
        pre, brs = nxt
        if c0 + MERGE_COLS < D_MODEL:
            nxt = chunk_dots(c0 + MERGE_COLS)
        terms = [_sigmoid(g) * b for g, b in zip(pre, brs)]
        merged.append(((terms[0] + terms[1]) + terms[2]) + terms[3])
    merged = jnp.concatenate(merged, axis=1)
    mix = jnp.dot(merged.astype(BF16), wout_ref[...], preferred_element_type=F32)
    x1 = _ln(ALPHA * x_ref[0] + (1.0 + ada[2:3]) * mix) * lng_ref[...] + lnb_ref[...]
    x1_ref[0] = x1
    h2 = _ln(x1) * (1.0 + ada[4:5]) + ada[3:4]
    for s in range(ROW_TILE):
        h2_ref[pl.ds(s, tm, stride=ROW_TILE), :] = h2[:, s * 128:(s + 1) * 128]
    h2_hi = h2.astype(BF16)
    h2_lo = (h2 - h2_hi.astype(F32)).astype(BF16)
    logit = (jnp.dot(h2_hi, wr_ref[0], preferred_element_type=F32)
             + (jnp.dot(h2_lo, wr_ref[0], preferred_element_type=F32)
                + jnp.dot(h2_hi, wr_ref[1], preferred_element_type=F32))) + br_ref[...]
    lane = lax.broadcasted_iota(jnp.int32, logit.shape, 1)
    ninf = -jnp.inf
    gmask = lane < MOE_GROUPS
    gl = jnp.where(gmask, logit, ninf)
    gmax = jnp.max(gl, axis=-1, keepdims=True)
    gsel = jnp.min(jnp.where(gl == gmax, lane, 1 << 20), axis=-1, keepdims=True)
    gw = 1.0 / jnp.sum(jnp.exp(gl - gmax), axis=-1, keepdims=True)
    lo = MOE_GROUPS + gsel * MOE_EPG
    emask = (lane >= lo) & (lane < lo + MOE_EPG)
    el = jnp.where(emask, logit, ninf)
    v1 = jnp.max(el, axis=-1, keepdims=True)
    i1 = jnp.min(jnp.where(el == v1, lane, 1 << 20), axis=-1, keepdims=True)
    el2 = jnp.where(lane == i1, ninf, el)
    v2 = jnp.max(el2, axis=-1, keepdims=True)
    i2 = jnp.min(jnp.where(el2 == v2, lane, 1 << 20), axis=-1, keepdims=True)
    t = jnp.exp(v2 - v1)
    w1 = gw / (1.0 + t)
    w2 = gw * t / (1.0 + t)
    route = jnp.where(lane == 0, (i1 - MOE_GROUPS).astype(F32),
                      jnp.where(lane == 1, (i2 - MOE_GROUPS).astype(F32),
                                jnp.where(lane == 2, w1, jnp.where(lane == 3, w2, 0.0))))
    route_ref[0] = route


def _merge_call(x, h, ada8, oa, la, yb, yc, s5, lw, tm):
    bsz, seq, _ = x.shape
    nt = seq // tm

    def tok(width):
        return pl.BlockSpec((1, tm, width), lambda b, i: (b, i, 0))

    def fold(g):
        dil = A_PAIRS[g][1]
        return pl.BlockSpec((1, tm // dil, dil * A_WIDTH), lambda b, i: (b, i, 0))

    def full(arr):
        return pl.BlockSpec(arr.shape, lambda b, i: (0,) * arr.ndim, pipeline_mode=pl.Buffered(1))

    weights = (lw["w_gate"], lw["b_gate"], lw["w_branch"], lw["w_out"], lw["w_glu"], lw["b_glu"],
               lw["ln1_g"], lw["ln1_b"], lw["w_route"], lw["b_route"])
    return pl.pallas_call(
        _merge_kernel,
        grid=(bsz, nt),
        in_specs=[tok(D_MODEL), tok(D_MODEL), pl.BlockSpec((1, 8, D_MODEL), lambda b, i: (b, 0, 0)),
                  tok(A_WIDTH), fold(1), fold(2), tok(A_WIDTH), fold(1), fold(2),
                  tok(B_WIDTH), tok(C_WIDTH), tok(D_WIDTH)] + [full(w) for w in weights],
        out_specs=[tok(D_MODEL), pl.BlockSpec((tm * ROW_TILE, 128), lambda b, i: (b * nt + i, 0)), tok(128)],
        out_shape=[
            jax.ShapeDtypeStruct((bsz, seq, D_MODEL), F32),
            jax.ShapeDtypeStruct((bsz * seq * ROW_TILE, 128), F32),
            jax.ShapeDtypeStruct((bsz, seq, 128), F32),
        ],
        scratch_shapes=[pltpu.VMEM((A_WIDTH // 128, tm, 128), F32)] * 4,
        compiler_params=_cparams(("parallel", "parallel")),
        name="merge",
    )(x, h, ada8, oa[0], oa[1], oa[2], la[0], la[1], la[2], yb, yc, s5, *weights)


def _moe_plan(eid, tm):
    ntok = eid.shape[0]
    npair = 2 * ntok
    blk = 256
    e_flat = jnp.concatenate([eid[:, 0], eid[:, 1]])
    experts = jnp.arange(N_EXPERTS, dtype=jnp.int32)
    oh = (e_flat[:, None] == experts[None, :]).astype(F32).reshape(npair // blk, blk, N_EXPERTS)
    tri = (jnp.arange(blk)[:, None] >= jnp.arange(blk)[None, :]).astype(F32)
    within = jnp.einsum("ij,bjk->bik", tri, oh, precision=HIGHEST)
    btot = within[:, -1, :]
    before = jnp.cumsum(btot, axis=0) - btot
    rank = (within - 1.0 + before[:, None, :]).reshape(npair, N_EXPERTS)
    oh = oh.reshape(npair, N_EXPERTS)
    counts = jnp.sum(btot, axis=0).astype(jnp.int32)
    pcounts = (counts + tm - 1) // tm * tm
    pend = jnp.cumsum(pcounts)
    pstart = pend - pcounts
    pos = jnp.sum(oh * (rank + pstart.astype(F32)[None, :]), axis=1).astype(jnp.int32)
    nrow = npair + N_EXPERTS * tm
    ntile = nrow // tm
    tile_expert = jnp.sum((pend[None, :] <= (jnp.arange(ntile, dtype=jnp.int32) * tm)[:, None]).astype(jnp.int32), axis=1)
    tile_expert = jnp.minimum(tile_expert, N_EXPERTS - 1)
    n_used = (pend[-1] // tm).astype(jnp.int32).reshape(1)
    padcnt = pcounts - counts
    padend = jnp.cumsum(padcnt)
    j = jnp.arange(N_EXPERTS * tm, dtype=jnp.int32)
    e_j = jnp.sum((padend[None, :] <= j[:, None]).astype(jnp.int32), axis=1)
    shift = pstart + counts - (padend - padcnt)
    row_pad = jnp.sum((e_j[:, None] == experts[None, :]).astype(jnp.int32) * shift[None, :], axis=1) + j
    zero_rows = jnp.where(e_j < N_EXPERTS, row_pad, pend[-1] + j - padend[-1])
    return pos, zero_rows, tile_expert, n_used


def _dispatch_kernel(pos_ref, zr_ref, h_ref, xs_ref, zbuf, sem, *, ntok, tm, nz):
    i = pl.program_id(0)

    def issue(r, carry):
        src = h_ref.at[pl.ds(pl.multiple_of(r * ROW_TILE, ROW_TILE), ROW_TILE)]
        for k in range(2):
            row = pos_ref[k * ntok + i * tm + r]
            dst = xs_ref.at[pl.ds(pl.multiple_of(row * ROW_TILE, ROW_TILE), ROW_TILE)]
            pltpu.make_async_copy(src, dst, sem).start(priority=k)
        return carry

    lax.fori_loop(0, tm, issue, 0, unroll=ISSUE_UNROLL)
    zbuf[...] = jnp.zeros_like(zbuf)

    def zero(r, carry):
        for k in range(2):
            row = zr_ref[i * nz + 2 * r + k]
            dst = xs_ref.at[pl.ds(pl.multiple_of(row * ROW_TILE, ROW_TILE), ROW_TILE)]
            pltpu.make_async_copy(zbuf, dst, sem).start(priority=k)
        return carry

    lax.fori_loop(0, nz // 2, zero, 0)
    for _ in range(2):
        pltpu.make_async_copy(h_ref, xs_ref.at[pl.ds(0, tm * ROW_TILE)], sem).wait()
    pltpu.make_async_copy(h_ref.at[pl.ds(0, nz * ROW_TILE)], xs_ref.at[pl.ds(0, nz * ROW_TILE)], sem).wait()


def _dispatch_call(h2t, pos, zero_rows, nrow, tm):
    ntok = pos.shape[0] // 2
    nstep = ntok // tm
    nz = zero_rows.shape[0] // nstep
    grid_spec = pltpu.PrefetchScalarGridSpec(
        num_scalar_prefetch=2,
        grid=(nstep,),
        in_specs=[pl.BlockSpec((tm * ROW_TILE, 128), lambda i, p, z: (i, 0))],
        out_specs=pl.BlockSpec(memory_space=pl.ANY),
        scratch_shapes=[pltpu.VMEM((ROW_TILE, 128), F32), pltpu.SemaphoreType.DMA(())],
    )
    return pl.pallas_call(
        functools.partial(_dispatch_kernel, ntok=ntok, tm=tm, nz=nz),
        grid_spec=grid_spec,
        out_shape=jax.ShapeDtypeStruct((nrow * ROW_TILE, 128), F32),
        compiler_params=_cparams(("arbitrary",)),
        name="moe_dispatch",
    )(pos, zero_rows, h2t)


def _ffn_kernel(te_ref, nu_ref, xs_ref, wup_ref, wdn_ref, ys_ref, *, tm):
    del te_ref
    i = pl.program_id(0)

    @pl.when(i < nu_ref[0])
    def _():
        x = jnp.concatenate([xs_ref[pl.ds(s, tm, stride=ROW_TILE), :] for s in range(ROW_TILE)], axis=1)
        a = jnp.dot(x.astype(BF16), wup_ref[0], preferred_element_type=F32)
        g = a[:, :MOE_FF]
        hid = g * _sigmoid(g) * a[:, MOE_FF:]
        y = jnp.dot(hid.astype(BF16), wdn_ref[0], preferred_element_type=F32)
        for s in range(ROW_TILE):
            ys_ref[pl.ds(s, tm, stride=ROW_TILE), :] = y[:, s * 128:(s + 1) * 128]

    @pl.when(i >= nu_ref[0])
    def _():
        ys_ref[...] = jnp.zeros_like(ys_ref)


def _ffn_call(xs, tile_expert, n_used, w_up, w_down):
    tm = MOE_TM
    ntile = tile_expert.shape[0]
    grid_spec = pltpu.PrefetchScalarGridSpec(
        num_scalar_prefetch=2,
        grid=(ntile,),
        in_specs=[
            pl.BlockSpec((tm * ROW_TILE, 128), lambda i, te, nu: (i, 0)),
            pl.BlockSpec((1, D_MODEL, 2 * MOE_FF), lambda i, te, nu: (te[i], 0, 0)),
            pl.BlockSpec((1, MOE_FF, D_MODEL), lambda i, te, nu: (te[i], 0, 0)),
        ],
        out_specs=pl.BlockSpec((tm * ROW_TILE, 128), lambda i, te, nu: (i, 0)),
    )
    return pl.pallas_call(
        functools.partial(_ffn_kernel, tm=tm),
        grid_spec=grid_spec,
        out_shape=jax.ShapeDtypeStruct(xs.shape, F32),
        compiler_params=_cparams(("arbitrary",)),
        name="moe_ffn",
    )(tile_expert, n_used, xs, w_up, w_down)


def _final_kernel(pos_ref, x1_ref, route_ref, ada_ref, lng_ref, lnb_ref, ys_ref, o_ref, buf, sem,
                  *, ntok, tm, nstep):
    i = pl.program_id(0)
    slot = i % 2

    def fetch(step, to_slot):
        def issue(r, carry):
            for k in range(2):
                row = pos_ref[k * ntok + step * tm + r]
                src = ys_ref.at[pl.ds(pl.multiple_of(row * ROW_TILE, ROW_TILE), ROW_TILE)]
                dst = buf.at[to_slot, k, pl.ds(pl.multiple_of(r * ROW_TILE, ROW_TILE), ROW_TILE)]
                pltpu.make_async_copy(src, dst, sem.at[to_slot]).start(priority=k)
            return carry

        lax.fori_loop(0, tm, issue, 0, unroll=ISSUE_UNROLL)

    @pl.when(i == 0)
    def _():
        fetch(0, 0)

    @pl.when(i + 1 < nstep)
    def _():
        fetch(i + 1, 1 - slot)

    for k in range(2):
        pltpu.make_async_copy(ys_ref.at[pl.ds(0, tm * ROW_TILE)], buf.at[slot, k], sem.at[slot]).wait()
    ada = ada_ref[0]
    route = route_ref[0]
    w0, w1 = route[:, 2:3], route[:, 3:4]
    ffn = jnp.concatenate(
        [w0 * buf[slot, 0, pl.ds(s, tm, stride=ROW_TILE), :] + w1 * buf[slot, 1, pl.ds(s, tm, stride=ROW_TILE), :]
         for s in range(ROW_TILE)], axis=1)
    o_ref[0] = _ln(ALPHA * x1_ref[0] + (1.0 + ada[5:6]) * ffn) * lng_ref[...] + lnb_ref[...]


def _final_call(x1, ys, pos, route, ada8, ln_g, ln_b, tm):
    bsz, seq, _ = x1.shape
    nt = seq // tm
    nstep = bsz * nt
    grid_spec = pltpu.PrefetchScalarGridSpec(
        num_scalar_prefetch=1,
        grid=(nstep,),
        in_specs=[
            pl.BlockSpec((1, tm, D_MODEL), lambda i, p: (i // nt, i % nt, 0)),
            pl.BlockSpec((1, tm, 128), lambda i, p: (i // nt, i % nt, 0)),
            pl.BlockSpec((1, 8, D_MODEL), lambda i, p: (i // nt, 0, 0)),
            pl.BlockSpec((1, D_MODEL), lambda i, p: (0, 0)),
            pl.BlockSpec((1, D_MODEL), lambda i, p: (0, 0)),
            pl.BlockSpec(memory_space=pl.ANY),
        ],
        out_specs=pl.BlockSpec((1, tm, D_MODEL), lambda i, p: (i // nt, i % nt, 0)),
        scratch_shapes=[pltpu.VMEM((2, 2, tm * ROW_TILE, 128), F32), pltpu.SemaphoreType.DMA((2,))],
    )
    return pl.pallas_call(
        functools.partial(_final_kernel, ntok=bsz * seq, tm=tm, nstep=nstep),
        grid_spec=grid_spec,
        out_shape=jax.ShapeDtypeStruct((bsz, seq, D_MODEL), F32),
        compiler_params=_cparams(("arbitrary",)),
        name="final",
    )(pos, x1, route, ada8, ln_g, ln_b, ys)


def _split_w_in(w_in):
    sizes = (768, 768, 768, 512, 128, 128, 256, 256, 256, 256, N_BRANCH * D_MODEL)
    pts = [0]
    for s in sizes:
        pts.append(pts[-1] + s)
    return [w_in[:, pts[n]:pts[n + 1]] for n in range(len(sizes))]


def _hi_lo(w):
    hi = w.astype(BF16)
    return jnp.stack([hi, (w - hi.astype(F32)).astype(BF16)])


def _layer_weights(p, l):
    a_q, a_k, a_v, b_q, b_k, b_v, c_q, c_k, c_v, d_u, w_gate = _split_w_in(p["w_in"][l])
    scale = HEAD_DIM ** -0.5
    def a_cols(g):
        gs = slice(g * A_WIDTH, (g + 1) * A_WIDTH)
        return [a_q[:, gs] * scale, a_k[:, gs], a_v[:, gs]]

    cols = a_cols(0) + [b_k, b_v, b_q * scale, c_q * scale, c_k, c_v, d_u] + a_cols(1) + a_cols(2)
    w_route = jnp.concatenate([p["w_route_g"][l], p["w_route_e"][l]], axis=1)
    b_route = jnp.concatenate([p["b_route_g"][l], p["b_route_e"][l]])
    npad = 128 - w_route.shape[1]
    return {
        "w_qkv": jnp.concatenate(cols, axis=1).astype(BF16),
        "w_gate": w_gate.astype(BF16),
        "b_gate": p["b_gate"][l].reshape(1, -1),
        "w_branch": p["w_branch"][l].astype(BF16),
        "w_out": p["w_out"][l].astype(BF16),
        "w_glu": p["w_glu"][l].astype(BF16),
        "b_glu": p["b_glu"][l].reshape(1, -1),
        "ln1_g": p["ln1_g"][l].reshape(1, -1),
        "ln1_b": p["ln1_b"][l].reshape(1, -1),
        "ln2_g": p["ln2_g"][l].reshape(1, -1),
        "ln2_b": p["ln2_b"][l].reshape(1, -1),
        "w_route": _hi_lo(jnp.pad(w_route, ((0, 0), (0, npad)))),
        "b_route": jnp.pad(b_route, (0, npad)).reshape(1, -1),
        "w_up": p["w_up"][l].astype(BF16),
        "w_down": p["w_down"][l].astype(BF16),
        "b_sink": p["b_sink"][l],
        "c_bias": _c_bias_table(p["c_rpb"][l]),
    }


def _layer(x, ada8, lw, ssm_prm):
    bsz, seq, _ = x.shape
    tm, tm_mm = 256, 512
    h, main, fold1, fold2 = _proj_call(x, ada8, lw["w_qkv"], tm_mm)
    oa, la = zip(*[_attn_a_call(f, g, seq) for g, f in enumerate((main, fold1, fold2))])
    yb = _attn_b_call(main, lw["b_sink"])
    yc = _attn_c_call(main, lw["c_bias"])
    s5 = _ssm_call(main, ssm_prm)
    x1, h2t, route = _merge_call(x, h, ada8, oa, la, yb, yc, s5, lw, tm_mm)
    ntok = bsz * seq
    eid = route.reshape(ntok, 128)[:, :2].astype(jnp.int32)
    pos, zero_rows, tile_expert, n_used = _moe_plan(eid, MOE_TM)
    xs = _dispatch_call(h2t, pos, zero_rows, 2 * ntok + N_EXPERTS * MOE_TM, tm)
    ys = _ffn_call(xs, tile_expert, n_used, lw["w_up"], lw["w_down"])
    return _final_call(x1, ys, pos, route, ada8, lw["ln2_g"], lw["ln2_b"], tm)


def kernel(x_prompt, x_sample, c_prompt, c_sample, w_ada, b_ada, w_in, b_gate, b_sink, c_rpb, lam_re, lam_im,
           log_dt, ssm_b_re, ssm_b_im, ssm_c_re, ssm_c_im, ssm_d, w_glu, b_glu, w_branch, w_out, ln1_g, ln1_b,
           ln2_g, ln2_b, w_route_g, b_route_g, w_route_e, b_route_e, w_up, w_down):
    p = dict(w_in=w_in, b_gate=b_gate, b_sink=b_sink, c_rpb=c_rpb, w_glu=w_glu, b_glu=b_glu, w_branch=w_branch,
             w_out=w_out, ln1_g=ln1_g, ln1_b=ln1_b, ln2_g=ln2_g, ln2_b=ln2_b, w_route_g=w_route_g,
             b_route_g=b_route_g, w_route_e=w_route_e, b_route_e=b_route_e, w_up=w_up, w_down=w_down)
    nbp, nbs = c_prompt.shape[0], c_sample.shape[0]
    c_all = jnp.concatenate([c_prompt, c_sample], axis=0)
    c_all = jnp.pad(c_all, ((0, -c_all.shape[0] % 8), (0, 0)))
    ada = _ada_call(c_all, w_ada, b_ada).reshape(DEPTH, c_all.shape[0], 6, D_MODEL)
    ada = jnp.pad(ada, ((0, 0), (0, 0), (0, 2), (0, 0)))
    m_mat, bend, cin, apow = _ssm_param_call(lam_re, lam_im, log_dt, ssm_b_re, ssm_b_im, ssm_c_re, ssm_c_im, ssm_d)
    xs = [x_prompt, x_sample]
    for l in range(DEPTH):
        lw = _layer_weights(p, l)
        prm = (m_mat[l * SSM_GROUPS:(l + 1) * SSM_GROUPS], bend[2 * l:2 * l + 2], cin[2 * l:2 * l + 2],
               apow[2 * l:2 * l + 2])
        xs = [_layer(xs[0], ada[l, :nbp], lw, prm), _layer(xs[1], ada[l, nbp:nbp + nbs], lw, prm)]
    return (xs[0], xs[1])
```

```python
import functools
import math

import jax
import jax.numpy as jnp
from jax import lax
from jax.experimental import pallas as pl
from jax.experimental.pallas import tpu as pltpu

F32 = jnp.float32
BF16 = jnp.bfloat16
HIGHEST = lax.Precision.HIGHEST

D_MODEL = 1024
DEPTH = 2
HEAD_DIM = 64
A_PAIRS = ((128, 1), (512, 4), (2048, 16))
A_GROUPS = 3
A_HEADS = 4
A_WIDTH = A_HEADS * HEAD_DIM
B_Q_HEADS = 8
B_KV_HEADS = 2
B_WINDOW = 128
B_WIDTH = B_Q_HEADS * HEAD_DIM
GRID_W = 64
C_HEADS = 4
C_WIN_R = 8
C_WIN_C = 16
C_WIDTH = C_HEADS * HEAD_DIM
D_WIDTH = 256
SSM_GROUP = 16
SSM_GROUPS = D_WIDTH // SSM_GROUP
SSM_STATE = 64
N_BRANCH = 4
MOE_GROUPS = 4
MOE_EPG = 8
N_EXPERTS = MOE_GROUPS * MOE_EPG
MOE_FF = 512
ALPHA = (2 * DEPTH) ** 0.25
LN_EPS = 1e-5
NEG = -1e30

QKV_COLS = 4096
MAIN_COLS = 2560
COL_BKV = 3
COL_BQ = 2
COL_CQ = 6
COL_DU = 9
A_QKV = 3 * A_WIDTH
A_STEP_QUERIES = 2048
ROW_TILE = 8
A_HALF = 64
ATTN_SQ = 128
ATTN_TQ = 1024
B_INTERLEAVE = 8
MERGE_COLS = 256
SSM_CHUNK = 64
SSM_CW = SSM_CHUNK * SSM_GROUP
SSM_SCAN_BATCH = 4
MOE_TM = 512
ISSUE_UNROLL = 16
VMEM_LIMIT = 56 * 1024 * 1024


def _cparams(sem):
    return pltpu.CompilerParams(dimension_semantics=sem, vmem_limit_bytes=VMEM_LIMIT)


def _sigmoid(x):
    return 1.0 / (1.0 + jnp.exp(-x))


def _ln(x):
    mu = jnp.mean(x, axis=-1, keepdims=True)
    xc = x - mu
    var = jnp.mean(xc * xc, axis=-1, keepdims=True)
    return xc * lax.rsqrt(var + LN_EPS)


def _dot_nt(a, b, **kw):
    return lax.dot_general(a, b, (((1,), (1,)), ((), ())), preferred_element_type=F32, **kw)


def _split_head_pairs(q):
    lane = lax.broadcasted_iota(jnp.int32, q.shape, 1)
    even = (lane % 128) < HEAD_DIM
    zero = jnp.zeros_like(q)
    return jnp.where(even, q, zero), jnp.where(even, zero, q)


def _alibi(n):
    return [2.0 ** (-8.0 * (i + 1) / n) for i in range(n)]


def _ada_kernel(c_ref, w_ref, b_ref, o_ref):
    c = c_ref[...]
    sc = c * _sigmoid(c)
    o_ref[0] = jnp.dot(sc, w_ref[0], precision=HIGHEST, preferred_element_type=F32) + b_ref[0]


def _ada_call(c_all, w_ada, b_ada):
    nb = c_all.shape[0]
    return pl.pallas_call(
        _ada_kernel,
        grid=(DEPTH, 6),
        in_specs=[
            pl.BlockSpec((nb, D_MODEL), lambda l, j: (0, 0)),
            pl.BlockSpec((1, D_MODEL, D_MODEL), lambda l, j: (l, 0, j)),
            pl.BlockSpec((1, 1, D_MODEL), lambda l, j: (l, 0, j)),
        ],
        out_specs=pl.BlockSpec((1, nb, D_MODEL), lambda l, j: (l, 0, j)),
        out_shape=jax.ShapeDtypeStruct((DEPTH, nb, 6 * D_MODEL), F32),
        compiler_params=_cparams(("arbitrary", "arbitrary")),
        name="ada",
    )(c_all, w_ada, b_ada.reshape(DEPTH, 1, 6 * D_MODEL))


def _proj_kernel(x_ref, ada_ref, w_ref, h_ref, main_ref, f1_ref, f2_ref, scr_ref, *, chunk):
    x = x_ref[0]
    ada = ada_ref[0]
    tm = x.shape[0]
    h = _ln(x) * (1.0 + ada[1:2]) + ada[0:1]
    hb = h.astype(BF16)
    h_ref[0] = hb
    for j in range(MAIN_COLS // chunk):
        sl = slice(j * chunk, (j + 1) * chunk)
        main_ref[0, :, sl] = jnp.dot(hb, w_ref[:, sl], preferred_element_type=F32).astype(BF16)
    for g, f_ref in ((1, f1_ref), (2, f2_ref)):
        dil = A_PAIRS[g][1]
        c0 = MAIN_COLS + (g - 1) * A_QKV
        res = jnp.dot(hb, w_ref[:, c0:c0 + A_QKV], preferred_element_type=F32)
        for c in range(A_QKV // 128):
            scr_ref[c] = res[:, c * 128:(c + 1) * 128]
        for r in range(dil):
            for c in range(A_QKV // 128):
                lo = r * A_QKV + c * 128
                f_ref[0, :, lo:lo + 128] = scr_ref[c, pl.ds(r, tm // dil, stride=dil), :].astype(BF16)


def _proj_call(x, ada8, w_qkv, tm):
    bsz, seq, _ = x.shape
    d1, d2 = A_PAIRS[1][1], A_PAIRS[2][1]
    return pl.pallas_call(
        functools.partial(_proj_kernel, chunk=512),
        grid=(bsz, seq // tm),
        in_specs=[
            pl.BlockSpec((1, tm, D_MODEL), lambda b, i: (b, i, 0)),
            pl.BlockSpec((1, 8, D_MODEL), lambda b, i: (b, 0, 0)),
            pl.BlockSpec((D_MODEL, QKV_COLS), lambda b, i: (0, 0), pipeline_mode=pl.Buffered(1)),
        ],
        out_specs=[
            pl.BlockSpec((1, tm, D_MODEL), lambda b, i: (b, i, 0)),
            pl.BlockSpec((1, tm, MAIN_COLS), lambda b, i: (b, i, 0)),
            pl.BlockSpec((1, tm // d1, d1 * A_QKV), lambda b, i: (b, i, 0)),
            pl.BlockSpec((1, tm // d2, d2 * A_QKV), lambda b, i: (b, i, 0)),
        ],
        out_shape=[
            jax.ShapeDtypeStruct((bsz, seq, D_MODEL), BF16),
            jax.ShapeDtypeStruct((bsz, seq, MAIN_COLS), BF16),
            jax.ShapeDtypeStruct((bsz, seq // d1, d1 * A_QKV), BF16),
            jax.ShapeDtypeStruct((bsz, seq // d2, d2 * A_QKV), BF16),
        ],
        scratch_shapes=[pltpu.VMEM((A_QKV // 128, tm, 128), F32)],
        compiler_params=_cparams(("parallel", "parallel")),
        name="proj",
    )(x, ada8, w_qkv)


def _attn_a_kernel(cur_ref, prev_ref, nxt_ref, o_ref, lse_ref, *, tq, sub, slopes, dil, nres):
    for res in range(nres):
        c0 = res * A_QKV

        def cols(ref, n):
            return ref[0, :, c0 + n * A_WIDTH:c0 + (n + 1) * A_WIDTH]

        k = jnp.concatenate([cols(prev_ref, 1), cols(cur_ref, 1), cols(nxt_ref, 1)], axis=0)
        v = jnp.concatenate([cols(prev_ref, 2), cols(cur_ref, 2), cols(nxt_ref, 2)], axis=0)
        _attn_a_residue(cols(cur_ref, 0), k, v, o_ref, lse_ref, res * A_WIDTH,
                        tq=tq, sub=sub, slopes=slopes, dil=dil)


def _attn_a_residue(q, k, v, o_ref, lse_ref, out_col, *, tq, sub, slopes, dil):
    i = pl.program_id(2)
    sq = ATTN_SQ
    kw = sq + 2 * A_HALF
    row = lax.broadcasted_iota(jnp.int32, (sq, kw), 0)
    col = lax.broadcasted_iota(jnp.int32, (sq, kw), 1)
    rel = col - A_HALF - row
    dist = jnp.abs(rel).astype(F32) * float(dil)
    even_q, odd_q = _split_head_pairs(q)
    even_lanes = (lax.broadcasted_iota(jnp.int32, (sq, 128), 1) < HEAD_DIM)
    for j in range(tq // sq):
        kpos = i * tq + j * sq - A_HALF + col
        valid = (jnp.abs(rel) <= A_HALF) & (kpos >= 0) & (kpos < sub)
        rows = slice(j * sq, (j + 1) * sq)
        kj = k[j * sq:j * sq + kw]
        vj = v[j * sq:j * sq + kw]
        hs = range(A_HEADS)
        pss = [slice((h // 2) * 128, (h // 2 + 1) * 128) for h in hs]
        ss = [_dot_nt((even_q, odd_q)[h % 2][rows, ps], kj[:, ps]) for h, ps in zip(hs, pss)]
        ss = [jnp.where(valid, s - slopes[h] * dist, NEG) for s, h in zip(ss, hs)]
        ms = [jnp.max(s, axis=-1, keepdims=True) for s in ss]
        pp = [jnp.exp(s - m) for s, m in zip(ss, ms)]
        dens = [jnp.sum(p, axis=-1, keepdims=True) for p in pp]
        outs = [jnp.dot(p.astype(BF16), vj[:, ps], preferred_element_type=F32) / den
                for p, ps, den in zip(pp, pss, dens)]
        lses = [m + jnp.log(den) for m, den in zip(ms, dens)]
        for n in range(0, A_HEADS, 2):
            ocols = slice(out_col + n // 2 * 128, out_col + (n // 2 + 1) * 128)
            o_ref[0, rows, ocols] = jnp.where(even_lanes, outs[n], outs[n + 1]).astype(BF16)
            lse_ref[0, rows, ocols] = jnp.where(even_lanes, lses[n], lses[n + 1])


def _attn_a_call(folded, g, seq):
    bsz = folded.shape[0]
    _, dil = A_PAIRS[g]
    sub = seq // dil
    tq = min(ATTN_TQ, sub)
    nq = sub // tq
    hb = tq // A_HALF
    nhb = sub // A_HALF
    nres = max(1, min(dil, A_STEP_QUERIES // tq))
    slopes = _alibi(A_GROUPS * A_HEADS)[g * A_HEADS:(g + 1) * A_HEADS]
    wide = nres * A_QKV

    return pl.pallas_call(
        functools.partial(_attn_a_kernel, tq=tq, sub=sub, slopes=slopes, dil=dil, nres=nres),
        grid=(bsz, dil // nres, nq),
        in_specs=[
            pl.BlockSpec((1, tq, wide), lambda b, r, i: (b, i, r)),
            pl.BlockSpec((1, A_HALF, wide), lambda b, r, i: (b, jnp.maximum(i * hb - 1, 0), r)),
            pl.BlockSpec((1, A_HALF, wide), lambda b, r, i: (b, jnp.minimum((i + 1) * hb, nhb - 1), r)),
        ],
        out_specs=[
            pl.BlockSpec((1, tq, nres * A_WIDTH), lambda b, r, i: (b, i, r)),
            pl.BlockSpec((1, tq, nres * A_WIDTH), lambda b, r, i: (b, i, r)),
        ],
        out_shape=[
            jax.ShapeDtypeStruct((bsz, sub, dil * A_WIDTH), BF16),
            jax.ShapeDtypeStruct((bsz, sub, dil * A_WIDTH), F32),
        ],
        compiler_params=_cparams(("parallel", "parallel", "parallel")),
        name=f"attn_a{g}",
    )(folded, folded, folded)


def _attn_b_kernel(sink_ref, q_ref, kvc_ref, kvp_ref, kvn_ref, o_ref, *, tq, seq, slopes):
    i = pl.program_id(1)
    q = q_ref[0]
    kv = jnp.concatenate([kvp_ref[0], kvc_ref[0], kvn_ref[0]], axis=0)
    sq = ATTN_SQ
    kw = sq + 2 * B_WINDOW
    row = lax.broadcasted_iota(jnp.int32, (sq, kw), 0)
    col = lax.broadcasted_iota(jnp.int32, (sq, kw), 1)
    rel = col - B_WINDOW - row
    dist = jnp.abs(rel).astype(F32)
    rep = B_Q_HEADS // B_KV_HEADS
    even_q, odd_q = _split_head_pairs(q)
    even_lanes = (lax.broadcasted_iota(jnp.int32, (sq, 128), 1) < HEAD_DIM)

    def doubled(c):
        piece = kv[:, c * HEAD_DIM:(c + 1) * HEAD_DIM]
        return jnp.concatenate([piece, piece], axis=1)

    k2 = [doubled(g) for g in range(B_KV_HEADS)]
    v2 = [doubled(B_KV_HEADS + g) for g in range(B_KV_HEADS)]
    for j in range(tq // sq):
        kpos = i * tq + j * sq - B_WINDOW + col
        valid = (jnp.abs(rel) <= B_WINDOW) & (kpos >= 0) & (kpos < seq)
        rows = slice(j * sq, (j + 1) * sq)
        for h0 in range(0, B_Q_HEADS, B_INTERLEAVE):
            hs = range(h0, h0 + B_INTERLEAVE)
            qms = [(even_q, odd_q)[h % 2][rows, (h // 2) * 128:(h // 2 + 1) * 128] for h in hs]
            kjs = [k2[h // rep][j * sq:j * sq + kw] for h in hs]
            vjs = [v2[h // rep][j * sq:j * sq + kw] for h in hs]
            sks = [sink_ref[h] for h in hs]
            ss = [_dot_nt(qm, kj) for qm, kj in zip(qms, kjs)]
            ss = [jnp.where(valid, s - slopes[h] * dist, NEG) for s, h in zip(ss, hs)]
            ms = [jnp.maximum(jnp.max(s, axis=-1, keepdims=True), sk) for s, sk in zip(ss, sks)]
            pp = [jnp.exp(s - m) for s, m in zip(ss, ms)]
            dens = [jnp.sum(p, axis=-1, keepdims=True) + jnp.exp(sk - m) for p, sk, m in zip(pp, sks, ms)]
            outs = [jnp.dot(p.astype(BF16), vj, preferred_element_type=F32) / den
                    for p, vj, den in zip(pp, vjs, dens)]
            for n in range(0, B_INTERLEAVE, 2):
                ps = slice((h0 + n) // 2 * 128, ((h0 + n) // 2 + 1) * 128)
                o_ref[0, rows, ps] = jnp.where(even_lanes, outs[n], outs[n + 1]).astype(BF16)


def _attn_b_call(qkv, sink):
    bsz, seq, _ = qkv.shape
    tq = min(ATTN_TQ, seq)
    hb = tq // B_WINDOW
    nhb = seq // B_WINDOW
    return pl.pallas_call(
        functools.partial(_attn_b_kernel, tq=tq, seq=seq, slopes=_alibi(B_Q_HEADS)),
        grid=(bsz, seq // tq),
        in_specs=[
            pl.BlockSpec(memory_space=pltpu.SMEM),
            pl.BlockSpec((1, tq, B_WIDTH), lambda b, i: (b, i, COL_BQ)),
            pl.BlockSpec((1, tq, 256), lambda b, i: (b, i, COL_BKV)),
            pl.BlockSpec((1, B_WINDOW, 256), lambda b, i: (b, jnp.maximum(i * hb - 1, 0), COL_BKV)),
            pl.BlockSpec((1, B_WINDOW, 256), lambda b, i: (b, jnp.minimum((i + 1) * hb, nhb - 1), COL_BKV)),
        ],
        out_specs=pl.BlockSpec((1, tq, B_WIDTH), lambda b, i: (b, i, 0)),
        out_shape=jax.ShapeDtypeStruct((bsz, seq, B_WIDTH), BF16),
        compiler_params=_cparams(("parallel", "parallel")),
        name="attn_b",
    )(sink, qkv, qkv, qkv, qkv)


C_ROWS_PER_STEP = 16
C_INTERLEAVE_ROWS = 4


def _attn_c_kernel(q_ref, k_ref, v_ref, bias_ref, o_ref, *, rows):
    i = pl.program_id(1)
    kwin = C_WIN_R * GRID_W
    even_q, odd_q = _split_head_pairs(q_ref[0])
    even_lanes = (lax.broadcasted_iota(jnp.int32, (GRID_W, 128), 1) < HEAD_DIM)
    for jr0 in range(0, C_ROWS_PER_STEP, C_INTERLEAVE_ROWS):
        units = []
        for jr in range(jr0, jr0 + C_INTERLEAVE_ROWS):
            r = i * C_ROWS_PER_STEP + jr
            rs = jnp.clip(r - C_WIN_R // 2, 0, rows - C_WIN_R)
            start = pl.multiple_of(rs * GRID_W, GRID_W)
            kw = k_ref[0, pl.ds(start, kwin), :]
            vw = v_ref[0, pl.ds(start, kwin), :]
            for h in range(C_HEADS):
                units.append((slice(jr * GRID_W, (jr + 1) * GRID_W), h, r - rs, kw, vw))
        pss = [slice((h // 2) * 128, (h // 2 + 1) * 128) for _, h, _, _, _ in units]
        ss = [_dot_nt((even_q, odd_q)[h % 2][qrows, ps], kw[:, ps]) + bias_ref[h, delta]
              for (qrows, h, delta, kw, _), ps in zip(units, pss)]
        ms = [jnp.max(s, axis=-1, keepdims=True) for s in ss]
        pp = [jnp.exp(s - m) for s, m in zip(ss, ms)]
        dens = [jnp.sum(p, axis=-1, keepdims=True) for p in pp]
        outs = [jnp.dot(p.astype(BF16), vw[:, ps], preferred_element_type=F32) / den
                for p, (_, _, _, _, vw), ps, den in zip(pp, units, pss, dens)]
        for n in range(0, len(units), 2):
            o_ref[0, units[n][0], pss[n]] = jnp.where(even_lanes, outs[n], outs[n + 1]).astype(BF16)


def _c_bias_table(rpb):
    qc = jnp.arange(GRID_W)[:, None]
    kc = jnp.arange(GRID_W)[None, :]
    col_off = jnp.clip(kc - qc + (C_WIN_C - 1), 0, 2 * C_WIN_C - 2)
    sel = (col_off[None] == jnp.arange(2 * C_WIN_C - 1)[:, None, None]).astype(F32)
    band = jnp.einsum("hrc,cqk->hrqk", rpb.astype(F32), sel, precision=HIGHEST)
    cs = jnp.clip(qc - C_WIN_C // 2, 0, GRID_W - C_WIN_C)
    valid = (kc >= cs) & (kc < cs + C_WIN_C)
    band = jnp.where(valid[None, None], band, NEG)
    tabs = [band[:, C_WIN_R - 1 - d:2 * C_WIN_R - 1 - d].transpose(0, 2, 1, 3) for d in range(C_WIN_R)]
    return jnp.stack(tabs, axis=1).reshape(C_HEADS, C_WIN_R, GRID_W, C_WIN_R * GRID_W)


def _attn_c_call(qkv, bias_tab):
    bsz, seq, _ = qkv.shape
    rows = seq // GRID_W
    tq = C_ROWS_PER_STEP * GRID_W
    return pl.pallas_call(
        functools.partial(_attn_c_kernel, rows=rows),
        grid=(bsz, seq // tq),
        in_specs=[
            pl.BlockSpec((1, tq, C_WIDTH), lambda b, i: (b, i, COL_CQ)),
            pl.BlockSpec((1, seq, C_WIDTH), lambda b, i: (b, 0, COL_CQ + 1)),
            pl.BlockSpec((1, seq, C_WIDTH), lambda b, i: (b, 0, COL_CQ + 2)),
            pl.BlockSpec(bias_tab.shape, lambda b, i: (0, 0, 0, 0)),
        ],
        out_specs=pl.BlockSpec((1, tq, C_WIDTH), lambda b, i: (b, i, 0)),
        out_shape=jax.ShapeDtypeStruct((bsz, seq, C_WIDTH), BF16),
        compiler_params=_cparams(("parallel", "arbitrary")),
        name="attn_c",
    )(qkv, qkv, qkv, bias_tab)


def _cmul(ar, ai, br, bi):
    return ar * br - ai * bi, ar * bi + ai * br


def _ssm_param_kernel(lre_ref, lim_ref, ldt_ref, cre_ref, cim_ref, bre_ref, bim_ref, d_ref,
                      m_ref, bend_ref, cin_ref, apow_ref):
    tc, hh, pp = SSM_CHUNK, SSM_GROUP, SSM_STATE
    it = lax.broadcasted_iota(jnp.int32, (tc, pp), 0).astype(F32)
    lane = lax.broadcasted_iota(jnp.int32, (hh, 128), 1)
    subl = lax.broadcasted_iota(jnp.int32, (hh, 128), 0)
    tabs = []
    for d in range(2):
        lr, li = lre_ref[d, 0, 0], lim_ref[d, 0, 0]
        dt = jnp.exp(ldt_ref[d, 0, 0])

        def powers(kv):
            mag = jnp.exp(kv * (lr * dt))
            ang = kv * (li * dt)
            return mag * jnp.cos(ang), mag * jnp.sin(ang)

        one = jnp.ones((1, pp), F32)
        lbr, lbi = powers(one)
        den = lr * lr + li * li
        nr, ni = lbr - 1.0, lbi
        fr = (nr * lr + ni * li) / den
        fi = (ni * lr - nr * li) / den
        apr, api = powers(one * float(tc))
        apow_ref[d, 0] = jnp.concatenate([apr, api], axis=1)

        cr, ci = cre_ref[d, 0], cim_ref[d, 0]
        br, bi = bre_ref[d, 0], bim_ref[d, 0]

        wr, wi = powers(it + 1.0 if d == 0 else float(tc) - it)
        cwr, cwi = _cmul(cr[None], ci[None], wr[:, None, :], wi[:, None, :])
        cin = jnp.concatenate([cwr, -cwi], axis=2).reshape(tc * hh, 2 * pp)
        cin_ref[d, 0] = cin.astype(BF16)

        wr, wi = powers(float(tc - 1) - it if d == 0 else it)
        wfr, wfi = _cmul(wr, wi, fr, fi)
        er, ei = _cmul(br[None], bi[None], wfr[:, None, :], wfi[:, None, :])
        bend_ref[d, 0] = jnp.concatenate([er, ei], axis=2).reshape(tc * hh, 2 * pp).astype(BF16)

        wr, wi = powers(it if d == 0 else float(tc - 1) - it)
        wfr, wfi = _cmul(wr, wi, fr, fi)
        gr, gi = _cmul(cr[None], ci[None], wfr[:, None, :], wfi[:, None, :])
        gr = gr.reshape(tc * hh, pp)
        gi = gi.reshape(tc * hh, pp)
        tabs.append(_dot_nt(br, gr, precision=HIGHEST) - _dot_nt(bi, gi, precision=HIGHEST))

    ktf, ktb = tabs
    last = pltpu.roll(ktb[:, SSM_CW - 128:], hh, axis=1)
    diag = jnp.where(lane == subl, d_ref[0], 0.0)
    first = ktf[:, :128] + jnp.where(lane < hh, last + diag, 0.0)
    ktf = jnp.concatenate([first, ktf[:, 128:]], axis=1)
    lane_w = lax.broadcasted_iota(jnp.int32, (hh, SSM_CW), 1)
    ktb = jnp.where(lane_w < SSM_CW - hh, ktb, 0.0)
    zeros = jnp.zeros((hh, SSM_CW), F32)
    pf = jnp.concatenate([ktf, zeros], axis=1)
    pb = jnp.concatenate([ktb, zeros], axis=1)
    for j in range(tc):
        blk = pf if j == 0 else pltpu.roll(pf, j * hh, axis=1)
        sh = (tc - 1 - j) * hh
        blb = pb if sh == 0 else pltpu.roll(pb, 2 * SSM_CW - sh, axis=1)
        m_ref[0, j * hh:(j + 1) * hh, :] = (blk[:, :SSM_CW] + blb[:, :SSM_CW]).astype(BF16)


def _ssm_param_call(lam_re, lam_im, log_dt, b_re, b_im, c_re, c_im, d_skip):
    gg, pp, hh = SSM_GROUPS, SSM_STATE, SSM_GROUP
    nl = lam_re.shape[0]
    lre = lam_re.reshape(nl * 2, gg, 1, 1, pp)
    lim = lam_im.reshape(nl * 2, gg, 1, 1, pp)
    ldt = jnp.broadcast_to(log_dt.reshape(nl * 2, gg, 1, 1, 1), (nl * 2, gg, 1, 1, pp))
    cre = c_re.reshape(nl * 2, gg, hh, pp)
    cim = c_im.reshape(nl * 2, gg, hh, pp)
    bre = jnp.swapaxes(b_re, -1, -2).reshape(nl * 2, gg, hh, pp)
    bim = jnp.swapaxes(b_im, -1, -2).reshape(nl * 2, gg, hh, pp)
    dpad = jnp.pad(d_skip.reshape(nl * gg, 1, hh), ((0, 0), (0, 0), (0, 128 - hh)))

    def vec(l, g):
        return (l, g, 0, 0, 0)

    def mat(l, g):
        return (l, g, 0, 0)

    return pl.pallas_call(
        _ssm_param_kernel,
        grid=(nl, gg),
        in_specs=[
            pl.BlockSpec((2, 1, 1, 1, pp), vec), pl.BlockSpec((2, 1, 1, 1, pp), vec),
            pl.BlockSpec((2, 1, 1, 1, pp), vec),
            pl.BlockSpec((2, 1, hh, pp), mat), pl.BlockSpec((2, 1, hh, pp), mat),
            pl.BlockSpec((2, 1, hh, pp), mat), pl.BlockSpec((2, 1, hh, pp), mat),
            pl.BlockSpec((1, 1, 128), lambda l, g: (l * gg + g, 0, 0)),
        ],
        out_specs=[
            pl.BlockSpec((1, SSM_CW, SSM_CW), lambda l, g: (l * gg + g, 0, 0)),
            pl.BlockSpec((2, 1, SSM_CW, 2 * pp), mat),
            pl.BlockSpec((2, 1, SSM_CW, 2 * pp), mat),
            pl.BlockSpec((2, 1, 1, 2 * pp), mat),
        ],
        out_shape=[
            jax.ShapeDtypeStruct((nl * gg, SSM_CW, SSM_CW), BF16),
            jax.ShapeDtypeStruct((nl * 2, gg, SSM_CW, 2 * pp), BF16),
            jax.ShapeDtypeStruct((nl * 2, gg, SSM_CW, 2 * pp), BF16),
            jax.ShapeDtypeStruct((nl * 2, gg, 1, 2 * pp), F32),
        ],
        compiler_params=_cparams(("arbitrary", "arbitrary")),
        name="ssm_param",
    )(lre, lim, ldt, cre, cim, bre, bim, dpad)


def _ssm_state_kernel(u_ref, bend_ref, s_ref):
    for g in range(SSM_GROUPS):
        u = u_ref[0, g]
        for d in range(2):
            s_ref[0, d, :, g * 128:(g + 1) * 128] = jnp.dot(u, bend_ref[d, g], preferred_element_type=F32)


def _ssm_scan_kernel(s_ref, apow_ref, x_ref, *, nc, bb):
    width = SSM_GROUPS * 128
    lane = lax.broadcasted_iota(jnp.int32, (1, width), 1)
    first_half = (lane % 128) < SSM_STATE
    a_re, a_im = [], []
    for d in range(2):
        a = apow_ref[d]
        a_re.append(jnp.where(first_half, a, pltpu.roll(a, SSM_STATE, axis=1)))
        a_im.append(jnp.where(first_half, -pltpu.roll(a, width - SSM_STATE, axis=1), a))

    def body(c, xs):
        new = []
        for n, x in enumerate(xs):
            b, d = divmod(n, 2)
            cc = c if d == 0 else nc - 1 - c
            x_ref[b, d, pl.ds(cc, 1), :] = x
            x_sw_lo = pltpu.roll(x, SSM_STATE, axis=1)
            x_sw_hi = pltpu.roll(x, width - SSM_STATE, axis=1)
            x_sw = jnp.where(first_half, x_sw_hi, x_sw_lo)
            new.append(a_re[d] * x + a_im[d] * x_sw + s_ref[b, d, pl.ds(cc, 1), :])
        return tuple(new)

    lax.fori_loop(0, nc, body, tuple(jnp.zeros((1, width), F32) for _ in range(2 * bb)))


def _ssm_out_kernel(u_ref, m_ref, cin_ref, x_ref, y_ref):
    bsz, _, nc, cw = u_ref.shape
    y = jnp.dot(u_ref[:, 0].reshape(bsz * nc, cw), m_ref[0], preferred_element_type=F32)
    for d in range(2):
        y = y + _dot_nt(x_ref[:, d].reshape(bsz * nc, 128).astype(BF16), cin_ref[d, 0])
    y_ref[:, 0] = y.reshape(bsz, nc, cw).astype(BF16)


def _ssm_call(qkv, prm):
    m_mat, bend, cin, apow = prm
    bsz, seq, _ = qkv.shape
    gg, tc, hh = SSM_GROUPS, SSM_CHUNK, SSM_GROUP
    nc = seq // tc
    u = qkv[:, :, COL_DU * 256:(COL_DU + 1) * 256]
    u = u.reshape(bsz, nc, tc, gg, hh).transpose(0, 3, 1, 2, 4).reshape(bsz, gg, nc, SSM_CW)
    width = gg * 128
    bb = math.gcd(bsz, SSM_SCAN_BATCH)
    s = pl.pallas_call(
        _ssm_state_kernel,
        grid=(bsz,),
        in_specs=[
            pl.BlockSpec((1, gg, nc, SSM_CW), lambda b: (b, 0, 0, 0)),
            pl.BlockSpec((2, gg, SSM_CW, 128), lambda b: (0, 0, 0, 0)),
        ],
        out_specs=pl.BlockSpec((1, 2, nc, width), lambda b: (b, 0, 0, 0)),
        out_shape=jax.ShapeDtypeStruct((bsz, 2, nc, width), F32),
        compiler_params=_cparams(("parallel",)),
        name="ssm_state",
    )(u, bend)
    xprev = pl.pallas_call(
        functools.partial(_ssm_scan_kernel, nc=nc, bb=bb),
        grid=(bsz // bb,),
        in_specs=[
            pl.BlockSpec((bb, 2, nc, width), lambda b: (b, 0, 0, 0)),
            pl.BlockSpec((2, 1, width), lambda b: (0, 0, 0)),
        ],
        out_specs=pl.BlockSpec((bb, 2, nc, width), lambda b: (b, 0, 0, 0)),
        out_shape=jax.ShapeDtypeStruct((bsz, 2, nc, width), F32),
        compiler_params=_cparams(("parallel",)),
        name="ssm_scan",
    )(s, apow.reshape(2, 1, width))
    y = pl.pallas_call(
        _ssm_out_kernel,
        grid=(gg,),
        in_specs=[
            pl.BlockSpec((bsz, 1, nc, SSM_CW), lambda g: (0, g, 0, 0)),
            pl.BlockSpec((1, SSM_CW, SSM_CW), lambda g: (g, 0, 0)),
            pl.BlockSpec((2, 1, SSM_CW, 128), lambda g: (0, g, 0, 0)),
            pl.BlockSpec((bsz, 2, nc, 128), lambda g: (0, 0, 0, g)),
        ],
        out_specs=pl.BlockSpec((bsz, 1, nc, SSM_CW), lambda g: (0, g, 0, 0)),
        out_shape=jax.ShapeDtypeStruct((bsz, gg, nc, SSM_CW), BF16),
        compiler_params=_cparams(("arbitrary",)),
        name="ssm_out",
    )(u, m_mat, cin, xprev)
    return y.reshape(bsz, gg, nc, tc, hh).transpose(0, 2, 3, 1, 4).reshape(bsz, seq, D_WIDTH)


def _merge_kernel(x_ref, h_ref, ada_ref,
                  oa0_ref, oa1_ref, oa2_ref, la0_ref, la1_ref, la2_ref, yb_ref, yc_ref, s5_ref,
                  wg_ref, bg_ref, wbr_ref, wout_ref, wglu_ref, bglu_ref, lng_ref, lnb_ref,
                  wr_ref, br_ref,
                  x1_ref, h2_ref, route_ref, o1_scr, l1_scr, o2_scr, l2_scr):
    ada = ada_ref[0]
    hb = h_ref[0]
    tm = hb.shape[0]
    for g, (o_ref, l_ref, o_scr, l_scr) in ((1, (oa1_ref, la1_ref, o1_scr, l1_scr)),
                                            (2, (oa2_ref, la2_ref, o2_scr, l2_scr))):
        dil = A_PAIRS[g][1]
        for r in range(dil):
            for c in range(A_WIDTH // 128):
                cs = slice(r * A_WIDTH + c * 128, r * A_WIDTH + (c + 1) * 128)
                o_scr[c, pl.ds(r, tm // dil, stride=dil), :] = o_ref[0, :, cs].astype(F32)
                l_scr[c, pl.ds(r, tm // dil, stride=dil), :] = l_ref[0, :, cs]

    def unfolded(scr):
        return jnp.concatenate([scr[c] for c in range(A_WIDTH // 128)], axis=1)

    l0, l1, l2 = la0_ref[0], unfolded(l1_scr), unfolded(l2_scr)
    lm = jnp.maximum(jnp.maximum(l0, l1), l2)
    e0, e1, e2 = jnp.exp(l0 - lm), jnp.exp(l1 - lm), jnp.exp(l2 - lm)
    ya = (e0 * oa0_ref[0].astype(F32) + e1 * unfolded(o1_scr) + e2 * unfolded(o2_scr)) / (e0 + e1 + e2)
    s5 = s5_ref[0].astype(F32)
    gel = 0.5 * s5 * (1.0 + jnp.tanh(math.sqrt(2.0 / math.pi) * (s5 + 0.044715 * (s5 * s5 * s5))))
    z = jnp.dot(gel.astype(BF16), wglu_ref[...], preferred_element_type=F32) + bglu_ref[...]
    yd = z[:, :D_WIDTH] * _sigmoid(z[:, D_WIDTH:])
    branches = (ya.astype(BF16), yb_ref[0], yc_ref[0], yd.astype(BF16))
    offs = (0, A_WIDTH, A_WIDTH + B_WIDTH, A_WIDTH + B_WIDTH + C_WIDTH, A_WIDTH + B_WIDTH + C_WIDTH + D_WIDTH)
    def chunk_dots(c0):
        css = [slice(n * D_MODEL + c0, n * D_MODEL + c0 + MERGE_COLS) for n in range(N_BRANCH)]
        pre = [jnp.dot(hb, wg_ref[:, cs], preferred_element_type=F32) + bg_ref[:, cs] for cs in css]
        brs = [jnp.dot(yb, wbr_ref[offs[n]:offs[n + 1], c0:c0 + MERGE_COLS], preferred_element_type=F32)
               for n, yb in enumerate(branches)]
        return pre, brs

    merged = []
    nxt = chunk_dots(0)
    for c0 in range(0, D_MODEL, MERGE_COLS):
        pre, brs = nxt
        if c0 + MERGE_COLS < D_MODEL:
            nxt = chunk_dots(c0 + MERGE_COLS)
        terms = [_sigmoid(g) * b for g, b in zip(pre, brs)]
        merged.append(((terms[0] + terms[1]) + terms[2]) + terms[3])
    merged = jnp.concatenate(merged, axis=1)
    mix = jnp.dot(merged.astype(BF16), wout_ref[...], preferred_element_type=F32)
    x1 = _ln(ALPHA * x_ref[0] + (1.0 + ada[2:3]) * mix) * lng_ref[...] + lnb_ref[...]
    x1_ref[0] = x1
    h2 = _ln(x1) * (1.0 + ada[4:5]) + ada[3:4]
    for s in range(ROW_TILE):
        h2_ref[pl.ds(s, tm, stride=ROW_TILE), :] = h2[:, s * 128:(s + 1) * 128]
    h2_hi = h2.astype(BF16)
    h2_lo = (h2 - h2_hi.astype(F32)).astype(BF16)
    logit = (jnp.dot(h2_hi, wr_ref[0], preferred_element_type=F32)
             + (jnp.dot(h2_lo, wr_ref[0], preferred_element_type=F32)
                + jnp.dot(h2_hi, wr_ref[1], preferred_element_type=F32))) + br_ref[...]
    lane = lax.broadcasted_iota(jnp.int32, logit.shape, 1)
    ninf = -jnp.inf
    gmask = lane < MOE_GROUPS
    gl = jnp.where(gmask, logit, ninf)
    gmax = jnp.max(gl, axis=-1, keepdims=True)
    gsel = jnp.min(jnp.where(gl == gmax, lane, 1 << 20), axis=-1, keepdims=True)
    gw = 1.0 / jnp.sum(jnp.exp(gl - gmax), axis=-1, keepdims=True)
    lo = MOE_GROUPS + gsel * MOE_EPG
    emask = (lane >= lo) & (lane < lo + MOE_EPG)
    el = jnp.where(emask, logit, ninf)
    v1 = jnp.max(el, axis=-1, keepdims=True)
    i1 = jnp.min(jnp.where(el == v1, lane, 1 << 20), axis=-1, keepdims=True)
    el2 = jnp.where(lane == i1, ninf, el)
    v2 = jnp.max(el2, axis=-1, keepdims=True)
    i2 = jnp.min(jnp.where(el2 == v2, lane, 1 << 20), axis=-1, keepdims=True)
    t = jnp.exp(v2 - v1)
    w1 = gw / (1.0 + t)
    w2 = gw * t / (1.0 + t)
    route = jnp.where(lane == 0, (i1 - MOE_GROUPS).astype(F32),
                      jnp.where(lane == 1, (i2 - MOE_GROUPS).astype(F32),
                                jnp.where(lane == 2, w1, jnp.where(lane == 3, w2, 0.0))))
    route_ref[0] = route


def _merge_call(x, h, ada8, oa, la, yb, yc, s5, lw, tm):
    bsz, seq, _ = x.shape
    nt = seq // tm

    def tok(width):
        return pl.BlockSpec((1, tm, width), lambda b, i: (b, i, 0))

    def fold(g):
        dil = A_PAIRS[g][1]
        return pl.BlockSpec((1, tm // dil, dil * A_WIDTH), lambda b, i: (b, i, 0))

    def full(arr):
        return pl.BlockSpec(arr.shape, lambda b, i: (0,) * arr.ndim, pipeline_mode=pl.Buffered(1))

    weights = (lw["w_gate"], lw["b_gate"], lw["w_branch"], lw["w_out"], lw["w_glu"], lw["b_glu"],
               lw["ln1_g"], lw["ln1_b"], lw["w_route"], lw["b_route"])
    return pl.pallas_call(
        _merge_kernel,
        grid=(bsz, nt),
        in_specs=[tok(D_MODEL), tok(D_MODEL), pl.BlockSpec((1, 8, D_MODEL), lambda b, i: (b, 0, 0)),
                  tok(A_WIDTH), fold(1), fold(2), tok(A_WIDTH), fold(1), fold(2),
                  tok(B_WIDTH), tok(C_WIDTH), tok(D_WIDTH)] + [full(w) for w in weights],
        out_specs=[tok(D_MODEL), pl.BlockSpec((tm * ROW_TILE, 128), lambda b, i: (b * nt + i, 0)), tok(128)],
        out_shape=[
            jax.ShapeDtypeStruct((bsz, seq, D_MODEL), F32),
            jax.ShapeDtypeStruct((bsz * seq * ROW_TILE, 128), F32),
            jax.ShapeDtypeStruct((bsz, seq, 128), F32),
        ],
        scratch_shapes=[pltpu.VMEM((A_WIDTH // 128, tm, 128), F32)] * 4,
        compiler_params=_cparams(("parallel", "parallel")),
        name="merge",
    )(x, h, ada8, oa[0], oa[1], oa[2], la[0], la[1], la[2], yb, yc, s5, *weights)


def _moe_plan(eid, tm):
    ntok = eid.shape[0]
    npair = 2 * ntok
    blk = 256
    e_flat = jnp.concatenate([eid[:, 0], eid[:, 1]])
    experts = jnp.arange(N_EXPERTS, dtype=jnp.int32)
    oh = (e_flat[:, None] == experts[None, :]).astype(F32).reshape(npair // blk, blk, N_EXPERTS)
    tri = (jnp.arange(blk)[:, None] >= jnp.arange(blk)[None, :]).astype(F32)
    within = jnp.einsum("ij,bjk->bik", tri, oh, precision=HIGHEST)
    btot = within[:, -1, :]
    before = jnp.cumsum(btot, axis=0) - btot
    rank = (within - 1.0 + before[:, None, :]).reshape(npair, N_EXPERTS)
    oh = oh.reshape(npair, N_EXPERTS)
    counts = jnp.sum(btot, axis=0).astype(jnp.int32)
    pcounts = (counts + tm - 1) // tm * tm
    pend = jnp.cumsum(pcounts)
    pstart = pend - pcounts
    pos = jnp.sum(oh * (rank + pstart.astype(F32)[None, :]), axis=1).astype(jnp.int32)
    nrow = npair + N_EXPERTS * tm
    ntile = nrow // tm
    tile_expert = jnp.sum((pend[None, :] <= (jnp.arange(ntile, dtype=jnp.int32) * tm)[:, None]).astype(jnp.int32), axis=1)
    tile_expert = jnp.minimum(tile_expert, N_EXPERTS - 1)
    n_used = (pend[-1] // tm).astype(jnp.int32).reshape(1)
    padcnt = pcounts - counts
    padend = jnp.cumsum(padcnt)
    j = jnp.arange(N_EXPERTS * tm, dtype=jnp.int32)
    e_j = jnp.sum((padend[None, :] <= j[:, None]).astype(jnp.int32), axis=1)
    shift = pstart + counts - (padend - padcnt)
    row_pad = jnp.sum((e_j[:, None] == experts[None, :]).astype(jnp.int32) * shift[None, :], axis=1) + j
    zero_rows = jnp.where(e_j < N_EXPERTS, row_pad, pend[-1] + j - padend[-1])
    return pos, zero_rows, tile_expert, n_used


def _dispatch_kernel(pos_ref, zr_ref, h_ref, xs_ref, zbuf, sem, *, ntok, tm, nz):
    i = pl.program_id(0)

    def issue(r, carry):
        src = h_ref.at[pl.ds(pl.multiple_of(r * ROW_TILE, ROW_TILE), ROW_TILE)]
        for k in range(2):
            row = pos_ref[k * ntok + i * tm + r]
            dst = xs_ref.at[pl.ds(pl.multiple_of(row * ROW_TILE, ROW_TILE), ROW_TILE)]
            pltpu.make_async_copy(src, dst, sem).start(priority=k)
        return carry

    lax.fori_loop(0, tm, issue, 0, unroll=ISSUE_UNROLL)
    zbuf[...] = jnp.zeros_like(zbuf)

    def zero(r, carry):
        for k in range(2):
            row = zr_ref[i * nz + 2 * r + k]
            dst = xs_ref.at[pl.ds(pl.multiple_of(row * ROW_TILE, ROW_TILE), ROW_TILE)]
            pltpu.make_async_copy(zbuf, dst, sem).start(priority=k)
        return carry

    lax.fori_loop(0, nz // 2, zero, 0)
    for _ in range(2):
        pltpu.make_async_copy(h_ref, xs_ref.at[pl.ds(0, tm * ROW_TILE)], sem).wait()
    pltpu.make_async_copy(h_ref.at[pl.ds(0, nz * ROW_TILE)], xs_ref.at[pl.ds(0, nz * ROW_TILE)], sem).wait()


def _dispatch_call(h2t, pos, zero_rows, nrow, tm):
    ntok = pos.shape[0] // 2
    nstep = ntok // tm
    nz = zero_rows.shape[0] // nstep
    grid_spec = pltpu.PrefetchScalarGridSpec(
        num_scalar_prefetch=2,
        grid=(nstep,),
        in_specs=[pl.BlockSpec((tm * ROW_TILE, 128), lambda i, p, z: (i, 0))],
        out_specs=pl.BlockSpec(memory_space=pl.ANY),
        scratch_shapes=[pltpu.VMEM((ROW_TILE, 128), F32), pltpu.SemaphoreType.DMA(())],
    )
    return pl.pallas_call(
        functools.partial(_dispatch_kernel, ntok=ntok, tm=tm, nz=nz),
        grid_spec=grid_spec,
        out_shape=jax.ShapeDtypeStruct((nrow * ROW_TILE, 128), F32),
        compiler_params=_cparams(("arbitrary",)),
        name="moe_dispatch",
    )(pos, zero_rows, h2t)


def _ffn_kernel(te_ref, nu_ref, xs_ref, wup_ref, wdn_ref, ys_ref, *, tm):
    del te_ref
    i = pl.program_id(0)

    @pl.when(i < nu_ref[0])
    def _():
        x = jnp.concatenate([xs_ref[pl.ds(s, tm, stride=ROW_TILE), :] for s in range(ROW_TILE)], axis=1)
        a = jnp.dot(x.astype(BF16), wup_ref[0], preferred_element_type=F32)
        g = a[:, :MOE_FF]
        hid = g * _sigmoid(g) * a[:, MOE_FF:]
        y = jnp.dot(hid.astype(BF16), wdn_ref[0], preferred_element_type=F32)
        for s in range(ROW_TILE):
            ys_ref[pl.ds(s, tm, stride=ROW_TILE), :] = y[:, s * 128:(s + 1) * 128]

    @pl.when(i >= nu_ref[0])
    def _():
        ys_ref[...] = jnp.zeros_like(ys_ref)


def _ffn_call(xs, tile_expert, n_used, w_up, w_down):
    tm = MOE_TM
    ntile = tile_expert.shape[0]
    grid_spec = pltpu.PrefetchScalarGridSpec(
        num_scalar_prefetch=2,
        grid=(ntile,),
        in_specs=[
            pl.BlockSpec((tm * ROW_TILE, 128), lambda i, te, nu: (i, 0)),
            pl.BlockSpec((1, D_MODEL, 2 * MOE_FF), lambda i, te, nu: (te[i], 0, 0)),
            pl.BlockSpec((1, MOE_FF, D_MODEL), lambda i, te, nu: (te[i], 0, 0)),
        ],
        out_specs=pl.BlockSpec((tm * ROW_TILE, 128), lambda i, te, nu: (i, 0)),
    )
    return pl.pallas_call(
        functools.partial(_ffn_kernel, tm=tm),
        grid_spec=grid_spec,
        out_shape=jax.ShapeDtypeStruct(xs.shape, F32),
        compiler_params=_cparams(("arbitrary",)),
        name="moe_ffn",
    )(tile_expert, n_used, xs, w_up, w_down)


def _final_kernel(pos_ref, x1_ref, route_ref, ada_ref, lng_ref, lnb_ref, ys_ref, o_ref, buf, sem,
                  *, ntok, tm, nstep):
    i = pl.program_id(0)
    slot = i % 2

    def fetch(step, to_slot):
        def issue(r, carry):
            for k in range(2):
                row = pos_ref[k * ntok + step * tm + r]
                src = ys_ref.at[pl.ds(pl.multiple_of(row * ROW_TILE, ROW_TILE), ROW_TILE)]
                dst = buf.at[to_slot, k, pl.ds(pl.multiple_of(r * ROW_TILE, ROW_TILE), ROW_TILE)]
                pltpu.make_async_copy(src, dst, sem.at[to_slot]).start(priority=k)
            return carry

        lax.fori_loop(0, tm, issue, 0, unroll=ISSUE_UNROLL)

    @pl.when(i == 0)
    def _():
        fetch(0, 0)

    @pl.when(i + 1 < nstep)
    def _():
        fetch(i + 1, 1 - slot)

    for k in range(2):
        pltpu.make_async_copy(ys_ref.at[pl.ds(0, tm * ROW_TILE)], buf.at[slot, k], sem.at[slot]).wait()
    ada = ada_ref[0]
    route = route_ref[0]
    w0, w1 = route[:, 2:3], route[:, 3:4]
    ffn = jnp.concatenate(
        [w0 * buf[slot, 0, pl.ds(s, tm, stride=ROW_TILE), :] + w1 * buf[slot, 1, pl.ds(s, tm, stride=ROW_TILE), :]
         for s in range(ROW_TILE)], axis=1)
    o_ref[0] = _ln(ALPHA * x1_ref[0] + (1.0 + ada[5:6]) * ffn) * lng_ref[...] + lnb_ref[...]


def _final_call(x1, ys, pos, route, ada8, ln_g, ln_b, tm):
    bsz, seq, _ = x1.shape
    nt = seq // tm
    nstep = bsz * nt
    grid_spec = pltpu.PrefetchScalarGridSpec(
        num_scalar_prefetch=1,
        grid=(nstep,),
        in_specs=[
            pl.BlockSpec((1, tm, D_MODEL), lambda i, p: (i // nt, i % nt, 0)),
            pl.BlockSpec((1, tm, 128), lambda i, p: (i // nt, i % nt, 0)),
            pl.BlockSpec((1, 8, D_MODEL), lambda i, p: (i // nt, 0, 0)),
            pl.BlockSpec((1, D_MODEL), lambda i, p: (0, 0)),
            pl.BlockSpec((1, D_MODEL), lambda i, p: (0, 0)),
            pl.BlockSpec(memory_space=pl.ANY),
        ],
        out_specs=pl.BlockSpec((1, tm, D_MODEL), lambda i, p: (i // nt, i % nt, 0)),
        scratch_shapes=[pltpu.VMEM((2, 2, tm * ROW_TILE, 128), F32), pltpu.SemaphoreType.DMA((2,))],
    )
    return pl.pallas_call(
        functools.partial(_final_kernel, ntok=bsz * seq, tm=tm, nstep=nstep),
        grid_spec=grid_spec,
        out_shape=jax.ShapeDtypeStruct((bsz, seq, D_MODEL), F32),
        compiler_params=_cparams(("arbitrary",)),
        name="final",
    )(pos, x1, route, ada8, ln_g, ln_b, ys)


def _split_w_in(w_in):
    sizes = (768, 768, 768, 512, 128, 128, 256, 256, 256, 256, N_BRANCH * D_MODEL)
    pts = [0]
    for s in sizes:
        pts.append(pts[-1] + s)
    return [w_in[:, pts[n]:pts[n + 1]] for n in range(len(sizes))]


def _hi_lo(w):
    hi = w.astype(BF16)
    return jnp.stack([hi, (w - hi.astype(F32)).astype(BF16)])


def _layer_weights(p, l):
    a_q, a_k, a_v, b_q, b_k, b_v, c_q, c_k, c_v, d_u, w_gate = _split_w_in(p["w_in"][l])
    scale = HEAD_DIM ** -0.5
    def a_cols(g):
        gs = slice(g * A_WIDTH, (g + 1) * A_WIDTH)
        return [a_q[:, gs] * scale, a_k[:, gs], a_v[:, gs]]

    cols = a_cols(0) + [b_k, b_v, b_q * scale, c_q * scale, c_k, c_v, d_u] + a_cols(1) + a_cols(2)
    w_route = jnp.concatenate([p["w_route_g"][l], p["w_route_e"][l]], axis=1)
    b_route = jnp.concatenate([p["b_route_g"][l], p["b_route_e"][l]])
    npad = 128 - w_route.shape[1]
    return {
        "w_qkv": jnp.concatenate(cols, axis=1).astype(BF16),
        "w_gate": w_gate.astype(BF16),
        "b_gate": p["b_gate"][l].reshape(1, -1),
        "w_branch": p["w_branch"][l].astype(BF16),
        "w_out": p["w_out"][l].astype(BF16),
        "w_glu": p["w_glu"][l].astype(BF16),
        "b_glu": p["b_glu"][l].reshape(1, -1),
        "ln1_g": p["ln1_g"][l].reshape(1, -1),
        "ln1_b": p["ln1_b"][l].reshape(1, -1),
        "ln2_g": p["ln2_g"][l].reshape(1, -1),
        "ln2_b": p["ln2_b"][l].reshape(1, -1),
        "w_route": _hi_lo(jnp.pad(w_route, ((0, 0), (0, npad)))),
        "b_route": jnp.pad(b_route, (0, npad)).reshape(1, -1),
        "w_up": p["w_up"][l].astype(BF16),
        "w_down": p["w_down"][l].astype(BF16),
        "b_sink": p["b_sink"][l],
        "c_bias": _c_bias_table(p["c_rpb"][l]),
    }


def _layer(x, ada8, lw, ssm_prm):
    bsz, seq, _ = x.shape
    tm, tm_mm = 256, 512
    h, main, fold1, fold2 = _proj_call(x, ada8, lw["w_qkv"], tm_mm)
    oa, la = zip(*[_attn_a_call(f, g, seq) for g, f in enumerate((main, fold1, fold2))])
    yb = _attn_b_call(main, lw["b_sink"])
    yc = _attn_c_call(main, lw["c_bias"])
    s5 = _ssm_call(main, ssm_prm)
    x1, h2t, route = _merge_call(x, h, ada8, oa, la, yb, yc, s5, lw, tm_mm)
    ntok = bsz * seq
    eid = route.reshape(ntok, 128)[:, :2].astype(jnp.int32)
    pos, zero_rows, tile_expert, n_used = _moe_plan(eid, MOE_TM)
    xs = _dispatch_call(h2t, pos, zero_rows, 2 * ntok + N_EXPERTS * MOE_TM, tm)
    ys = _ffn_call(xs, tile_expert, n_used, lw["w_up"], lw["w_down"])
    return _final_call(x1, ys, pos, route, ada8, lw["ln2_g"], lw["ln2_b"], tm)


def kernel(x_prompt, x_sample, c_prompt, c_sample, w_ada, b_ada, w_in, b_gate, b_sink, c_rpb, lam_re, lam_im,
           log_dt, ssm_b_re, ssm_b_im, ssm_c_re, ssm_c_im, ssm_d, w_glu, b_glu, w_branch, w_out, ln1_g, ln1_b,
           ln2_g, ln2_b, w_route_g, b_route_g, w_route_e, b_route_e, w_up, w_down):
    p = dict(w_in=w_in, b_gate=b_gate, b_sink=b_sink, c_rpb=c_rpb, w_glu=w_glu, b_glu=b_glu, w_branch=w_branch,
             w_out=w_out, ln1_g=ln1_g, ln1_b=ln1_b, ln2_g=ln2_g, ln2_b=ln2_b, w_route_g=w_route_g,
             b_route_g=b_route_g, w_route_e=w_route_e, b_route_e=b_route_e, w_up=w_up, w_down=w_down)
    nbp, nbs = c_prompt.shape[0], c_sample.shape[0]
    c_all = jnp.concatenate([c_prompt, c_sample], axis=0)
    c_all = jnp.pad(c_all, ((0, -c_all.shape[0] % 8), (0, 0)))
    ada = _ada_call(c_all, w_ada, b_ada).reshape(DEPTH, c_all.shape[0], 6, D_MODEL)
    ada = jnp.pad(ada, ((0, 0), (0, 0), (0, 2), (0, 0)))
    m_mat, bend, cin, apow = _ssm_param_call(lam_re, lam_im, log_dt, ssm_b_re, ssm_b_im, ssm_c_re, ssm_c_im, ssm_d)
    xs = [x_prompt, x_sample]
    for l in range(DEPTH):
        lw = _layer_weights(p, l)
        prm = (m_mat[l * SSM_GROUPS:(l + 1) * SSM_GROUPS], bend[2 * l:2 * l + 2], cin[2 * l:2 * l + 2],
               apow[2 * l:2 * l + 2])
        xs = [_layer(xs[0], ada[l, :nbp], lw, prm), _layer(xs[1], ada[l, nbp:nbp + nbs], lw, prm)]
    return (xs[0], xs[1])
```
